```python
import math
import jax, jax.numpy as jnp
from jax import lax
import numpy as np

D_MODEL = 1024
BATCH = 2
SEQ = 8192
DEPTH = 4

N_A_LAYERS = DEPTH // 2
N_B_LAYERS = DEPTH - N_A_LAYERS
N_DENSE = (DEPTH + 1) // 2
N_MOE = DEPTH // 2

HEAD_DIM = 64
N_HEADS = D_MODEL // HEAD_DIM
LORA_DECAY = 64
LORA_ICLR = 64
LORA_VRES = 32
LORA_GATE = 160
GN_EPS = 64e-5
LN_EPS = 1e-5

MOBA_BLOCK = 256
MOBA_TOPK = 3
QUERY_CHUNK = 64

D_FF = 2816
N_EXPERTS = 8
TOP_K_EXPERTS = 2
D_FF_EXPERT = 3584

DEEPNORM_ALPHA = (2.0 * DEPTH) ** 0.25
DEEPNORM_BETA = (8.0 * DEPTH) ** -0.25

kernel_name = 'rwkv7_moba_yoco_deepnorm_moe'


def _heads(t):
    return t.reshape(t.shape[0], t.shape[1], N_HEADS, HEAD_DIM)


def layer_norm(x, g, b):
    xf = x.astype(jnp.float32)
    mu = jnp.mean(xf, axis=-1, keepdims=True)
    var = jnp.mean(jnp.square(xf - mu), axis=-1, keepdims=True)
    y = (xf - mu) * lax.rsqrt(var + LN_EPS)
    return (y * g + b).astype(x.dtype)


def deepnorm_residual(x, sub, g, b):
    return layer_norm(DEEPNORM_ALPHA * x + sub, g, b)


def wkv7_scan(r, decay, k, v, a, b):
    bsz, T, H, N = r.shape
    xs = tuple(jnp.moveaxis(t.astype(jnp.float32), 1, 0) for t in (r, decay, k, v, a, b))

    def step(S, inp):
        r_t, w_t, k_t, v_t, a_t, b_t = inp
        sa = jnp.einsum('bhvk,bhk->bhv', S, a_t)
        S = (S * w_t[:, :, None, :] + sa[..., None] * b_t[:, :, None, :]
             + v_t[..., None] * k_t[:, :, None, :])
        return S, jnp.einsum('bhvk,bhk->bhv', S, r_t)

    S0 = jnp.zeros((bsz, H, N, N), jnp.float32)
    _, y = lax.scan(step, S0, xs)
    return jnp.moveaxis(y, 0, 1)


def rwkv7_time_mix(x, v_first, mu, w_rkv, w_out, decay_w0, decay_w1, decay_w2,
                   iclr_a0, iclr_a1, iclr_a2, vres_v0, vres_v1, vres_v2,
                   gate_g1, gate_g2, k_k, k_a, r_k, gn_g, gn_b):
    bsz, T, D = x.shape
    x_prev = jnp.pad(x[:, :-1], ((0, 0), (1, 0), (0, 0)))
    xx = x_prev - x
    xm = x[None] + xx[None] * mu[:, None, None, :]
    rkv = jnp.einsum('cbtd,cde->cbte', xm[:3], w_rkv)
    r, k, v = rkv[0], rkv[1], rkv[2]
    xw, xa, xg = xm[3], xm[4], xm[5]
    w = -jax.nn.softplus(-(decay_w0 + jnp.tanh(xw @ decay_w1) @ decay_w2)) - 0.5
    if vres_v0 is None:
        v_first = v
    else:
        v = v + (v_first - v) * jax.nn.sigmoid(vres_v0 + (xm[2] @ vres_v1) @ vres_v2)
    a = jax.nn.sigmoid(iclr_a0 + (xa @ iclr_a1) @ iclr_a2)
    g = jax.nn.sigmoid(xg @ gate_g1) @ gate_g2
    kk = _heads((k * k_k).astype(jnp.float32))
    kk = kk * lax.rsqrt(jnp.maximum(jnp.sum(kk * kk, axis=-1, keepdims=True), 1e-24))
    k = k * (1 + (a - 1) * k_a)
    decay = jnp.exp(-jnp.exp(w.astype(jnp.float32)))
    a_h = _heads(a.astype(jnp.float32))
    y = wkv7_scan(_heads(r), _heads(decay), _heads(k), _heads(v), -kk, kk * a_h)
    mean = jnp.mean(y, axis=-1, keepdims=True)
    var = jnp.mean(jnp.square(y - mean), axis=-1, keepdims=True)
    y = ((y - mean) * lax.rsqrt(var + GN_EPS)).reshape(bsz, T, D) * gn_g + gn_b
    bonus = jnp.sum(_heads(r) * _heads(k) * r_k, axis=-1, keepdims=True) * _heads(v)
    y = y + bonus.reshape(bsz, T, D)
    out = (y * g) @ w_out
    return out.astype(x.dtype), v_first


def moba_shared_kv(h, w_k, w_v):
    bsz, T, D = h.shape
    nb = -(-T // MOBA_BLOCK)
    pad = nb * MOBA_BLOCK - T

    def blocks(t):
        t = _heads(t).transpose(0, 2, 1, 3)
        t = jnp.pad(t, ((0, 0), (0, 0), (0, pad), (0, 0)))
        return t.reshape(bsz, N_HEADS, nb, MOBA_BLOCK, HEAD_DIM)

    k_blocks = blocks(h @ w_k)
    v_blocks = blocks(h @ w_v)
    k_means = jnp.mean(k_blocks.astype(jnp.float32), axis=3)
    return k_blocks, v_blocks, k_means


def moba_attention(x, w_q, w_o, k_blocks, v_blocks, k_means):
    bsz, T, D = x.shape
    nb = k_blocks.shape[2]
    topk = min(MOBA_TOPK, nb)
    scale = HEAD_DIM ** -0.5
    q = _heads(x @ w_q).transpose(0, 2, 1, 3)
    n_chunks = T // QUERY_CHUNK
    q_chunks = q.reshape(bsz, N_HEADS, n_chunks, QUERY_CHUNK, HEAD_DIM).transpose(2, 0, 1, 3, 4)
    gather_blocks = jax.vmap(jax.vmap(lambda blk_arr, idx: blk_arr[idx]))

    def attend_chunk(args):
        c, q_c = args
        start = c * QUERY_CHUNK
        own = start // MOBA_BLOCK
        q_pos = start + jnp.arange(QUERY_CHUNK)
        gate = jnp.einsum('bhqd,bhnd->bhqn', q_c.astype(jnp.float32), k_means)
        gate = jnp.where(jnp.arange(nb) < own, gate, -jnp.inf)
        _, sel = lax.top_k(gate, topk)
        k_sel = gather_blocks(k_blocks, sel)
        v_sel = gather_blocks(v_blocks, sel)
        s_sel = jnp.einsum('bhqd,bhqsjd->bhqsj', q_c, k_sel).astype(jnp.float32) * scale
        slot_ok = jnp.arange(topk) < own
        s_sel = jnp.where(slot_ok[:, None], s_sel, -jnp.inf)
        s_sel = s_sel.reshape(bsz, N_HEADS, QUERY_CHUNK, topk * MOBA_BLOCK)
        k_own = lax.dynamic_index_in_dim(k_blocks, own, axis=2, keepdims=False)
        v_own = lax.dynamic_index_in_dim(v_blocks, own, axis=2, keepdims=False)
        s_own = jnp.einsum('bhqd,bhkd->bhqk', q_c, k_own).astype(jnp.float32) * scale
        k_pos = own * MOBA_BLOCK + jnp.arange(MOBA_BLOCK)
        s_own = jnp.where(k_pos[None, :] <= q_pos[:, None], s_own, -jnp.inf)
        p = jax.nn.softmax(jnp.concatenate([s_sel, s_own], axis=-1), axis=-1)
        p_sel = p[..., :topk * MOBA_BLOCK].reshape(bsz, N_HEADS, QUERY_CHUNK, topk, MOBA_BLOCK)
        p_own = p[..., topk * MOBA_BLOCK:]
        o = (jnp.einsum('bhqsj,bhqsjd->bhqd', p_sel.astype(v_sel.dtype), v_sel)
             + jnp.einsum('bhqk,bhkd->bhqd', p_own.astype(v_own.dtype), v_own))
        return o

    o = lax.map(attend_chunk, (jnp.arange(n_chunks), q_chunks))
    o = o.transpose(1, 0, 3, 2, 4).reshape(bsz, T, D)
    return (o @ w_o).astype(x.dtype)


def swiglu(x, w_gate, w_up, w_down):
    return (jax.nn.silu(x @ w_gate) * (x @ w_up)) @ w_down


def moe_swiglu(x, router, w_gate, w_up, w_down):
    logits = (x @ router).astype(jnp.float32)
    top_logits, top_idx = lax.top_k(logits, TOP_K_EXPERTS)
    top_w = jax.nn.softmax(top_logits, axis=-1)
    gates = jnp.sum(jax.nn.one_hot(top_idx, N_EXPERTS, dtype=jnp.float32) * top_w[..., None], axis=-2)
    gates = gates.astype(x.dtype)
    out = jnp.zeros_like(x)
    for e in range(N_EXPERTS):
        out = out + gates[..., e:e + 1] * swiglu(x, w_gate[e], w_up[e], w_down[e])
    return out


def setup_inputs(seed: int = 0) -> dict:
    key = jax.random.key(seed)
    ks = iter(jax.random.split(key, 48))
    D, H, N = D_MODEL, N_HEADS, HEAD_DIM
    f32 = jnp.float32

    def nrm(shape, scale):
        return jax.random.normal(next(ks), shape, f32) * scale

    def unif(shape, lo, hi):
        return jax.random.uniform(next(ks), shape, f32, lo, hi)

    beta = DEEPNORM_BETA
    nA, nV = N_A_LAYERS, max(N_A_LAYERS - 1, 0)
    rkv_scale = jnp.array([1.0, 1.0, beta], f32)[:, None, None]
    return {
        'x': nrm((BATCH, SEQ, D), 1.0),
        'rwkv_mu': unif((nA, 6, D), 0.0, 1.0),
        'rwkv_w_rkv': nrm((nA, 3, D, D), D ** -0.5) * rkv_scale,
        'rwkv_w_out': nrm((nA, D, D), D ** -0.5 * beta),
        'rwkv_decay_w0': unif((nA, D), -6.0, -1.0),
        'rwkv_decay_w1': nrm((nA, D, LORA_DECAY), D ** -0.5),
        'rwkv_decay_w2': nrm((nA, LORA_DECAY, D), 0.1 * LORA_DECAY ** -0.5),
        'rwkv_iclr_a0': nrm((nA, D), 0.1),
        'rwkv_iclr_a1': nrm((nA, D, LORA_ICLR), D ** -0.5),
        'rwkv_iclr_a2': nrm((nA, LORA_ICLR, D), 0.1 * LORA_ICLR ** -0.5),
        'rwkv_vres_v0': 1.0 + nrm((nV, D), 0.1),
        'rwkv_vres_v1': nrm((nV, D, LORA_VRES), D ** -0.5),
        'rwkv_vres_v2': nrm((nV, LORA_VRES, D), 0.1 * LORA_VRES ** -0.5),
        'rwkv_gate_g1': nrm((nA, D, LORA_GATE), D ** -0.5),
        'rwkv_gate_g2': nrm((nA, LORA_GATE, D), LORA_GATE ** -0.5),
        'rwkv_k_k': 0.85 + nrm((nA, D), 0.05),
        'rwkv_k_a': 1.0 + nrm((nA, D), 0.05),
        'rwkv_r_k': nrm((nA, H, N), 0.1),
        'rwkv_gn_g': 1.0 + nrm((nA, D), 0.05),
        'rwkv_gn_b': nrm((nA, D), 0.01),
        'moba_w_k': nrm((D, D), D ** -0.5),
        'moba_w_v': nrm((D, D), D ** -0.5 * beta),
        'moba_w_q': nrm((N_B_LAYERS, D, D), D ** -0.5),
        'moba_w_o': nrm((N_B_LAYERS, D, D), D ** -0.5 * beta),
        'ffn_w_gate': nrm((N_DENSE, D, D_FF), D ** -0.5),
        'ffn_w_up': nrm((N_DENSE, D, D_FF), D ** -0.5),
        'ffn_w_down': nrm((N_DENSE, D_FF, D), D_FF ** -0.5 * beta),
        'moe_router': nrm((N_MOE, D, N_EXPERTS), D ** -0.5),
        'moe_w_gate': nrm((N_MOE, N_EXPERTS, D, D_FF_EXPERT), D ** -0.5),
        'moe_w_up': nrm((N_MOE, N_EXPERTS, D, D_FF_EXPERT), D ** -0.5),
        'moe_w_down': nrm((N_MOE, N_EXPERTS, D_FF_EXPERT, D), D_FF_EXPERT ** -0.5 * beta),
        'ln_g': 1.0 + nrm((DEPTH, 2, D), 0.05),
        'ln_b': nrm((DEPTH, 2, D), 0.01),
    }


def reference(x, rwkv_mu, rwkv_w_rkv, rwkv_w_out, rwkv_decay_w0, rwkv_decay_w1, rwkv_decay_w2,
              rwkv_iclr_a0, rwkv_iclr_a1, rwkv_iclr_a2, rwkv_vres_v0, rwkv_vres_v1, rwkv_vres_v2,
              rwkv_gate_g1, rwkv_gate_g2, rwkv_k_k, rwkv_k_a, rwkv_r_k, rwkv_gn_g, rwkv_gn_b,
              moba_w_k, moba_w_v, moba_w_q, moba_w_o,
              ffn_w_gate, ffn_w_up, ffn_w_down,
              moe_router, moe_w_gate, moe_w_up, moe_w_down,
              ln_g, ln_b):
    h = x
    v_first = None
    kv = None
    for layer in range(DEPTH):
        if layer < N_A_LAYERS:
            i = layer
            if i == 0:
                v0, v1, v2 = None, None, None
            else:
                v0, v1, v2 = rwkv_vres_v0[i - 1], rwkv_vres_v1[i - 1], rwkv_vres_v2[i - 1]
            mix, v_first = rwkv7_time_mix(
                h, v_first, rwkv_mu[i], rwkv_w_rkv[i], rwkv_w_out[i],
                rwkv_decay_w0[i], rwkv_decay_w1[i], rwkv_decay_w2[i],
                rwkv_iclr_a0[i], rwkv_iclr_a1[i], rwkv_iclr_a2[i], v0, v1, v2,
                rwkv_gate_g1[i], rwkv_gate_g2[i], rwkv_k_k[i], rwkv_k_a[i], rwkv_r_k[i],
                rwkv_gn_g[i], rwkv_gn_b[i])
        else:
            j = layer - N_A_LAYERS
            k_blocks, v_blocks, k_means = kv
            mix = moba_attention(h, moba_w_q[j], moba_w_o[j], k_blocks, v_blocks, k_means)
        h = deepnorm_residual(h, mix, ln_g[layer, 0], ln_b[layer, 0])
        if layer % 2 == 0:
            e = layer // 2
            ffn = swiglu(h, ffn_w_gate[e], ffn_w_up[e], ffn_w_down[e])
        else:
            e = layer // 2
            ffn = moe_swiglu(h, moe_router[e], moe_w_gate[e], moe_w_up[e], moe_w_down[e])
        h = deepnorm_residual(h, ffn, ln_g[layer, 1], ln_b[layer, 1])
        if layer == N_A_LAYERS - 1:
            kv = moba_shared_kv(h, moba_w_k, moba_w_v)
    return h
```

```python
import functools
import math

import jax
import jax.numpy as jnp
from jax import lax
from jax.experimental import pallas as pl
from jax.experimental.pallas import tpu as pltpu

HEAD_DIM = 64
LANES = 128
GN_EPS = 64e-5
LN_EPS = 1e-5
MOBA_BLOCK = 256
MOBA_TOPK = 3
N_EXPERTS = 8
WKV_CHUNK = 64
NEG_BIG = -1e30

F32 = jnp.float32
BF16 = jnp.bfloat16
VMEM_LIMIT = 56 * 1024 * 1024


def _cparams(sem):
    return pltpu.CompilerParams(dimension_semantics=sem, vmem_limit_bytes=VMEM_LIMIT)


def _dot(a, b):
    return jnp.dot(a, b, preferred_element_type=F32)


def _dot_nt(a, b):
    return lax.dot_general(a, b, (((1,), (1,)), ((), ())), preferred_element_type=F32)


def _split(x):
    hi = x.astype(BF16)
    lo = (x - hi.astype(F32)).astype(BF16)
    return hi, lo


def _dot_sp(a, b, nt=False):
    d = _dot_nt if nt else _dot
    return d(a[0], b[0]) + (d(a[0], b[1]) + d(a[1], b[0]))


def _dot_exact_rhs(a, b_exact, nt=False):
    d = _dot_nt if nt else _dot
    hi = a.astype(BF16)
    r1 = a - hi.astype(F32)
    mid = r1.astype(BF16)
    lo = (r1 - mid.astype(F32)).astype(BF16)
    return d(hi, b_exact) + (d(mid, b_exact) + d(lo, b_exact))


def _sigmoid(x):
    return 1.0 / (1.0 + jnp.exp(-x))


def _layer_norm(y, g, b):
    mu = jnp.mean(y, axis=-1, keepdims=True)
    yc = y - mu
    var = jnp.mean(yc * yc, axis=-1, keepdims=True)
    return yc * lax.rsqrt(var + LN_EPS) * g + b


def _head_blockdiag(n):
    i = jnp.arange(n) // HEAD_DIM
    return (i[:, None] == i[None, :]).astype(BF16)


def _tile(n, want):
    t = min(n, want)
    assert n % t == 0, (n, want)
    return t


def _rwkv_proj_kernel(has_vres, steps_per_seq, *refs):
    if has_vres:
        (x_ref, xp_ref, mu_ref, wrkv_ref, wd1_ref, wa1_ref, wg1_ref, wv1_ref,
         wd2_ref, wa2_ref, wg2_ref, wv2_ref, w0_ref, a0_ref, v0_ref, kk_ref, ka_ref,
         bd_ref, vf_ref,
         r_out, e_out, k_out, v_out, a_out, b_out, g_out,
         xm_s, hd_s, ha_s, hg_s, hv_s) = refs
    else:
        (x_ref, xp_ref, mu_ref, wrkv_ref, wd1_ref, wa1_ref, wg1_ref,
         wd2_ref, wa2_ref, wg2_ref, w0_ref, a0_ref, kk_ref, ka_ref,
         bd_ref,
         r_out, e_out, k_out, v_out, a_out, b_out, g_out,
         xm_s, hd_s, ha_s, hg_s) = refs
    i = pl.program_id(0)
    j = pl.program_id(1)

    @pl.when(j == 0)
    def _():
        x = x_ref[...]
        tm = x.shape[0]
        prev_row = jnp.where(i % steps_per_seq == 0, 0.0, xp_ref[7:8, :])
        rolled = pltpu.roll(x, 1, axis=0)
        row = lax.broadcasted_iota(jnp.int32, (tm, 1), 0)
        x_prev = jnp.where(row == 0, prev_row, rolled)
        xx = x_prev - x
        for c in range(3):
            xm_s[c] = (x + xx * mu_ref[c:c + 1, :]).astype(BF16)
        xw = (x + xx * mu_ref[3:4, :]).astype(BF16)
        xa = (x + xx * mu_ref[4:5, :]).astype(BF16)
        xg = (x + xx * mu_ref[5:6, :]).astype(BF16)
        hd_s[...] = jnp.tanh(_dot(xw, wd1_ref[...])).astype(BF16)
        ha_s[...] = _dot(xa, wa1_ref[...]).astype(BF16)
        hg_s[...] = _sigmoid(_dot(xg, wg1_ref[...])).astype(BF16)
        if has_vres:
            hv_s[...] = _dot(xm_s[2], wv1_ref[...]).astype(BF16)

    r = _dot(xm_s[0], wrkv_ref[0])
    k = _dot(xm_s[1], wrkv_ref[1])
    v = _dot(xm_s[2], wrkv_ref[2])
    z = w0_ref[...] + _dot(hd_s[...], wd2_ref[...])
    nz = -z
    softplus = jnp.maximum(nz, 0.0) + jnp.log(1.0 + jnp.exp(-jnp.abs(nz)))
    e = jnp.exp(-softplus - 0.5)
    a = _sigmoid(a0_ref[...] + _dot(ha_s[...], wa2_ref[...]))
    g = _dot(hg_s[...], wg2_ref[...])
    n_slab = r.shape[1] // LANES
    if has_vres:
        vf = jnp.concatenate([vf_ref[q] for q in range(n_slab)], axis=1)
        v = v + (vf - v) * _sigmoid(v0_ref[...] + _dot(hv_s[...], wv2_ref[...]))
    kk = k * kk_ref[...]
    ss = _dot_exact_rhs(kk * kk, bd_ref[...])
    kk = kk * lax.rsqrt(jnp.maximum(ss, 1e-24))
    k = k * (1.0 + (a - 1.0) * ka_ref[...])
    for q in range(n_slab):
        sl = slice(q * LANES, (q + 1) * LANES)
        r_out[q] = r[:, sl]
        e_out[q] = e[:, sl]
        k_out[q] = k[:, sl]
        v_out[q] = v[:, sl]
        a_out[q] = -kk[:, sl]
        b_out[q] = (kk * a)[:, sl]
        g_out[q] = g[:, sl]


def _rwkv_proj(x, seq_len, mu, w_rkv, wd1, wa1, wg1, wd2, wa2, wg2, w0, a0, k_k, k_a,
               vres=None, v_first=None):
    m, d = x.shape
    tm = _tile(seq_len, 512)
    tn = _tile(d, 256)
    n_slab = tn // LANES
    has_vres = vres is not None
    row = lambda a: a.reshape(1, d)
    full = lambda a: pl.BlockSpec(a.shape, lambda i, j: (0,) * a.ndim)
    colblk = lambda rows: pl.BlockSpec((rows, tn), lambda i, j: (0, j))
    bd = _head_blockdiag(tn)
    wd1, wa1, wg1 = wd1.astype(BF16), wa1.astype(BF16), wg1.astype(BF16)
    wd2, wa2, wg2 = wd2.astype(BF16), wa2.astype(BF16), wg2.astype(BF16)
    w_rkv = w_rkv.astype(BF16)
    args = [x, x, mu, w_rkv, wd1, wa1, wg1]
    specs = [pl.BlockSpec((tm, d), lambda i, j: (i, 0)),
             pl.BlockSpec((8, d), lambda i, j: (jnp.maximum(i * (tm // 8) - 1, 0), 0)),
             full(mu),
             pl.BlockSpec((3, d, tn), lambda i, j: (0, 0, j)),
             full(wd1), full(wa1), full(wg1)]
    if has_vres:
        v0, wv1, wv2 = vres
        wv1, wv2 = wv1.astype(BF16), wv2.astype(BF16)
        args += [wv1]
        specs += [full(wv1)]
    args += [wd2, wa2, wg2]
    specs += [colblk(wd2.shape[0]), colblk(wa2.shape[0]), colblk(wg2.shape[0])]
    if has_vres:
        args += [wv2]
        specs += [colblk(wv2.shape[0])]
    args += [row(w0), row(a0)]
    specs += [colblk(1), colblk(1)]
    if has_vres:
        args += [row(v0)]
        specs += [colblk(1)]
    args += [row(k_k), row(k_a), bd]
    specs += [colblk(1), colblk(1), full(bd)]
    slab_spec = pl.BlockSpec((n_slab, tm, LANES), lambda i, j: (j, i, 0))
    if has_vres:
        args += [v_first]
        specs += [slab_spec]
    slab = jax.ShapeDtypeStruct((d // LANES, m, LANES), F32)
    scratch = [pltpu.VMEM((3, tm, d), BF16),
               pltpu.VMEM((tm, wd1.shape[1]), BF16),
               pltpu.VMEM((tm, wa1.shape[1]), BF16),
               pltpu.VMEM((tm, wg1.shape[1]), BF16)]
    if has_vres:
        scratch.append(pltpu.VMEM((tm, wv1.shape[1]), BF16))
    return pl.pallas_call(
        functools.partial(_rwkv_proj_kernel, has_vres, seq_len // tm),
        grid=(m // tm, d // tn),
        in_specs=specs,
        out_specs=[slab_spec] * 7,
        out_shape=[slab] * 7,
        scratch_shapes=scratch,
        compiler_params=_cparams(("parallel", "arbitrary")),
        name="rwkv_proj",
    )(*args)


def _wkv_kernel(r_ref, e_ref, k_ref, v_ref, a_ref, b_ref, y_ref, s_ref):
    c = WKV_CHUNK
    n_chunks = r_ref.shape[1] // c

    @pl.when(pl.program_id(2) == 0)
    def _():
        s_ref[...] = jnp.zeros_like(s_ref)

    lane = lax.broadcasted_iota(jnp.int32, (c, LANES), 1)
    t_idx = lax.broadcasted_iota(jnp.int32, (c, LANES), 0)
    head0 = lane < HEAD_DIM
    s_idx = jnp.bitwise_and(lane, HEAD_DIM - 1)
    strict = s_idx < t_idx
    incl = s_idx <= t_idx
    tri = jnp.where(lax.broadcasted_iota(jnp.int32, (c, c), 1)
                    <= lax.broadcasted_iota(jnp.int32, (c, c), 0), 1.0, 0.0).astype(BF16)
    rr = lax.broadcasted_iota(jnp.int32, (LANES, LANES), 0)
    cc = lax.broadcasted_iota(jnp.int32, (LANES, LANES), 1)
    same_head = (rr < HEAD_DIM) == (cc < HEAD_DIM)

    def bd(xs):
        z = jnp.zeros_like(xs[0])
        return tuple(jnp.concatenate([jnp.where(head0, p, z), jnp.where(head0, z, p)], axis=0)
                     for p in xs)

    def cat(a, b):
        return tuple(jnp.concatenate([p, q], axis=0) for p, q in zip(a, b))

    def chunk(ci, carry):
        sl = pl.ds(pl.multiple_of(ci * c, c), c)
        r = r_ref[0, sl, :]
        e = e_ref[0, sl, :]
        k = k_ref[0, sl, :]
        v = v_ref[0, sl, :]
        a = a_ref[0, sl, :]
        b = b_ref[0, sl, :]
        s0 = s_ref[...]
        cum = _cumsum(tri, e)
        tot = cum[c - 1:c, :]
        at = _split(a * jnp.exp(e - cum))
        rt = _split(r * jnp.exp(-cum))
        ec = jnp.exp(cum)
        bt = _split(b * ec)
        kt = _split(k * ec)
        eh = jnp.exp(cum - tot)
        bh = b * eh
        kh = k * eh
        vs = _split(v)
        ss = _split(s0)
        gram = _dot_sp(cat(at, rt), cat(bd(bt), bd(kt)), nt=True)
        zero = jnp.zeros((c, LANES), F32)
        l_ab = jnp.where(strict, gram[:c, :LANES], zero)
        n_ak = jnp.where(strict, gram[:c, LANES:], zero)
        m_rb = jnp.where(incl, gram[c:, :LANES], zero)
        m_rk = jnp.where(incl, gram[c:, LANES:], zero)
        bdv = bd(vs)
        x = _dot_sp(at, ss, nt=True) + _dot_sp(_split(n_ak), bdv)
        lp = l_ab
        n_steps = int(math.log2(c))
        for step in range(n_steps):
            lps = _split(lp)
            x = x + _dot_sp(lps, bd(_split(x)))
            if step + 1 < n_steps:
                lp = _dot_sp(lps, bd(lps))
        us = _split(x)
        y = (_dot_sp(rt, ss, nt=True) + _dot_sp(_split(m_rb), bd(us))
             + _dot_sp(_split(m_rk), bdv))
        y_ref[0, sl, :] = y
        uv_t = jnp.concatenate([x, v], axis=0).T
        bk = jnp.concatenate([bh, kh], axis=0)
        upd = _dot_sp(_split(uv_t), _split(bk))
        s_ref[...] = s0 * jnp.exp(-tot) + jnp.where(same_head, upd, jnp.zeros_like(upd))
        return carry

    lax.fori_loop(0, n_chunks, chunk, 0)


def _cumsum(tri, e):
    hi = e.astype(BF16)
    r1 = e - hi.astype(F32)
    mid = r1.astype(BF16)
    lo = (r1 - mid.astype(F32)).astype(BF16)
    return _dot(tri, hi) + (_dot(tri, mid) + _dot(tri, lo))


def _wkv_scan(r, e, k, v, a, b, batch):
    n_pair, m, _ = r.shape
    seq = m // batch
    tb = _tile(seq, 512)
    steps = seq // tb
    spec = pl.BlockSpec((1, tb, LANES), lambda p, bb, t: (p, bb * steps + t, 0))
    return pl.pallas_call(
        _wkv_kernel,
        grid=(n_pair, batch, steps),
        in_specs=[spec] * 6,
        out_specs=spec,
        out_shape=jax.ShapeDtypeStruct((n_pair, m, LANES), F32),
        scratch_shapes=[pltpu.VMEM((LANES, LANES), F32)],
        compiler_params=_cparams(("parallel", "parallel", "arbitrary")),
        name="wkv_scan",
    )(r, e, k, v, a, b)


def _rwkv_post_kernel(y_ref, r_ref, k_ref, v_ref, g_ref, rk_ref, gg_ref, gb_ref, bd_ref, z_ref):
    y = y_ref[0]
    bdm = bd_ref[...]
    inv_n = 1.0 / HEAD_DIM
    mean = _dot_exact_rhs(y, bdm) * inv_n
    yc = y - mean
    var = _dot_exact_rhs(yc * yc, bdm) * inv_n
    yn = yc * lax.rsqrt(var + GN_EPS) * gg_ref[0] + gb_ref[0]
    bonus = _dot_exact_rhs(r_ref[0] * k_ref[0] * rk_ref[0], bdm) * v_ref[0]
    z_ref[0] = ((yn + bonus) * g_ref[0]).astype(z_ref.dtype)


def _rwkv_post(y, r, k, v, g, r_k, gn_g, gn_b):
    n_pair, m, _ = y.shape
    tm = _tile(m, 1024)
    spec = pl.BlockSpec((1, tm, LANES), lambda p, i: (p, i, 0))
    pspec = pl.BlockSpec((1, 1, LANES), lambda p, i: (p, 0, 0))
    bd = _head_blockdiag(LANES)
    slab = lambda a: a.reshape(n_pair, 1, LANES)
    return pl.pallas_call(
        _rwkv_post_kernel,
        grid=(n_pair, m // tm),
        in_specs=[spec] * 5 + [pspec] * 3 + [pl.BlockSpec(bd.shape, lambda p, i: (0, 0))],
        out_specs=spec,
        out_shape=jax.ShapeDtypeStruct((n_pair, m, LANES), BF16),
        compiler_params=_cparams(("parallel", "parallel")),
        name="rwkv_post",
    )(y, r, k, v, g, slab(r_k), slab(gn_g), slab(gn_b), bd)


def _out_proj_kernel(alpha, z_ref, w_ref, x_ref, g_ref, b_ref, o_ref):
    n_pair = z_ref.shape[0]
    acc = _dot(z_ref[0], w_ref[0])
    for p in range(1, n_pair):
        acc = acc + _dot(z_ref[p], w_ref[p])
    o_ref[...] = _layer_norm(alpha * x_ref[...] + acc, g_ref[...], b_ref[...])


def _out_proj_deepnorm(z, w, x, ln_g, ln_b, alpha):
    n_pair, m, _ = z.shape
    d = w.shape[1]
    tm = _tile(m, 512)
    w3 = w.astype(BF16).reshape(n_pair, LANES, d)
    return pl.pallas_call(
        functools.partial(_out_proj_kernel, alpha),
        grid=(m // tm,),
        in_specs=[pl.BlockSpec((n_pair, tm, LANES), lambda i: (0, i, 0)),
                  pl.BlockSpec(w3.shape, lambda i: (0, 0, 0)),
                  pl.BlockSpec((tm, d), lambda i: (i, 0)),
                  pl.BlockSpec((1, d), lambda i: (0, 0)),
                  pl.BlockSpec((1, d), lambda i: (0, 0))],
        out_specs=pl.BlockSpec((tm, d), lambda i: (i, 0)),
        out_shape=jax.ShapeDtypeStruct((m, d), F32),
        compiler_params=_cparams(("parallel",)),
        name="out_proj_deepnorm",
    )(z, w3, x, ln_g.reshape(1, d), ln_b.reshape(1, d))


def _ffn_kernel(alpha, gated, *refs):
    if gated:
        x_ref, gate_ref, wg_ref, wu_ref, wd_ref, g_ref, b_ref, o_ref, xb_s, acc_s = refs
    else:
        x_ref, wg_ref, wu_ref, wd_ref, g_ref, b_ref, o_ref, xb_s, acc_s = refs
    ex = pl.program_id(1)
    f = pl.program_id(2)
    first = jnp.logical_and(ex == 0, f == 0)
    last = jnp.logical_and(ex == pl.num_programs(1) - 1, f == pl.num_programs(2) - 1)

    @pl.when(first)
    def _():
        xb_s[...] = x_ref[...].astype(BF16)
        acc_s[...] = jnp.zeros_like(acc_s)

    xb = xb_s[...]
    h1 = _dot(xb, wg_ref[0])
    h2 = _dot(xb, wu_ref[0])
    act = h1 * _sigmoid(h1) * h2
    if gated:
        gates = gate_ref[...]
        lane = lax.broadcasted_iota(jnp.int32, gates.shape, 1)
        ge = jnp.sum(jnp.where(lane == ex, gates, 0.0), axis=1, keepdims=True)
        act = act * ge
    acc_s[...] += _dot(act.astype(BF16), wd_ref[0])

    @pl.when(last)
    def _():
        o_ref[...] = _layer_norm(alpha * x_ref[...] + acc_s[...], g_ref[...], b_ref[...])


def _ffn_deepnorm(x, w_gate, w_up, w_down, ln_g, ln_b, alpha, gates=None, tf_want=1024):
    m, d = x.shape
    n_exp, _, ff = w_gate.shape
    tm = _tile(m, 512)
    tf = ff
    for cand in range(min(ff, tf_want) // LANES, 0, -1):
        if ff % (cand * LANES) == 0:
            tf = cand * LANES
            break
    gated = gates is not None
    args = [x]
    specs = [pl.BlockSpec((tm, d), lambda i, e, f: (i, 0))]
    if gated:
        args.append(gates)
        specs.append(pl.BlockSpec((tm, LANES), lambda i, e, f: (i, 0)))
    args += [w_gate.astype(BF16), w_up.astype(BF16), w_down.astype(BF16),
             ln_g.reshape(1, d), ln_b.reshape(1, d)]
    specs += [pl.BlockSpec((1, d, tf), lambda i, e, f: (e, 0, f)),
              pl.BlockSpec((1, d, tf), lambda i, e, f: (e, 0, f)),
              pl.BlockSpec((1, tf, d), lambda i, e, f: (e, f, 0)),
              pl.BlockSpec((1, d), lambda i, e, f: (0, 0)),
              pl.BlockSpec((1, d), lambda i, e, f: (0, 0))]
    return pl.pallas_call(
        functools.partial(_ffn_kernel, alpha, gated),
        grid=(m // tm, n_exp, ff // tf),
        in_specs=specs,
        out_specs=pl.BlockSpec((tm, d), lambda i, e, f: (i, 0)),
        out_shape=jax.ShapeDtypeStruct((m, d), F32),
        scratch_shapes=[pltpu.VMEM((tm, d), BF16), pltpu.VMEM((tm, d), F32)],
        compiler_params=_cparams(("parallel", "arbitrary", "arbitrary")),
        name="ffn_deepnorm",
    )(*args)


def _router_kernel(n_exp, x_ref, w_ref, gate_ref):
    logits = _dot_sp(_split(x_ref[...]), _split(w_ref[...]))
    lane = lax.broadcasted_iota(jnp.int32, logits.shape, 1).astype(F32)
    neg_inf = jnp.float32(-jnp.inf)
    lg = jnp.where(lane < n_exp, logits, neg_inf)
    m1 = jnp.max(lg, axis=1, keepdims=True)
    i1 = jnp.min(jnp.where(lg == m1, lane, float(LANES)), axis=1, keepdims=True)
    lg2 = jnp.where(lane == i1, neg_inf, lg)
    m2 = jnp.max(lg2, axis=1, keepdims=True)
    i2 = jnp.min(jnp.where(lg2 == m2, lane, float(LANES)), axis=1, keepdims=True)
    e2 = jnp.exp(m2 - m1)
    den = 1.0 + e2
    gate_ref[...] = (jnp.where(lane == i1, 1.0 / den, 0.0)
                     + jnp.where(lane == i2, e2 / den, 0.0))


def _router_gates(x, router):
    m, d = x.shape
    n_exp = router.shape[1]
    tm = _tile(m, 512)
    w = jnp.pad(router, ((0, 0), (0, LANES - n_exp)))
    return pl.pallas_call(
        functools.partial(_router_kernel, n_exp),
        grid=(m // tm,),
        in_specs=[pl.BlockSpec((tm, d), lambda i: (i, 0)),
                  pl.BlockSpec((d, LANES), lambda i: (0, 0))],
        out_specs=pl.BlockSpec((tm, LANES), lambda i: (i, 0)),
        out_shape=jax.ShapeDtypeStruct((m, LANES), F32),
        compiler_params=_cparams(("parallel",)),
        name="moe_router",
    )(x, w)


def _proj_pairs_kernel(transposed, x_ref, w_ref, o_ref, xb_s):
    @pl.when(pl.program_id(1) == 0)
    def _():
        xb_s[...] = x_ref[...].astype(BF16)

    n_slab = o_ref.shape[0]
    if transposed:
        res = _dot_nt(w_ref[...], xb_s[...])
        for q in range(n_slab):
            o_ref[q] = res[q * LANES:(q + 1) * LANES, :]
    else:
        res = _dot(xb_s[...], w_ref[...])
        for q in range(n_slab):
            o_ref[q] = res[:, q * LANES:(q + 1) * LANES]


def _proj_pairs(x, w, transposed):
    m, d = x.shape
    n = w.shape[1]
    tm = _tile(m, 512)
    tn = _tile(n, 256)
    n_slab = tn // LANES
    if transposed:
        wb = w.T.astype(BF16)
        w_spec = pl.BlockSpec((tn, d), lambda i, j: (j, 0))
        o_spec = pl.BlockSpec((n_slab, LANES, tm), lambda i, j: (j, 0, i))
        o_shape = jax.ShapeDtypeStruct((n // LANES, LANES, m), F32)
    else:
        wb = w.astype(BF16)
        w_spec = pl.BlockSpec((d, tn), lambda i, j: (0, j))
        o_spec = pl.BlockSpec((n_slab, tm, LANES), lambda i, j: (j, i, 0))
        o_shape = jax.ShapeDtypeStruct((n // LANES, m, LANES), F32)
    return pl.pallas_call(
        functools.partial(_proj_pairs_kernel, transposed),
        grid=(m // tm, n // tn),
        in_specs=[pl.BlockSpec((tm, d), lambda i, j: (i, 0)), w_spec],
        out_specs=o_spec,
        out_shape=o_shape,
        scratch_shapes=[pltpu.VMEM((tm, d), BF16)],
        compiler_params=_cparams(("parallel", "arbitrary")),
        name="proj_pairs_t" if transposed else "proj_pairs",
    )(x, wb)


def _block_mean_kernel(k_ref, o_ref):
    k = k_ref[0]
    nb = k.shape[0] // MOBA_BLOCK
    o_ref[0] = jnp.mean(k.reshape(nb, MOBA_BLOCK, LANES), axis=1)


def _block_means(k_pairs, batch):
    n_pair, m, _ = k_pairs.shape
    seq = m // batch
    nb = seq // MOBA_BLOCK
    return pl.pallas_call(
        _block_mean_kernel,
        grid=(n_pair, batch),
        in_specs=[pl.BlockSpec((1, seq, LANES), lambda p, b: (p, b, 0))],
        out_specs=pl.BlockSpec((1, nb, LANES), lambda p, b: (p, b, 0)),
        out_shape=jax.ShapeDtypeStruct((n_pair, batch * nb, LANES), F32),
        compiler_params=_cparams(("parallel", "parallel")),
        name="moba_block_means",
    )(k_pairs)


def _moba_kernel(scale, qt_ref, k_ref, vt_ref, km_ref, o_ref, neg_s):
    own = pl.program_id(2)
    blk = MOBA_BLOCK
    nb = km_ref.shape[1]
    qt = qt_ref[0]
    row = lax.broadcasted_iota(jnp.int32, qt.shape, 0)
    zero_q = jnp.zeros_like(qt)
    km = _split(km_ref[0])
    n_iota = lax.broadcasted_iota(jnp.int32, (nb, blk), 0).astype(F32)
    past = n_iota < own.astype(F32)
    neg_inf = jnp.float32(-jnp.inf)

    qh = []
    for h in range(2):
        in_head = (row < HEAD_DIM) if h == 0 else (row >= HEAD_DIM)
        q_h = jnp.where(in_head, qt, zero_q)
        qh.append(q_h.astype(BF16))
        gate = _dot_sp(km, _split(q_h))
        gate = jnp.where(past, gate, neg_inf)
        neg = jnp.full((nb, blk), NEG_BIG, F32)
        for _ in range(min(MOBA_TOPK, nb)):
            mx = jnp.max(gate, axis=0, keepdims=True)
            idx = jnp.min(jnp.where(gate == mx, n_iota, float(nb)), axis=0, keepdims=True)
            pick = n_iota == idx
            neg = jnp.where(jnp.logical_and(pick, past), 0.0, neg)
            gate = jnp.where(pick, neg_inf, gate)
        neg_s[h] = neg

    def kv_block(n):
        start = pl.multiple_of(n * blk, blk)
        kb = k_ref[0, pl.ds(start, blk), :].astype(BF16)
        vtb = vt_ref[0, :, pl.ds(start, blk)].astype(BF16)
        return kb, vtb

    kb, vtb = kv_block(own)
    kpos = lax.broadcasted_iota(jnp.int32, (blk, blk), 0)
    qpos = lax.broadcasted_iota(jnp.int32, (blk, blk), 1)
    causal = kpos <= qpos
    init = []
    for h in range(2):
        st = jnp.where(causal, _dot(kb, qh[h]) * scale, neg_inf)
        mx = jnp.max(st, axis=0, keepdims=True)
        p = jnp.exp(st - mx)
        den = jnp.sum(p, axis=0, keepdims=True)
        acc = _dot(vtb[h * HEAD_DIM:(h + 1) * HEAD_DIM, :], p.astype(BF16))
        init += [mx, den, acc]

    def body(n, carry):
        kb, vtb = kv_block(n)
        out = []
        for h in range(2):
            mx, den, acc = carry[3 * h:3 * h + 3]
            st = _dot(kb, qh[h]) * scale + neg_s[h, pl.ds(n, 1), :]
            mx_new = jnp.maximum(mx, jnp.max(st, axis=0, keepdims=True))
            alpha = jnp.exp(mx - mx_new)
            p = jnp.exp(st - mx_new)
            den = alpha * den + jnp.sum(p, axis=0, keepdims=True)
            acc = alpha * acc + _dot(vtb[h * HEAD_DIM:(h + 1) * HEAD_DIM, :], p.astype(BF16))
            out += [mx_new, den, acc]
        return tuple(out)

    fin = lax.fori_loop(0, own, body, tuple(init))
    o_t = jnp.concatenate([fin[2] / fin[1], fin[5] / fin[4]], axis=0)
    o_ref[0] = o_t.T.astype(o_ref.dtype)


def _moba_attention(q_t, k, v_t, k_means, batch):
    n_pair, _, m = q_t.shape
    seq = m // batch
    nb = seq // MOBA_BLOCK
    return pl.pallas_call(
        functools.partial(_moba_kernel, HEAD_DIM ** -0.5),
        grid=(n_pair, batch, nb),
        in_specs=[pl.BlockSpec((1, LANES, MOBA_BLOCK), lambda p, b, i: (p, 0, b * nb + i)),
                  pl.BlockSpec((1, seq, LANES), lambda p, b, i: (p, b, 0)),
                  pl.BlockSpec((1, LANES, seq), lambda p, b, i: (p, 0, b)),
                  pl.BlockSpec((1, nb, LANES), lambda p, b, i: (p, b, 0))],
        out_specs=pl.BlockSpec((1, MOBA_BLOCK, LANES), lambda p, b, i: (p, b * nb + i, 0)),
        out_shape=jax.ShapeDtypeStruct((n_pair, m, LANES), BF16),
        scratch_shapes=[pltpu.VMEM((2, nb, MOBA_BLOCK), F32)],
        compiler_params=_cparams(("parallel", "parallel", "arbitrary")),
        name="moba_attention",
    )(q_t, k, v_t, k_means)


def kernel(x, rwkv_mu, rwkv_w_rkv, rwkv_w_out, rwkv_decay_w0, rwkv_decay_w1, rwkv_decay_w2, rwkv_iclr_a0, rwkv_iclr_a1, rwkv_iclr_a2, rwkv_vres_v0, rwkv_vres_v1, rwkv_vres_v2, rwkv_gate_g1, rwkv_gate_g2, rwkv_k_k, rwkv_k_a, rwkv_r_k, rwkv_gn_g, rwkv_gn_b, moba_w_k, moba_w_v, moba_w_q, moba_w_o, ffn_w_gate, ffn_w_up, ffn_w_down, moe_router, moe_w_gate, moe_w_up, moe_w_down, ln_g, ln_b):
    batch, seq, d = x.shape
    assert d % (2 * LANES) == 0 and seq % MOBA_BLOCK == 0 and seq % WKV_CHUNK == 0
    depth = ln_g.shape[0]
    n_rwkv = rwkv_mu.shape[0]
    alpha = (2.0 * depth) ** 0.25
    h = x.reshape(batch * seq, d)
    v_first = None
    kv = None
    for layer in range(depth):
        if layer < n_rwkv:
            i = layer
            vres = None if i == 0 else (rwkv_vres_v0[i - 1], rwkv_vres_v1[i - 1], rwkv_vres_v2[i - 1])
            r, e, k, v, a, b, g = _rwkv_proj(
                h, seq, rwkv_mu[i], rwkv_w_rkv[i], rwkv_decay_w1[i], rwkv_iclr_a1[i],
                rwkv_gate_g1[i], rwkv_decay_w2[i], rwkv_iclr_a2[i], rwkv_gate_g2[i],
                rwkv_decay_w0[i], rwkv_iclr_a0[i], rwkv_k_k[i], rwkv_k_a[i],
                vres=vres, v_first=v_first)
            if i == 0:
                v_first = v
            y = _wkv_scan(r, e, k, v, a, b, batch)
            mix = _rwkv_post(y, r, k, v, g, rwkv_r_k[i], rwkv_gn_g[i], rwkv_gn_b[i])
            w_out = rwkv_w_out[i]
        else:
            jdx = layer - n_rwkv
            k_pairs, v_t, k_means = kv
            q_t = _proj_pairs(h, moba_w_q[jdx], transposed=True)
            mix = _moba_attention(q_t, k_pairs, v_t, k_means, batch)
            w_out = moba_w_o[jdx]
        h = _out_proj_deepnorm(mix, w_out, h, ln_g[layer, 0], ln_b[layer, 0], alpha)
        ex = layer // 2
        if layer % 2 == 0:
            h = _ffn_deepnorm(h, ffn_w_gate[ex][None], ffn_w_up[ex][None], ffn_w_down[ex][None],
                              ln_g[layer, 1], ln_b[layer, 1], alpha, tf_want=1408)
        else:
            gates = _router_gates(h, moe_router[ex])
            h = _ffn_deepnorm(h, moe_w_gate[ex], moe_w_up[ex], moe_w_down[ex],
                              ln_g[layer, 1], ln_b[layer, 1], alpha, gates=gates, tf_want=896)
        if layer == n_rwkv - 1:
            k_pairs = _proj_pairs(h, moba_w_k, transposed=False)
            v_t = _proj_pairs(h, moba_w_v, transposed=True)
            kv = (k_pairs, v_t, _block_means(k_pairs, batch))
    return h.reshape(batch, seq, d)
```

```python
import functools
import math

import jax
import jax.numpy as jnp
from jax import lax
from jax.experimental import pallas as pl
from jax.experimental.pallas import tpu as pltpu

HEAD_DIM = 64
LANES = 128
GN_EPS = 64e-5
LN_EPS = 1e-5
MOBA_BLOCK = 256
MOBA_TOPK = 3
N_EXPERTS = 8
WKV_CHUNK = 64
WKV_SLABS = 8
NEG_BIG = -1e30
MOBA_KEY_SPLIT = 1
MOBA_AHEAD = 16
MOBA_SLABS = 4
DEN_ROWS = 16
LOG2E = 1.4426950408889634

F32 = jnp.float32
BF16 = jnp.bfloat16
VMEM_LIMIT = 56 * 1024 * 1024


def _cparams(sem):
    return pltpu.CompilerParams(dimension_semantics=sem, vmem_limit_bytes=VMEM_LIMIT)


def _dot(a, b):
    return jnp.dot(a, b, preferred_element_type=F32)


def _dot_nt(a, b):
    return lax.dot_general(a, b, (((1,), (1,)), ((), ())), preferred_element_type=F32)


def _split(x):
    hi = x.astype(BF16)
    lo = (x - hi.astype(F32)).astype(BF16)
    return hi, lo


def _dot_sp(a, b, nt=False):
    d = _dot_nt if nt else _dot
    return d(a[0], b[0]) + (d(a[0], b[1]) + d(a[1], b[0]))


def _dot_exact_rhs(a, b_exact, nt=False):
    d = _dot_nt if nt else _dot
    hi = a.astype(BF16)
    r1 = a - hi.astype(F32)
    mid = r1.astype(BF16)
    lo = (r1 - mid.astype(F32)).astype(BF16)
    return d(hi, b_exact) + (d(mid, b_exact) + d(lo, b_exact))


def _sigmoid(x):
    return 1.0 / (1.0 + jnp.exp(-x))


def _layer_norm(y, g, b):
    mu = jnp.mean(y, axis=-1, keepdims=True)
    yc = y - mu
    var = jnp.mean(yc * yc, axis=-1, keepdims=True)
    return yc * lax.rsqrt(var + LN_EPS) * g + b


def _head_blockdiag(n):
    i = jnp.arange(n) // HEAD_DIM
    return (i[:, None] == i[None, :]).astype(BF16)


def _tile(n, want):
    t = min(n, want)
    assert n % t == 0, (n, want)
    return t


def _rwkv_proj_kernel(has_vres, steps_per_seq, *refs):
    if has_vres:
        (x_ref, xp_ref, mu_ref, wrkv_ref, wd1_ref, wa1_ref, wg1_ref, wv1_ref,
         wd2_ref, wa2_ref, wg2_ref, wv2_ref, w0_ref, a0_ref, v0_ref, kk_ref, ka_ref,
         bd_ref, vf_ref,
         r_out, e_out, k_out, v_out, a_out, b_out, g_out,
         xm_s, hd_s, ha_s, hg_s, hv_s) = refs
    else:
        (x_ref, xp_ref, mu_ref, wrkv_ref, wd1_ref, wa1_ref, wg1_ref,
         wd2_ref, wa2_ref, wg2_ref, w0_ref, a0_ref, kk_ref, ka_ref,
         bd_ref,
         r_out, e_out, k_out, v_out, a_out, b_out, g_out,
         xm_s, hd_s, ha_s, hg_s) = refs
    i = pl.program_id(0)
    j = pl.program_id(1)

    @pl.when(j == 0)
    def _():
        x = x_ref[...]
        tm = x.shape[0]
        prev_row = jnp.where(i % steps_per_seq == 0, 0.0, xp_ref[7:8, :])
        rolled = pltpu.roll(x, 1, axis=0)
        row = lax.broadcasted_iota(jnp.int32, (tm, 1), 0)
        x_prev = jnp.where(row == 0, prev_row, rolled)
        xx = x_prev - x
        for c in range(3):
            xm_s[c] = (x + xx * mu_ref[c:c + 1, :]).astype(BF16)
        xw = (x + xx * mu_ref[3:4, :]).astype(BF16)
        xa = (x + xx * mu_ref[4:5, :]).astype(BF16)
        xg = (x + xx * mu_ref[5:6, :]).astype(BF16)
        hd_s[...] = jnp.tanh(_dot(xw, wd1_ref[...])).astype(BF16)
        ha_s[...] = _dot(xa, wa1_ref[...]).astype(BF16)
        hg_s[...] = _sigmoid(_dot(xg, wg1_ref[...])).astype(BF16)
        if has_vres:
            hv_s[...] = _dot(xm_s[2], wv1_ref[...]).astype(BF16)

    r = _dot(xm_s[0], wrkv_ref[0])
    k = _dot(xm_s[1], wrkv_ref[1])
    v = _dot(xm_s[2], wrkv_ref[2])
    z = w0_ref[...] + _dot(hd_s[...], wd2_ref[...])
    nz = -z
    softplus = jnp.maximum(nz, 0.0) + jnp.log(1.0 + jnp.exp(-jnp.abs(nz)))
    e = jnp.exp(-softplus - 0.5)
    a = _sigmoid(a0_ref[...] + _dot(ha_s[...], wa2_ref[...]))
    g = _dot(hg_s[...], wg2_ref[...])
    n_slab = r.shape[1] // LANES
    if has_vres:
        vf = jnp.concatenate([vf_ref[q] for q in range(n_slab)], axis=1)
        v = v + (vf - v) * _sigmoid(v0_ref[...] + _dot(hv_s[...], wv2_ref[...]))
    kk = k * kk_ref[...]
    ss = _dot_exact_rhs(kk * kk, bd_ref[...])
    kk = kk * lax.rsqrt(jnp.maximum(ss, 1e-24))
    k = k * (1.0 + (a - 1.0) * ka_ref[...])
    for q in range(n_slab):
        sl = slice(q * LANES, (q + 1) * LANES)
        r_out[q] = r[:, sl]
        e_out[q] = e[:, sl]
        k_out[q] = k[:, sl]
        v_out[q] = v[:, sl]
        a_out[q] = -kk[:, sl]
        b_out[q] = (kk * a)[:, sl]
        g_out[q] = g[:, sl]


def _rwkv_proj(x, seq_len, mu, w_rkv, wd1, wa1, wg1, wd2, wa2, wg2, w0, a0, k_k, k_a,
               vres=None, v_first=None):
    m, d = x.shape
    tm = _tile(seq_len, 512)
    tn = _tile(d, 256)
    n_slab = tn // LANES
    has_vres = vres is not None
    row = lambda a: a.reshape(1, d)
    full = lambda a: pl.BlockSpec(a.shape, lambda i, j: (0,) * a.ndim)
    colblk = lambda rows: pl.BlockSpec((rows, tn), lambda i, j: (0, j))
    bd = _head_blockdiag(tn)
    wd1, wa1, wg1 = wd1.astype(BF16), wa1.astype(BF16), wg1.astype(BF16)
    wd2, wa2, wg2 = wd2.astype(BF16), wa2.astype(BF16), wg2.astype(BF16)
    w_rkv = w_rkv.astype(BF16)
    args = [x, x, mu, w_rkv, wd1, wa1, wg1]
    specs = [pl.BlockSpec((tm, d), lambda i, j: (i, 0)),
             pl.BlockSpec((8, d), lambda i, j: (jnp.maximum(i * (tm // 8) - 1, 0), 0)),
             full(mu),
             pl.BlockSpec((3, d, tn), lambda i, j: (0, 0, j)),
             full(wd1), full(wa1), full(wg1)]
    if has_vres:
        v0, wv1, wv2 = vres
        wv1, wv2 = wv1.astype(BF16), wv2.astype(BF16)
        args += [wv1]
        specs += [full(wv1)]
    args += [wd2, wa2, wg2]
    specs += [colblk(wd2.shape[0]), colblk(wa2.shape[0]), colblk(wg2.shape[0])]
    if has_vres:
        args += [wv2]
        specs += [colblk(wv2.shape[0])]
    args += [row(w0), row(a0)]
    specs += [colblk(1), colblk(1)]
    if has_vres:
        args += [row(v0)]
        specs += [colblk(1)]
    args += [row(k_k), row(k_a), bd]
    specs += [colblk(1), colblk(1), full(bd)]
    slab_spec = pl.BlockSpec((n_slab, tm, LANES), lambda i, j: (j, i, 0))
    if has_vres:
        args += [v_first]
        specs += [slab_spec]
    slab = jax.ShapeDtypeStruct((d // LANES, m, LANES), F32)
    scratch = [pltpu.VMEM((3, tm, d), BF16),
               pltpu.VMEM((tm, wd1.shape[1]), BF16),
               pltpu.VMEM((tm, wa1.shape[1]), BF16),
               pltpu.VMEM((tm, wg1.shape[1]), BF16)]
    if has_vres:
        scratch.append(pltpu.VMEM((tm, wv1.shape[1]), BF16))
    return pl.pallas_call(
        functools.partial(_rwkv_proj_kernel, has_vres, seq_len // tm),
        grid=(m // tm, d // tn),
        in_specs=specs,
        out_specs=[slab_spec] * 7,
        out_shape=[slab] * 7,
        scratch_shapes=scratch,
        compiler_params=_cparams(("parallel", "arbitrary")),
        name="rwkv_proj",
    )(*args)


def _wkv_kernel(r_ref, e_ref, k_ref, v_ref, a_ref, b_ref, y_ref, s_ref):
    c = WKV_CHUNK
    n_chunks = r_ref.shape[1] // c

    @pl.when(pl.program_id(2) == 0)
    def _():
        s_ref[...] = jnp.zeros_like(s_ref)

    lane = lax.broadcasted_iota(jnp.int32, (c, LANES), 1)
    t_idx = lax.broadcasted_iota(jnp.int32, (c, LANES), 0)
    head0 = lane < HEAD_DIM
    s_idx = jnp.bitwise_and(lane, HEAD_DIM - 1)
    strict = s_idx < t_idx
    incl = s_idx <= t_idx
    tri = jnp.where(lax.broadcasted_iota(jnp.int32, (c, c), 1)
                    <= lax.broadcasted_iota(jnp.int32, (c, c), 0), 1.0, 0.0).astype(BF16)
    rr = lax.broadcasted_iota(jnp.int32, (LANES, LANES), 0)
    cc = lax.broadcasted_iota(jnp.int32, (LANES, LANES), 1)
    same_head = (rr < HEAD_DIM) == (cc < HEAD_DIM)

    def bd(x):
        z = jnp.zeros_like(x)
        return jnp.concatenate([jnp.where(head0, x, z), jnp.where(head0, z, x)], axis=0)

    def cat(a, b, axis=0):
        return jnp.concatenate([a, b], axis=axis)

    def chunk(ci, carry):
        stages = [one_slab(ci, p) for p in range(r_ref.shape[0])]
        while stages:
            stages = [g for g in stages if next(g, "done") != "done"]
        return carry

    def one_slab(ci, p):
        sl = pl.ds(pl.multiple_of(ci * c, c), c)
        r = r_ref[p, sl, :]
        e = e_ref[p, sl, :]
        k = k_ref[p, sl, :]
        v = v_ref[p, sl, :]
        a = a_ref[p, sl, :]
        b = b_ref[p, sl, :]
        s0 = s_ref[p]
        cum = _cumsum(tri, e)
        yield
        tot = cum[c - 1:c, :]
        ar = cat(a * jnp.exp(e - cum), r * jnp.exp(-cum)).astype(BF16)
        ec = jnp.exp(cum)
        bt = (b * ec).astype(BF16)
        kt = (k * ec).astype(BF16)
        eh = jnp.exp(cum - tot)
        bk = cat(b * eh, k * eh).astype(BF16)
        bdv = bd(v.astype(BF16))
        gram = _dot_nt(ar, cat(bd(bt), bd(kt)))
        yield
        zero = jnp.zeros((c, LANES), F32)
        l_ab = jnp.where(strict, gram[:c, :LANES], zero)
        n_ak = jnp.where(strict, gram[:c, LANES:], zero)
        m_rb = jnp.where(incl, gram[c:, :LANES], zero)
        m_rk = jnp.where(incl, gram[c:, LANES:], zero)
        xy = _dot_nt(ar, s0.astype(BF16)) + _dot(cat(n_ak, m_rk).astype(BF16), bdv)
        x = xy[:c]
        y0 = xy[c:]
        yield
        lp = l_ab.astype(BF16)
        n_steps = int(math.log2(c))
        for step in range(n_steps):
            bdx = bd(x.astype(BF16))
            if step + 1 < n_steps:
                t = _dot(lp, cat(bdx, bd(lp), axis=1))
                x = x + t[:, :LANES]
                lp = t[:, LANES:].astype(BF16)
            else:
                x = x + _dot(lp, bdx)
            yield
        y_ref[p, sl, :] = y0 + _dot(m_rb.astype(BF16), bd(x.astype(BF16)))
        uv_t = cat(x, v).T.astype(BF16)
        upd = _dot(uv_t, bk)
        s_ref[p] = s0 * jnp.exp(-tot) + jnp.where(same_head, upd, jnp.zeros_like(upd))
        yield

    lax.fori_loop(0, n_chunks, chunk, 0)


def _cumsum(tri, e):
    hi = e.astype(BF16)
    r1 = e - hi.astype(F32)
    mid = r1.astype(BF16)
    lo = (r1 - mid.astype(F32)).astype(BF16)
    return _dot(tri, hi) + (_dot(tri, mid) + _dot(tri, lo))


def _wkv_scan(r, e, k, v, a, b, batch):
    n_pair, m, _ = r.shape
    seq = m // batch
    tb = _tile(seq, 256)
    steps = seq // tb
    n_slab = _tile(n_pair, WKV_SLABS)
    spec = pl.BlockSpec((n_slab, tb, LANES), lambda p, bb, t: (p, bb * steps + t, 0))
    return pl.pallas_call(
        _wkv_kernel,
        grid=(n_pair // n_slab, batch, steps),
        in_specs=[spec] * 6,
        out_specs=spec,
        out_shape=jax.ShapeDtypeStruct((n_pair, m, LANES), F32),
        scratch_shapes=[pltpu.VMEM((n_slab, LANES, LANES), F32)],
        compiler_params=_cparams(("parallel", "parallel", "arbitrary")),
        name="wkv_scan",
    )(r, e, k, v, a, b)


def _rwkv_post_kernel(y_ref, r_ref, k_ref, v_ref, g_ref, rk_ref, gg_ref, gb_ref, bd_ref, z_ref):
    y = y_ref[0]
    bdm = bd_ref[...]
    inv_n = 1.0 / HEAD_DIM
    mean = _dot_exact_rhs(y, bdm) * inv_n
    yc = y - mean
    var = _dot_exact_rhs(yc * yc, bdm) * inv_n
    yn = yc * lax.rsqrt(var + GN_EPS) * gg_ref[0] + gb_ref[0]
    bonus = _dot_exact_rhs(r_ref[0] * k_ref[0] * rk_ref[0], bdm) * v_ref[0]
    z_ref[0] = ((yn + bonus) * g_ref[0]).astype(z_ref.dtype)


def _rwkv_post(y, r, k, v, g, r_k, gn_g, gn_b):
    n_pair, m, _ = y.shape
    tm = _tile(m, 1024)
    spec = pl.BlockSpec((1, tm, LANES), lambda p, i: (p, i, 0))
    pspec = pl.BlockSpec((1, 1, LANES), lambda p, i: (p, 0, 0))
    bd = _head_blockdiag(LANES)
    slab = lambda a: a.reshape(n_pair, 1, LANES)
    return pl.pallas_call(
        _rwkv_post_kernel,
        grid=(n_pair, m // tm),
        in_specs=[spec] * 5 + [pspec] * 3 + [pl.BlockSpec(bd.shape, lambda p, i: (0, 0))],
        out_specs=spec,
        out_shape=jax.ShapeDtypeStruct((n_pair, m, LANES), BF16),
        compiler_params=_cparams(("parallel", "parallel")),
        name="rwkv_post",
    )(y, r, k, v, g, slab(r_k), slab(gn_g), slab(gn_b), bd)


def _out_proj_kernel(alpha, z_ref, w_ref, x_ref, g_ref, b_ref, o_ref):
    n_pair = z_ref.shape[0]
    acc = _dot(z_ref[0], w_ref[0])
    for p in range(1, n_pair):
        acc = acc + _dot(z_ref[p], w_ref[p])
    o_ref[...] = _layer_norm(alpha * x_ref[...] + acc, g_ref[...], b_ref[...])


def _out_proj_deepnorm(z, w, x, ln_g, ln_b, alpha):
    n_pair, m, _ = z.shape
    d = w.shape[1]
    tm = _tile(m, 512)
    w3 = w.astype(BF16).reshape(n_pair, LANES, d)
    return pl.pallas_call(
        functools.partial(_out_proj_kernel, alpha),
        grid=(m // tm,),
        in_specs=[pl.BlockSpec((n_pair, tm, LANES), lambda i: (0, i, 0)),
                  pl.BlockSpec(w3.shape, lambda i: (0, 0, 0)),
                  pl.BlockSpec((tm, d), lambda i: (i, 0)),
                  pl.BlockSpec((1, d), lambda i: (0, 0)),
                  pl.BlockSpec((1, d), lambda i: (0, 0))],
        out_specs=pl.BlockSpec((tm, d), lambda i: (i, 0)),
        out_shape=jax.ShapeDtypeStruct((m, d), F32),
        compiler_params=_cparams(("parallel",)),
        name="out_proj_deepnorm",
    )(z, w3, x, ln_g.reshape(1, d), ln_b.reshape(1, d))


def _ffn_kernel(alpha, gated, *refs):
    if gated:
        x_ref, gate_ref, wg_ref, wu_ref, wd_ref, g_ref, b_ref, o_ref, xb_s, acc_s = refs
    else:
        x_ref, wg_ref, wu_ref, wd_ref, g_ref, b_ref, o_ref, xb_s, acc_s = refs
    ex = pl.program_id(1)
    f = pl.program_id(2)
    first = jnp.logical_and(ex == 0, f == 0)
    last = jnp.logical_and(ex == pl.num_programs(1) - 1, f == pl.num_programs(2) - 1)

    @pl.when(first)
    def _():
        xb_s[...] = x_ref[...].astype(BF16)
        acc_s[...] = jnp.zeros_like(acc_s)

    xb = xb_s[...]
    h1 = _dot(xb, wg_ref[0])
    h2 = _dot(xb, wu_ref[0])
    act = h1 * _sigmoid(h1) * h2
    if gated:
        gates = gate_ref[...]
        lane = lax.broadcasted_iota(jnp.int32, gates.shape, 1)
        ge = jnp.sum(jnp.where(lane == ex, gates, 0.0), axis=1, keepdims=True)
        act = act * ge
    acc_s[...] += _dot(act.astype(BF16), wd_ref[0])

    @pl.when(last)
    def _():
        o_ref[...] = _layer_norm(alpha * x_ref[...] + acc_s[...], g_ref[...], b_ref[...])


def _ffn_deepnorm(x, w_gate, w_up, w_down, ln_g, ln_b, alpha, gates=None, tf_want=1024):
    m, d = x.shape
    n_exp, _, ff = w_gate.shape
    tm = _tile(m, 512)
    tf = ff
    for cand in range(min(ff, tf_want) // LANES, 0, -1):
        if ff % (cand * LANES) == 0:
            tf = cand * LANES
            break
    gated = gates is not None
    args = [x]
    specs = [pl.BlockSpec((tm, d), lambda i, e, f: (i, 0))]
    if gated:
        args.append(gates)
        specs.append(pl.BlockSpec((tm, LANES), lambda i, e, f: (i, 0)))
    args += [w_gate.astype(BF16), w_up.astype(BF16), w_down.astype(BF16),
             ln_g.reshape(1, d), ln_b.reshape(1, d)]
    specs += [pl.BlockSpec((1, d, tf), lambda i, e, f: (e, 0, f)),
              pl.BlockSpec((1, d, tf), lambda i, e, f: (e, 0, f)),
              pl.BlockSpec((1, tf, d), lambda i, e, f: (e, f, 0)),
              pl.BlockSpec((1, d), lambda i, e, f: (0, 0)),
              pl.BlockSpec((1, d), lambda i, e, f: (0, 0))]
    return pl.pallas_call(
        functools.partial(_ffn_kernel, alpha, gated),
        grid=(m // tm, n_exp, ff // tf),
        in_specs=specs,
        out_specs=pl.BlockSpec((tm, d), lambda i, e, f: (i, 0)),
        out_shape=jax.ShapeDtypeStruct((m, d), F32),
        scratch_shapes=[pltpu.VMEM((tm, d), BF16), pltpu.VMEM((tm, d), F32)],
        compiler_params=_cparams(("parallel", "arbitrary", "arbitrary")),
        name="ffn_deepnorm",
    )(*args)


def _router_kernel(n_exp, x_ref, w_ref, gate_ref):
    logits = _dot_sp(_split(x_ref[...]), _split(w_ref[...]))
    lane = lax.broadcasted_iota(jnp.int32, logits.shape, 1).astype(F32)
    neg_inf = jnp.float32(-jnp.inf)
    lg = jnp.where(lane < n_exp, logits, neg_inf)
    m1 = jnp.max(lg, axis=1, keepdims=True)
    i1 = jnp.min(jnp.where(lg == m1, lane, float(LANES)), axis=1, keepdims=True)
    lg2 = jnp.where(lane == i1, neg_inf, lg)
    m2 = jnp.max(lg2, axis=1, keepdims=True)
    i2 = jnp.min(jnp.where(lg2 == m2, lane, float(LANES)), axis=1, keepdims=True)
    e2 = jnp.exp(m2 - m1)
    den = 1.0 + e2
    gate_ref[...] = (jnp.where(lane == i1, 1.0 / den, 0.0)
                     + jnp.where(lane == i2, e2 / den, 0.0))


def _router_gates(x, router):
    m, d = x.shape
    n_exp = router.shape[1]
    tm = _tile(m, 512)
    w = jnp.pad(router, ((0, 0), (0, LANES - n_exp)))
    return pl.pallas_call(
        functools.partial(_router_kernel, n_exp),
        grid=(m // tm,),
        in_specs=[pl.BlockSpec((tm, d), lambda i: (i, 0)),
                  pl.BlockSpec((d, LANES), lambda i: (0, 0))],
        out_specs=pl.BlockSpec((tm, LANES), lambda i: (i, 0)),
        out_shape=jax.ShapeDtypeStruct((m, LANES), F32),
        compiler_params=_cparams(("parallel",)),
        name="moe_router",
    )(x, w)


def _proj_pairs_kernel(transposed, x_ref, w_ref, o_ref, xb_s):
    @pl.when(pl.program_id(1) == 0)
    def _():
        xb_s[...] = x_ref[...].astype(BF16)

    n_slab = o_ref.shape[0]
    if transposed:
        res = _dot_nt(w_ref[...], xb_s[...])
        for q in range(n_slab):
            o_ref[q] = res[q * LANES:(q + 1) * LANES, :].astype(o_ref.dtype)
    else:
        res = _dot(xb_s[...], w_ref[...])
        for q in range(n_slab):
            o_ref[q] = res[:, q * LANES:(q + 1) * LANES].astype(o_ref.dtype)


def _proj_pairs(x, w, transposed, out_dtype=F32):
    m, d = x.shape
    n = w.shape[1]
    tm = _tile(m, 512)
    tn = _tile(n, 256)
    n_slab = tn // LANES
    if transposed:
        wb = w.T.astype(BF16)
        w_spec = pl.BlockSpec((tn, d), lambda i, j: (j, 0))
        o_spec = pl.BlockSpec((n_slab, LANES, tm), lambda i, j: (j, 0, i))
        o_shape = jax.ShapeDtypeStruct((n // LANES, LANES, m), out_dtype)
    else:
        wb = w.astype(BF16)
        w_spec = pl.BlockSpec((d, tn), lambda i, j: (0, j))
        o_spec = pl.BlockSpec((n_slab, tm, LANES), lambda i, j: (j, i, 0))
        o_shape = jax.ShapeDtypeStruct((n // LANES, m, LANES), out_dtype)
    return pl.pallas_call(
        functools.partial(_proj_pairs_kernel, transposed),
        grid=(m // tm, n // tn),
        in_specs=[pl.BlockSpec((tm, d), lambda i, j: (i, 0)), w_spec],
        out_specs=o_spec,
        out_shape=o_shape,
        scratch_shapes=[pltpu.VMEM((tm, d), BF16)],
        compiler_params=_cparams(("parallel", "arbitrary")),
        name="proj_pairs_t" if transposed else "proj_pairs",
    )(x, wb)


def _block_mean_kernel(k_ref, o_ref):
    k = k_ref[0]
    nb = k.shape[0] // MOBA_BLOCK
    o_ref[0] = jnp.mean(k.reshape(nb, MOBA_BLOCK, LANES), axis=1)


def _block_means(k_pairs, batch):
    n_pair, m, _ = k_pairs.shape
    seq = m // batch
    nb = seq // MOBA_BLOCK
    return pl.pallas_call(
        _block_mean_kernel,
        grid=(n_pair, batch),
        in_specs=[pl.BlockSpec((1, seq, LANES), lambda p, b: (p, b, 0))],
        out_specs=pl.BlockSpec((1, nb, LANES), lambda p, b: (p, b, 0)),
        out_shape=jax.ShapeDtypeStruct((n_pair, batch * nb, LANES), F32),
        compiler_params=_cparams(("parallel", "parallel")),
        name="moba_block_means",
    )(k_pairs)


def _moba_kernel(scale, qt_ref, k_ref, vt_ref, km_ref, o_ref, neg_s):
    own = pl.program_id(2)
    blk = MOBA_BLOCK
    n_slab = qt_ref.shape[0]
    nb = km_ref.shape[1]
    row = lax.broadcasted_iota(jnp.int32, (LANES, blk), 0)
    zero_q = jnp.zeros((LANES, blk), F32)
    n_iota = lax.broadcasted_iota(jnp.int32, (nb, blk), 0).astype(F32)
    past = n_iota < own.astype(F32)
    neg_inf = jnp.float32(-jnp.inf)

    streams = [(g, h) for g in range(n_slab) for h in range(2)]
    qh = []
    for s, (g, h) in enumerate(streams):
        in_head = (row < HEAD_DIM) if h == 0 else (row >= HEAD_DIM)
        q_h = jnp.where(in_head, qt_ref[g], zero_q)
        qh.append((q_h * (scale * LOG2E)).astype(BF16))
        gate = _dot_sp(_split(km_ref[g]), _split(q_h))
        gate = jnp.where(past, gate, neg_inf)
        neg = jnp.full((nb, blk), NEG_BIG, F32)
        for _ in range(min(MOBA_TOPK, nb)):
            mx = jnp.max(gate, axis=0, keepdims=True)
            idx = jnp.min(jnp.where(gate == mx, n_iota, float(nb)), axis=0, keepdims=True)
            pick = n_iota == idx
            neg = jnp.where(jnp.logical_and(pick, past), 0.0, neg)
            gate = jnp.where(pick, neg_inf, gate)
        neg_s[s] = neg

    sub = blk // MOBA_KEY_SPLIT
    ones_rows = jnp.ones((DEN_ROWS, sub), BF16)

    def absorb(n, causal_mask, carry):
        start = pl.multiple_of(n * blk, blk)
        out = list(carry)
        items = [(sb, s) for sb in range(MOBA_KEY_SPLIT) for s in range(len(streams))]

        def scores(w):
            sb, s = items[w]
            kb = k_ref[streams[s][0], pl.ds(start + sb * sub, sub), :]
            return _dot(kb, qh[s])

        st_q = [scores(w) for w in range(min(MOBA_AHEAD, len(items)))]
        for w, (sb, s) in enumerate(items):
            g, h = streams[s]
            st = st_q[w]
            st_q[w] = None
            mx, acc = out[2 * s], out[2 * s + 1]
            if causal_mask is not None:
                st = jnp.where(causal_mask[sb], st, neg_inf)
                bias = None
                mblk = jnp.max(st, axis=0, keepdims=True)
            else:
                bias = neg_s[s, pl.ds(n, 1), :]
                mblk = jnp.max(st, axis=0, keepdims=True) + bias
            mx_new = jnp.maximum(mx, mblk)
            alpha = jnp.exp2(mx - mx_new)
            shift = mx_new if bias is None else mx_new - bias
            p = jnp.exp2(st - shift).astype(BF16)
            vtb = vt_ref[g, h * HEAD_DIM:(h + 1) * HEAD_DIM, pl.ds(start + sb * sub, sub)]
            vt_ext = jnp.concatenate([vtb, ones_rows], axis=0)
            out[2 * s] = mx_new
            out[2 * s + 1] = alpha * acc + _dot(vt_ext, p)
            if w + MOBA_AHEAD < len(items):
                st_q.append(scores(w + MOBA_AHEAD))
        return tuple(out)

    kpos = lax.broadcasted_iota(jnp.int32, (sub, blk), 0)
    qpos = lax.broadcasted_iota(jnp.int32, (sub, blk), 1)
    causal = [kpos + sb * sub <= qpos for sb in range(MOBA_KEY_SPLIT)]
    init = []
    for _ in streams:
        init += [jnp.full((1, blk), neg_inf, F32), jnp.zeros((HEAD_DIM + DEN_ROWS, blk), F32)]
    carry = absorb(own, causal, tuple(init))
    fin = lax.fori_loop(0, own, lambda n, c: absorb(n, None, c), carry)
    for g in range(n_slab):
        a0, a1 = fin[4 * g + 1], fin[4 * g + 3]
        o_t = jnp.concatenate([a0[:HEAD_DIM] / a0[HEAD_DIM:HEAD_DIM + 1],
                               a1[:HEAD_DIM] / a1[HEAD_DIM:HEAD_DIM + 1]], axis=0)
        o_ref[g] = o_t.T.astype(o_ref.dtype)


def _moba_attention(q_t, k, v_t, k_means, batch):
    n_pair, _, m = q_t.shape
    seq = m // batch
    nb = seq // MOBA_BLOCK
    g = _tile(n_pair, MOBA_SLABS)
    return pl.pallas_call(
        functools.partial(_moba_kernel, HEAD_DIM ** -0.5),
        grid=(n_pair // g, batch, nb),
        in_specs=[pl.BlockSpec((g, LANES, MOBA_BLOCK), lambda p, b, i: (p, 0, b * nb + i)),
                  pl.BlockSpec((g, seq, LANES), lambda p, b, i: (p, b, 0)),
                  pl.BlockSpec((g, LANES, seq), lambda p, b, i: (p, 0, b)),
                  pl.BlockSpec((g, nb, LANES), lambda p, b, i: (p, b, 0))],
        out_specs=pl.BlockSpec((g, MOBA_BLOCK, LANES), lambda p, b, i: (p, b * nb + i, 0)),
        out_shape=jax.ShapeDtypeStruct((n_pair, m, LANES), BF16),
        scratch_shapes=[pltpu.VMEM((2 * g, nb, MOBA_BLOCK), F32)],
        compiler_params=_cparams(("parallel", "parallel", "arbitrary")),
        name="moba_attention",
    )(q_t, k, v_t, k_means)


def kernel(x, rwkv_mu, rwkv_w_rkv, rwkv_w_out, rwkv_decay_w0, rwkv_decay_w1, rwkv_decay_w2, rwkv_iclr_a0, rwkv_iclr_a1, rwkv_iclr_a2, rwkv_vres_v0, rwkv_vres_v1, rwkv_vres_v2, rwkv_gate_g1, rwkv_gate_g2, rwkv_k_k, rwkv_k_a, rwkv_r_k, rwkv_gn_g, rwkv_gn_b, moba_w_k, moba_w_v, moba_w_q, moba_w_o, ffn_w_gate, ffn_w_up, ffn_w_down, moe_router, moe_w_gate, moe_w_up, moe_w_down, ln_g, ln_b):
    batch, seq, d = x.shape
    assert d % (2 * LANES) == 0 and seq % MOBA_BLOCK == 0 and seq % WKV_CHUNK == 0
    depth = ln_g.shape[0]
    n_rwkv = rwkv_mu.shape[0]
    alpha = (2.0 * depth) ** 0.25
    h = x.reshape(batch * seq, d)
    v_first = None
    kv = None
    for layer in range(depth):
        if layer < n_rwkv:
            i = layer
            vres = None if i == 0 else (rwkv_vres_v0[i - 1], rwkv_vres_v1[i - 1], rwkv_vres_v2[i - 1])
            r, e, k, v, a, b, g = _rwkv_proj(
                h, seq, rwkv_mu[i], rwkv_w_rkv[i], rwkv_decay_w1[i], rwkv_iclr_a1[i],
                rwkv_gate_g1[i], rwkv_decay_w2[i], rwkv_iclr_a2[i], rwkv_gate_g2[i],
                rwkv_decay_w0[i], rwkv_iclr_a0[i], rwkv_k_k[i], rwkv_k_a[i],
                vres=vres, v_first=v_first)
            if i == 0:
                v_first = v
            y = _wkv_scan(r, e, k, v, a, b, batch)
            mix = _rwkv_post(y, r, k, v, g, rwkv_r_k[i], rwkv_gn_g[i], rwkv_gn_b[i])
            w_out = rwkv_w_out[i]
        else:
            jdx = layer - n_rwkv
            k_pairs, v_t, k_means = kv
            q_t = _proj_pairs(h, moba_w_q[jdx], transposed=True)
            mix = _moba_attention(q_t, k_pairs, v_t, k_means, batch)
            w_out = moba_w_o[jdx]
        h = _out_proj_deepnorm(mix, w_out, h, ln_g[layer, 0], ln_b[layer, 0], alpha)
        ex = layer // 2
        if layer % 2 == 0:
            h = _ffn_deepnorm(h, ffn_w_gate[ex][None], ffn_w_up[ex][None], ffn_w_down[ex][None],
                              ln_g[layer, 1], ln_b[layer, 1], alpha, tf_want=1408)
        else:
            gates = _router_gates(h, moe_router[ex])
            h = _ffn_deepnorm(h, moe_w_gate[ex], moe_w_up[ex], moe_w_down[ex],
                              ln_g[layer, 1], ln_b[layer, 1], alpha, gates=gates, tf_want=896)
        if layer == n_rwkv - 1:
            k_pairs = _proj_pairs(h, moba_w_k, transposed=False)
            v_t = _proj_pairs(h, moba_w_v, transposed=True, out_dtype=BF16)
            kv = (k_pairs.astype(BF16), v_t, _block_means(k_pairs, batch))
    return h.reshape(batch, seq, d)
```

```python
import functools
import math

import jax
import jax.numpy as jnp
from jax import lax
from jax.experimental import pallas as pl
from jax.experimental.pallas import tpu as pltpu

HEAD_DIM = 64
LANES = 128
GN_EPS = 64e-5
LN_EPS = 1e-5
MOBA_BLOCK = 256
MOBA_TOPK = 3
MOE_ROWS = 512
WKV_CHUNK = 64
WKV_SLABS = 8
NEG_BIG = -1e30
MOBA_KEY_SPLIT = 1
MOBA_AHEAD = 16
MOBA_SLABS = 4
DEN_ROWS = 16
LOG2E = 1.4426950408889634

F32 = jnp.float32
BF16 = jnp.bfloat16
VMEM_LIMIT = 56 * 1024 * 1024


def _cparams(sem):
    return pltpu.CompilerParams(dimension_semantics=sem, vmem_limit_bytes=VMEM_LIMIT)


def _dot(a, b):
    return jnp.dot(a, b, preferred_element_type=F32)


def _dot_nt(a, b):
    return lax.dot_general(a, b, (((1,), (1,)), ((), ())), preferred_element_type=F32)


def _split(x):
    hi = x.astype(BF16)
    lo = (x - hi.astype(F32)).astype(BF16)
    return hi, lo


def _dot_sp(a, b, nt=False):
    d = _dot_nt if nt else _dot
    return d(a[0], b[0]) + (d(a[0], b[1]) + d(a[1], b[0]))


def _dot_exact_rhs(a, b_exact, nt=False):
    d = _dot_nt if nt else _dot
    hi = a.astype(BF16)
    r1 = a - hi.astype(F32)
    mid = r1.astype(BF16)
    lo = (r1 - mid.astype(F32)).astype(BF16)
    return d(hi, b_exact) + (d(mid, b_exact) + d(lo, b_exact))


def _sigmoid(x):
    return 1.0 / (1.0 + jnp.exp(-x))


def _layer_norm(y, g, b):
    mu = jnp.mean(y, axis=-1, keepdims=True)
    yc = y - mu
    var = jnp.mean(yc * yc, axis=-1, keepdims=True)
    return yc * lax.rsqrt(var + LN_EPS) * g + b


def _head_blockdiag(n):
    i = jnp.arange(n) // HEAD_DIM
    return (i[:, None] == i[None, :]).astype(BF16)


def _tile(n, want):
    t = min(n, want)
    assert n % t == 0, (n, want)
    return t


def _rwkv_proj_kernel(has_vres, steps_per_seq, *refs):
    if has_vres:
        (x_ref, xp_ref, mu_ref, wrkv_ref, wd1_ref, wa1_ref, wg1_ref, wv1_ref,
         wd2_ref, wa2_ref, wg2_ref, wv2_ref, w0_ref, a0_ref, v0_ref, kk_ref, ka_ref,
         bd_ref, vf_ref,
         r_out, e_out, k_out, v_out, a_out, b_out, g_out,
         xm_s, hd_s, ha_s, hg_s, hv_s) = refs
    else:
        (x_ref, xp_ref, mu_ref, wrkv_ref, wd1_ref, wa1_ref, wg1_ref,
         wd2_ref, wa2_ref, wg2_ref, w0_ref, a0_ref, kk_ref, ka_ref,
         bd_ref,
         r_out, e_out, k_out, v_out, a_out, b_out, g_out,
         xm_s, hd_s, ha_s, hg_s) = refs
    i = pl.program_id(0)
    j = pl.program_id(1)

    @pl.when(j == 0)
    def _():
        x = x_ref[...]
        tm = x.shape[0]
        prev_row = jnp.where(i % steps_per_seq == 0, 0.0, xp_ref[7:8, :])
        rolled = pltpu.roll(x, 1, axis=0)
        row = lax.broadcasted_iota(jnp.int32, (tm, 1), 0)
        x_prev = jnp.where(row == 0, prev_row, rolled)
        xx = x_prev - x
        for c in range(3):
            xm_s[c] = (x + xx * mu_ref[c:c + 1, :]).astype(BF16)
        xw = (x + xx * mu_ref[3:4, :]).astype(BF16)
        xa = (x + xx * mu_ref[4:5, :]).astype(BF16)
        xg = (x + xx * mu_ref[5:6, :]).astype(BF16)
        hd_s[...] = jnp.tanh(_dot(xw, wd1_ref[...])).astype(BF16)
        ha_s[...] = _dot(xa, wa1_ref[...]).astype(BF16)
        hg_s[...] = _sigmoid(_dot(xg, wg1_ref[...])).astype(BF16)
        if has_vres:
            hv_s[...] = _dot(xm_s[2], wv1_ref[...]).astype(BF16)

    r = _dot(xm_s[0], wrkv_ref[0])
    k = _dot(xm_s[1], wrkv_ref[1])
    v = _dot(xm_s[2], wrkv_ref[2])
    z = w0_ref[...] + _dot(hd_s[...], wd2_ref[...])
    nz = -z
    softplus = jnp.maximum(nz, 0.0) + jnp.log(1.0 + jnp.exp(-jnp.abs(nz)))
    e = jnp.exp(-softplus - 0.5)
    a = _sigmoid(a0_ref[...] + _dot(ha_s[...], wa2_ref[...]))
    g = _dot(hg_s[...], wg2_ref[...])
    n_slab = r.shape[1] // LANES
    if has_vres:
        vf = jnp.concatenate([vf_ref[q] for q in range(n_slab)], axis=1)
        v = v + (vf - v) * _sigmoid(v0_ref[...] + _dot(hv_s[...], wv2_ref[...]))
    kk = k * kk_ref[...]
    ss = _dot_exact_rhs(kk * kk, bd_ref[...])
    kk = kk * lax.rsqrt(jnp.maximum(ss, 1e-24))
    k = k * (1.0 + (a - 1.0) * ka_ref[...])
    for q in range(n_slab):
        sl = slice(q * LANES, (q + 1) * LANES)
        r_out[q] = r[:, sl]
        e_out[q] = e[:, sl]
        k_out[q] = k[:, sl]
        v_out[q] = v[:, sl]
        a_out[q] = -kk[:, sl]
        b_out[q] = (kk * a)[:, sl]
        g_out[q] = g[:, sl]


def _rwkv_proj(x, seq_len, mu, w_rkv, wd1, wa1, wg1, wd2, wa2, wg2, w0, a0, k_k, k_a,
               vres=None, v_first=None):
    m, d = x.shape
    tm = _tile(seq_len, 512)
    tn = _tile(d, 256)
    n_slab = tn // LANES
    has_vres = vres is not None
    row = lambda a: a.reshape(1, d)
    full = lambda a: pl.BlockSpec(a.shape, lambda i, j: (0,) * a.ndim)
    colblk = lambda rows: pl.BlockSpec((rows, tn), lambda i, j: (0, j))
    bd = _head_blockdiag(tn)
    wd1, wa1, wg1 = wd1.astype(BF16), wa1.astype(BF16), wg1.astype(BF16)
    wd2, wa2, wg2 = wd2.astype(BF16), wa2.astype(BF16), wg2.astype(BF16)
    w_rkv = w_rkv.astype(BF16)
    args = [x, x, mu, w_rkv, wd1, wa1, wg1]
    specs = [pl.BlockSpec((tm, d), lambda i, j: (i, 0)),
             pl.BlockSpec((8, d), lambda i, j: (jnp.maximum(i * (tm // 8) - 1, 0), 0)),
             full(mu),
             pl.BlockSpec((3, d, tn), lambda i, j: (0, 0, j)),
             full(wd1), full(wa1), full(wg1)]
    if has_vres:
        v0, wv1, wv2 = vres
        wv1, wv2 = wv1.astype(BF16), wv2.astype(BF16)
        args += [wv1]
        specs += [full(wv1)]
    args += [wd2, wa2, wg2]
    specs += [colblk(wd2.shape[0]), colblk(wa2.shape[0]), colblk(wg2.shape[0])]
    if has_vres:
        args += [wv2]
        specs += [colblk(wv2.shape[0])]
    args += [row(w0), row(a0)]
    specs += [colblk(1), colblk(1)]
    if has_vres:
        args += [row(v0)]
        specs += [colblk(1)]
    args += [row(k_k), row(k_a), bd]
    specs += [colblk(1), colblk(1), full(bd)]
    slab_spec = pl.BlockSpec((n_slab, tm, LANES), lambda i, j: (j, i, 0))
    if has_vres:
        args += [v_first]
        specs += [slab_spec]
    slab = jax.ShapeDtypeStruct((d // LANES, m, LANES), F32)
    scratch = [pltpu.VMEM((3, tm, d), BF16),
               pltpu.VMEM((tm, wd1.shape[1]), BF16),
               pltpu.VMEM((tm, wa1.shape[1]), BF16),
               pltpu.VMEM((tm, wg1.shape[1]), BF16)]
    if has_vres:
        scratch.append(pltpu.VMEM((tm, wv1.shape[1]), BF16))
    return pl.pallas_call(
        functools.partial(_rwkv_proj_kernel, has_vres, seq_len // tm),
        grid=(m // tm, d // tn),
        in_specs=specs,
        out_specs=[slab_spec] * 7,
        out_shape=[slab] * 7,
        scratch_shapes=scratch,
        compiler_params=_cparams(("parallel", "arbitrary")),
        name="rwkv_proj",
    )(*args)


def _wkv_kernel(r_ref, e_ref, k_ref, v_ref, a_ref, b_ref, y_ref, s_ref):
    c = WKV_CHUNK
    n_chunks = r_ref.shape[1] // c

    @pl.when(pl.program_id(2) == 0)
    def _():
        s_ref[...] = jnp.zeros_like(s_ref)

    lane = lax.broadcasted_iota(jnp.int32, (c, LANES), 1)
    t_idx = lax.broadcasted_iota(jnp.int32, (c, LANES), 0)
    head0 = lane < HEAD_DIM
    s_idx = jnp.bitwise_and(lane, HEAD_DIM - 1)
    strict = s_idx < t_idx
    incl = s_idx <= t_idx
    tri = jnp.where(lax.broadcasted_iota(jnp.int32, (c, c), 1)
                    <= lax.broadcasted_iota(jnp.int32, (c, c), 0), 1.0, 0.0).astype(BF16)
    rr = lax.broadcasted_iota(jnp.int32, (LANES, LANES), 0)
    cc = lax.broadcasted_iota(jnp.int32, (LANES, LANES), 1)
    same_head = (rr < HEAD_DIM) == (cc < HEAD_DIM)

    def bd(x):
        z = jnp.zeros_like(x)
        return jnp.concatenate([jnp.where(head0, x, z), jnp.where(head0, z, x)], axis=0)

    def cat(a, b, axis=0):
        return jnp.concatenate([a, b], axis=axis)

    def chunk(ci, carry):
        stages = [one_slab(ci, p) for p in range(r_ref.shape[0])]
        while stages:
            stages = [g for g in stages if next(g, "done") != "done"]
        return carry

    def one_slab(ci, p):
        sl = pl.ds(pl.multiple_of(ci * c, c), c)
        r = r_ref[p, sl, :]
        e = e_ref[p, sl, :]
        k = k_ref[p, sl, :]
        v = v_ref[p, sl, :]
        a = a_ref[p, sl, :]
        b = b_ref[p, sl, :]
        s0 = s_ref[p]
        cum = _cumsum(tri, e)
        yield
        tot = cum[c - 1:c, :]
        ar = cat(a * jnp.exp(e - cum), r * jnp.exp(-cum)).astype(BF16)
        ec = jnp.exp(cum)
        bt = (b * ec).astype(BF16)
        kt = (k * ec).astype(BF16)
        eh = jnp.exp(cum - tot)
        bk = cat(b * eh, k * eh).astype(BF16)
        bdv = bd(v.astype(BF16))
        gram = _dot_nt(ar, cat(bd(bt), bd(kt)))
        yield
        zero = jnp.zeros((c, LANES), F32)
        l_ab = jnp.where(strict, gram[:c, :LANES], zero)
        n_ak = jnp.where(strict, gram[:c, LANES:], zero)
        m_rb = jnp.where(incl, gram[c:, :LANES], zero)
        m_rk = jnp.where(incl, gram[c:, LANES:], zero)
        xy = _dot_nt(ar, s0.astype(BF16)) + _dot(cat(n_ak, m_rk).astype(BF16), bdv)
        x = xy[:c]
        y0 = xy[c:]
        yield
        lp = l_ab.astype(BF16)
        n_steps = int(math.log2(c))
        for step in range(n_steps):
            bdx = bd(x.astype(BF16))
            if step + 1 < n_steps:
                t = _dot(lp, cat(bdx, bd(lp), axis=1))
                x = x + t[:, :LANES]
                lp = t[:, LANES:].astype(BF16)
            else:
                x = x + _dot(lp, bdx)
            yield
        y_ref[p, sl, :] = y0 + _dot(m_rb.astype(BF16), bd(x.astype(BF16)))
        uv_t = cat(x, v).T.astype(BF16)
        upd = _dot(uv_t, bk)
        s_ref[p] = s0 * jnp.exp(-tot) + jnp.where(same_head, upd, jnp.zeros_like(upd))
        yield

    lax.fori_loop(0, n_chunks, chunk, 0)


def _cumsum(tri, e):
    hi = e.astype(BF16)
    r1 = e - hi.astype(F32)
    mid = r1.astype(BF16)
    lo = (r1 - mid.astype(F32)).astype(BF16)
    return _dot(tri, hi) + (_dot(tri, mid) + _dot(tri, lo))


def _wkv_scan(r, e, k, v, a, b, batch):
    n_pair, m, _ = r.shape
    seq = m // batch
    tb = _tile(seq, 256)
    steps = seq // tb
    n_slab = _tile(n_pair, WKV_SLABS)
    spec = pl.BlockSpec((n_slab, tb, LANES), lambda p, bb, t: (p, bb * steps + t, 0))
    return pl.pallas_call(
        _wkv_kernel,
        grid=(n_pair // n_slab, batch, steps),
        in_specs=[spec] * 6,
        out_specs=spec,
        out_shape=jax.ShapeDtypeStruct((n_pair, m, LANES), F32),
        scratch_shapes=[pltpu.VMEM((n_slab, LANES, LANES), F32)],
        compiler_params=_cparams(("parallel", "parallel", "arbitrary")),
        name="wkv_scan",
    )(r, e, k, v, a, b)


def _rwkv_post_kernel(y_ref, r_ref, k_ref, v_ref, g_ref, rk_ref, gg_ref, gb_ref, bd_ref, z_ref):
    y = y_ref[0]
    bdm = bd_ref[...]
    inv_n = 1.0 / HEAD_DIM
    mean = _dot_exact_rhs(y, bdm) * inv_n
    yc = y - mean
    var = _dot_exact_rhs(yc * yc, bdm) * inv_n
    yn = yc * lax.rsqrt(var + GN_EPS) * gg_ref[0] + gb_ref[0]
    bonus = _dot_exact_rhs(r_ref[0] * k_ref[0] * rk_ref[0], bdm) * v_ref[0]
    z_ref[0] = ((yn + bonus) * g_ref[0]).astype(z_ref.dtype)


def _rwkv_post(y, r, k, v, g, r_k, gn_g, gn_b):
    n_pair, m, _ = y.shape
    tm = _tile(m, 1024)
    spec = pl.BlockSpec((1, tm, LANES), lambda p, i: (p, i, 0))
    pspec = pl.BlockSpec((1, 1, LANES), lambda p, i: (p, 0, 0))
    bd = _head_blockdiag(LANES)
    slab = lambda a: a.reshape(n_pair, 1, LANES)
    return pl.pallas_call(
        _rwkv_post_kernel,
        grid=(n_pair, m // tm),
        in_specs=[spec] * 5 + [pspec] * 3 + [pl.BlockSpec(bd.shape, lambda p, i: (0, 0))],
        out_specs=spec,
        out_shape=jax.ShapeDtypeStruct((n_pair, m, LANES), BF16),
        compiler_params=_cparams(("parallel", "parallel")),
        name="rwkv_post",
    )(y, r, k, v, g, slab(r_k), slab(gn_g), slab(gn_b), bd)


def _out_proj_kernel(alpha, z_ref, w_ref, x_ref, g_ref, b_ref, o_ref):
    n_pair = z_ref.shape[0]
    acc = _dot(z_ref[0], w_ref[0])
    for p in range(1, n_pair):
        acc = acc + _dot(z_ref[p], w_ref[p])
    o_ref[...] = _layer_norm(alpha * x_ref[...] + acc, g_ref[...], b_ref[...])


def _out_proj_deepnorm(z, w, x, ln_g, ln_b, alpha):
    n_pair, m, _ = z.shape
    d = w.shape[1]
    tm = _tile(m, 512)
    w3 = w.astype(BF16).reshape(n_pair, LANES, d)
    return pl.pallas_call(
        functools.partial(_out_proj_kernel, alpha),
        grid=(m // tm,),
        in_specs=[pl.BlockSpec((n_pair, tm, LANES), lambda i: (0, i, 0)),
                  pl.BlockSpec(w3.shape, lambda i: (0, 0, 0)),
                  pl.BlockSpec((tm, d), lambda i: (i, 0)),
                  pl.BlockSpec((1, d), lambda i: (0, 0)),
                  pl.BlockSpec((1, d), lambda i: (0, 0))],
        out_specs=pl.BlockSpec((tm, d), lambda i: (i, 0)),
        out_shape=jax.ShapeDtypeStruct((m, d), F32),
        compiler_params=_cparams(("parallel",)),
        name="out_proj_deepnorm",
    )(z, w3, x, ln_g.reshape(1, d), ln_b.reshape(1, d))


def _ff_tile(ff, want):
    for cand in range(min(ff, want) // LANES, 0, -1):
        if ff % (cand * LANES) == 0:
            return cand * LANES
    return ff


def _swiglu_partial(xb, wg, wu, wd):
    h1 = _dot(xb, wg)
    h2 = _dot(xb, wu)
    act = h1 * _sigmoid(h1) * h2
    return _dot(act.astype(BF16), wd)


def _ffn_kernel(alpha, x_ref, wg_ref, wu_ref, wd_ref, g_ref, b_ref, o_ref, xb_s, acc_s):
    f = pl.program_id(1)

    @pl.when(f == 0)
    def _():
        xb_s[...] = x_ref[...].astype(BF16)
        acc_s[...] = jnp.zeros_like(acc_s)

    acc_s[...] += _swiglu_partial(xb_s[...], wg_ref[...], wu_ref[...], wd_ref[...])

    @pl.when(f == pl.num_programs(1) - 1)
    def _():
        o_ref[...] = _layer_norm(alpha * x_ref[...] + acc_s[...], g_ref[...], b_ref[...])


def _ffn_deepnorm(x, w_gate, w_up, w_down, ln_g, ln_b, alpha):
    m, d = x.shape
    ff = w_gate.shape[1]
    tm = _tile(m, 512)
    tf = _ff_tile(ff, 1408)
    return pl.pallas_call(
        functools.partial(_ffn_kernel, alpha),
        grid=(m // tm, ff // tf),
        in_specs=[pl.BlockSpec((tm, d), lambda i, f: (i, 0)),
                  pl.BlockSpec((d, tf), lambda i, f: (0, f)),
                  pl.BlockSpec((d, tf), lambda i, f: (0, f)),
                  pl.BlockSpec((tf, d), lambda i, f: (f, 0)),
                  pl.BlockSpec((1, d), lambda i, f: (0, 0)),
                  pl.BlockSpec((1, d), lambda i, f: (0, 0))],
        out_specs=pl.BlockSpec((tm, d), lambda i, f: (i, 0)),
        out_shape=jax.ShapeDtypeStruct((m, d), F32),
        scratch_shapes=[pltpu.VMEM((tm, d), BF16), pltpu.VMEM((tm, d), F32)],
        compiler_params=_cparams(("parallel", "arbitrary")),
        name="ffn_deepnorm",
    )(x, w_gate.astype(BF16), w_up.astype(BF16), w_down.astype(BF16),
      ln_g.reshape(1, d), ln_b.reshape(1, d))


def _router_kernel(n_exp, x_ref, w_ref, idx_ref, wgt_ref):
    logits = _dot_sp(_split(x_ref[...]), _split(w_ref[...]))
    lane = lax.broadcasted_iota(jnp.int32, logits.shape, 1).astype(F32)
    neg_inf = jnp.float32(-jnp.inf)
    lg = jnp.where(lane < n_exp, logits, neg_inf)
    m1 = jnp.max(lg, axis=1, keepdims=True)
    i1 = jnp.min(jnp.where(lg == m1, lane, float(LANES)), axis=1, keepdims=True)
    lg2 = jnp.where(lane == i1, neg_inf, lg)
    m2 = jnp.max(lg2, axis=1, keepdims=True)
    i2 = jnp.min(jnp.where(lg2 == m2, lane, float(LANES)), axis=1, keepdims=True)
    e2 = jnp.exp(m2 - m1)
    den = 1.0 + e2
    idx_ref[...] = jnp.where(lane == 0.0, i1, jnp.where(lane == 1.0, i2, 0.0)).astype(jnp.int32)
    wgt_ref[...] = jnp.where(lane == 0.0, 1.0 / den, jnp.where(lane == 1.0, e2 / den, 0.0))


def _router_top2(x, router):
    m, d = x.shape
    n_exp = router.shape[1]
    tm = _tile(m, 512)
    w = jnp.pad(router, ((0, 0), (0, LANES - n_exp)))
    out_spec = pl.BlockSpec((tm, LANES), lambda i: (i, 0))
    return pl.pallas_call(
        functools.partial(_router_kernel, n_exp),
        grid=(m // tm,),
        in_specs=[pl.BlockSpec((tm, d), lambda i: (i, 0)),
                  pl.BlockSpec((d, LANES), lambda i: (0, 0))],
        out_specs=[out_spec, out_spec],
        out_shape=[jax.ShapeDtypeStruct((m, LANES), jnp.int32),
                   jax.ShapeDtypeStruct((m, LANES), F32)],
        compiler_params=_cparams(("parallel",)),
        name="moe_router",
    )(x, w)


def _moe_plan(idx, m, tm, n_exp):
    n_ent = 2 * m
    expert = jnp.concatenate([idx[:, 0], idx[:, 1]])
    onehot = (expert[:, None] == jnp.arange(n_exp, dtype=jnp.int32)[None, :]).astype(jnp.int32)
    csum = jnp.cumsum(onehot, axis=0)
    counts = csum[-1]
    rank = jnp.sum(csum * onehot, axis=1) - 1
    padded = ((counts + tm - 1) // tm) * tm
    ends = jnp.cumsum(padded)
    starts = ends - padded
    dest = jnp.sum(starts[None, :] * onehot, axis=1) + rank
    n_tiles = n_ent // tm + n_exp
    entry = jnp.arange(n_ent, dtype=jnp.int32)
    src = jnp.zeros((n_tiles * tm,), jnp.int32).at[dest].set(entry % m)
    dst = jnp.zeros((n_tiles * tm,), jnp.int32).at[dest].set(entry)
    tile_start = jnp.arange(n_tiles, dtype=jnp.int32) * tm
    tile_exp = jnp.minimum(jnp.sum((tile_start[:, None] >= ends[None, :]).astype(jnp.int32), axis=1),
                           n_exp - 1)
    n_valid = jnp.clip(starts[tile_exp] + counts[tile_exp] - tile_start, 0, tm)
    return tile_exp, n_valid, src.reshape(n_tiles, 1, tm), dst.reshape(n_tiles, 1, tm)


def _moe_kernel(te_ref, nv_ref, src_ref, srcn_ref, dst_ref, x_hbm, wg_ref, wu_ref, wd_ref,
                y_hbm, xg_s, xb_s, acc_s, gsem, ssem):
    t = pl.program_id(0)
    f = pl.program_id(1)
    tm = xb_s.shape[0]
    slot = t % 2

    def gather_row(idx_ref, r, buf):
        return pltpu.make_async_copy(x_hbm.at[pl.ds(idx_ref[0, 0, r], 1), :],
                                     xg_s.at[buf, pl.ds(r, 1), :], gsem.at[buf])

    def start_gather(idx_ref, buf):
        def body(r, c):
            gather_row(idx_ref, r, buf).start()
            return c
        lax.fori_loop(0, tm, body, 0, unroll=8)

    def scatter_row(r):
        return pltpu.make_async_copy(acc_s.at[pl.ds(r, 1), :],
                                     y_hbm.at[pl.ds(dst_ref[0, 0, r], 1), :], ssem.at[0])

    @pl.when(f == 0)
    def _():
        @pl.when(t == 0)
        def _():
            start_gather(src_ref, 0)

        @pl.when(t + 1 < pl.num_programs(0))
        def _():
            start_gather(srcn_ref, 1 - slot)

        def wait_body(r, c):
            gather_row(src_ref, r, slot).wait()
            return c
        lax.fori_loop(0, tm, wait_body, 0, unroll=8)
        xb_s[...] = xg_s[slot].astype(BF16)
        acc_s[...] = jnp.zeros_like(acc_s)

    n_valid = nv_ref[t]

    @pl.when(n_valid > 0)
    def _():
        acc_s[...] += _swiglu_partial(xb_s[...], wg_ref[0], wu_ref[0], wd_ref[0])

    @pl.when(f == pl.num_programs(1) - 1)
    def _():
        def start_body(r, c):
            scatter_row(r).start()
            return c

        def wait_body(r, c):
            scatter_row(r).wait()
            return c
        lax.fori_loop(0, n_valid, start_body, 0)
        lax.fori_loop(0, n_valid, wait_body, 0)


def _moe_experts(x, plan, w_gate, w_up, w_down):
    m, d = x.shape
    n_exp, _, ff = w_gate.shape
    tile_exp, n_valid, src, dst = plan
    n_tiles, _, tm = src.shape
    tf = _ff_tile(ff, 896)
    smem_rows = lambda imap: pl.BlockSpec((1, 1, tm), imap, memory_space=pltpu.SMEM)
    grid_spec = pltpu.PrefetchScalarGridSpec(
        num_scalar_prefetch=2,
        grid=(n_tiles, ff // tf),
        in_specs=[smem_rows(lambda t, f, te, nv: (t, 0, 0)),
                  smem_rows(lambda t, f, te, nv: (jnp.minimum(t + 1, n_tiles - 1), 0, 0)),
                  smem_rows(lambda t, f, te, nv: (t, 0, 0)),
                  pl.BlockSpec(memory_space=pl.ANY),
                  pl.BlockSpec((1, d, tf), lambda t, f, te, nv: (te[t], 0, f)),
                  pl.BlockSpec((1, d, tf), lambda t, f, te, nv: (te[t], 0, f)),
                  pl.BlockSpec((1, tf, d), lambda t, f, te, nv: (te[t], f, 0))],
        out_specs=pl.BlockSpec(memory_space=pl.ANY),
        scratch_shapes=[pltpu.VMEM((2, tm, d), F32), pltpu.VMEM((tm, d), BF16),
                        pltpu.VMEM((tm, d), F32),
                        pltpu.SemaphoreType.DMA((2,)), pltpu.SemaphoreType.DMA((1,))],
    )
    return pl.pallas_call(
        _moe_kernel,
        grid_spec=grid_spec,
        out_shape=jax.ShapeDtypeStruct((2 * m, d), F32),
        compiler_params=_cparams(("arbitrary", "arbitrary")),
        name="moe_experts",
    )(tile_exp, n_valid, src, src, dst, x,
      w_gate.astype(BF16), w_up.astype(BF16), w_down.astype(BF16))


def _moe_combine_kernel(alpha, x_ref, y_ref, w_ref, g_ref, b_ref, o_ref):
    w = w_ref[...]
    mix = y_ref[0] * w[:, 0:1] + y_ref[1] * w[:, 1:2]
    o_ref[...] = _layer_norm(alpha * x_ref[...] + mix, g_ref[...], b_ref[...])


def _moe_combine_deepnorm(x, y, wgt, ln_g, ln_b, alpha):
    m, d = x.shape
    tm = _tile(m, 512)
    return pl.pallas_call(
        functools.partial(_moe_combine_kernel, alpha),
        grid=(m // tm,),
        in_specs=[pl.BlockSpec((tm, d), lambda i: (i, 0)),
                  pl.BlockSpec((2, tm, d), lambda i: (0, i, 0)),
                  pl.BlockSpec((tm, LANES), lambda i: (i, 0)),
                  pl.BlockSpec((1, d), lambda i: (0, 0)),
                  pl.BlockSpec((1, d), lambda i: (0, 0))],
        out_specs=pl.BlockSpec((tm, d), lambda i: (i, 0)),
        out_shape=jax.ShapeDtypeStruct((m, d), F32),
        compiler_params=_cparams(("parallel",)),
        name="moe_combine_deepnorm",
    )(x, y.reshape(2, m, d), wgt, ln_g.reshape(1, d), ln_b.reshape(1, d))


def _moe_deepnorm(x, router, w_gate, w_up, w_down, ln_g, ln_b, alpha):
    m = x.shape[0]
    n_exp = router.shape[1]
    idx, wgt = _router_top2(x, router)
    plan = _moe_plan(idx, m, _tile(m, MOE_ROWS), n_exp)
    y = _moe_experts(x, plan, w_gate, w_up, w_down)
    return _moe_combine_deepnorm(x, y, wgt, ln_g, ln_b, alpha)


def _proj_pairs_kernel(transposed, x_ref, w_ref, o_ref, xb_s):
    @pl.when(pl.program_id(1) == 0)
    def _():
        xb_s[...] = x_ref[...].astype(BF16)

    n_slab = o_ref.shape[0]
    if transposed:
        res = _dot_nt(w_ref[...], xb_s[...])
        for q in range(n_slab):
            o_ref[q] = res[q * LANES:(q + 1) * LANES, :].astype(o_ref.dtype)
    else:
        res = _dot(xb_s[...], w_ref[...])
        for q in range(n_slab):
            o_ref[q] = res[:, q * LANES:(q + 1) * LANES].astype(o_ref.dtype)


def _proj_pairs(x, w, transposed, out_dtype=F32):
    m, d = x.shape
    n = w.shape[1]
    tm = _tile(m, 512)
    tn = _tile(n, 256)
    n_slab = tn // LANES
    if transposed:
        wb = w.T.astype(BF16)
        w_spec = pl.BlockSpec((tn, d), lambda i, j: (j, 0))
        o_spec = pl.BlockSpec((n_slab, LANES, tm), lambda i, j: (j, 0, i))
        o_shape = jax.ShapeDtypeStruct((n // LANES, LANES, m), out_dtype)
    else:
        wb = w.astype(BF16)
        w_spec = pl.BlockSpec((d, tn), lambda i, j: (0, j))
        o_spec = pl.BlockSpec((n_slab, tm, LANES), lambda i, j: (j, i, 0))
        o_shape = jax.ShapeDtypeStruct((n // LANES, m, LANES), out_dtype)
    return pl.pallas_call(
        functools.partial(_proj_pairs_kernel, transposed),
        grid=(m // tm, n // tn),
        in_specs=[pl.BlockSpec((tm, d), lambda i, j: (i, 0)), w_spec],
        out_specs=o_spec,
        out_shape=o_shape,
        scratch_shapes=[pltpu.VMEM((tm, d), BF16)],
        compiler_params=_cparams(("parallel", "arbitrary")),
        name="proj_pairs_t" if transposed else "proj_pairs",
    )(x, wb)


def _block_mean_kernel(k_ref, o_ref):
    k = k_ref[0]
    nb = k.shape[0] // MOBA_BLOCK
    o_ref[0] = jnp.mean(k.reshape(nb, MOBA_BLOCK, LANES), axis=1)


def _block_means(k_pairs, batch):
    n_pair, m, _ = k_pairs.shape
    seq = m // batch
    nb = seq // MOBA_BLOCK
    return pl.pallas_call(
        _block_mean_kernel,
        grid=(n_pair, batch),
        in_specs=[pl.BlockSpec((1, seq, LANES), lambda p, b: (p, b, 0))],
        out_specs=pl.BlockSpec((1, nb, LANES), lambda p, b: (p, b, 0)),
        out_shape=jax.ShapeDtypeStruct((n_pair, batch * nb, LANES), F32),
        compiler_params=_cparams(("parallel", "parallel")),
        name="moba_block_means",
    )(k_pairs)


def _moba_kernel(scale, qt_ref, k_ref, vt_ref, km_ref, o_ref, neg_s):
    own = pl.program_id(2)
    blk = MOBA_BLOCK
    n_slab = qt_ref.shape[0]
    nb = km_ref.shape[1]
    row = lax.broadcasted_iota(jnp.int32, (LANES, blk), 0)
    zero_q = jnp.zeros((LANES, blk), F32)
    n_iota = lax.broadcasted_iota(jnp.int32, (nb, blk), 0).astype(F32)
    past = n_iota < own.astype(F32)
    neg_inf = jnp.float32(-jnp.inf)

    streams = [(g, h) for g in range(n_slab) for h in range(2)]
    qh = []
    for s, (g, h) in enumerate(streams):
        in_head = (row < HEAD_DIM) if h == 0 else (row >= HEAD_DIM)
        q_h = jnp.where(in_head, qt_ref[g], zero_q)
        qh.append((q_h * (scale * LOG2E)).astype(BF16))
        gate = _dot_sp(_split(km_ref[g]), _split(q_h))
        gate = jnp.where(past, gate, neg_inf)
        neg = jnp.full((nb, blk), NEG_BIG, F32)
        for _ in range(min(MOBA_TOPK, nb)):
            mx = jnp.max(gate, axis=0, keepdims=True)
            idx = jnp.min(jnp.where(gate == mx, n_iota, float(nb)), axis=0, keepdims=True)
            pick = n_iota == idx
            neg = jnp.where(jnp.logical_and(pick, past), 0.0, neg)
            gate = jnp.where(pick, neg_inf, gate)
        neg_s[s] = neg

    sub = blk // MOBA_KEY_SPLIT
    ones_rows = jnp.ones((DEN_ROWS, sub), BF16)

    def absorb(n, causal_mask, carry):
        start = pl.multiple_of(n * blk, blk)
        out = list(carry)
        items = [(sb, s) for sb in range(MOBA_KEY_SPLIT) for s in range(len(streams))]

        def scores(w):
            sb, s = items[w]
            kb = k_ref[streams[s][0], pl.ds(start + sb * sub, sub), :]
            return _dot(kb, qh[s])

        st_q = [scores(w) for w in range(min(MOBA_AHEAD, len(items)))]
        for w, (sb, s) in enumerate(items):
            g, h = streams[s]
            st = st_q[w]
            st_q[w] = None
            mx, acc = out[2 * s], out[2 * s + 1]
            if causal_mask is not None:
                st = jnp.where(causal_mask[sb], st, neg_inf)
                bias = None
                mblk = jnp.max(st, axis=0, keepdims=True)
            else:
                bias = neg_s[s, pl.ds(n, 1), :]
                mblk = jnp.max(st, axis=0, keepdims=True) + bias
            mx_new = jnp.maximum(mx, mblk)
            alpha = jnp.exp2(mx - mx_new)
            shift = mx_new if bias is None else mx_new - bias
            p = jnp.exp2(st - shift).astype(BF16)
            vtb = vt_ref[g, h * HEAD_DIM:(h + 1) * HEAD_DIM, pl.ds(start + sb * sub, sub)]
            vt_ext = jnp.concatenate([vtb, ones_rows], axis=0)
            out[2 * s] = mx_new
            out[2 * s + 1] = alpha * acc + _dot(vt_ext, p)
            if w + MOBA_AHEAD < len(items):
                st_q.append(scores(w + MOBA_AHEAD))
        return tuple(out)

    kpos = lax.broadcasted_iota(jnp.int32, (sub, blk), 0)
    qpos = lax.broadcasted_iota(jnp.int32, (sub, blk), 1)
    causal = [kpos + sb * sub <= qpos for sb in range(MOBA_KEY_SPLIT)]
    init = []
    for _ in streams:
        init += [jnp.full((1, blk), neg_inf, F32), jnp.zeros((HEAD_DIM + DEN_ROWS, blk), F32)]
    carry = absorb(own, causal, tuple(init))
    fin = lax.fori_loop(0, own, lambda n, c: absorb(n, None, c), carry)
    for g in range(n_slab):
        a0, a1 = fin[4 * g + 1], fin[4 * g + 3]
        o_t = jnp.concatenate([a0[:HEAD_DIM] / a0[HEAD_DIM:HEAD_DIM + 1],
                               a1[:HEAD_DIM] / a1[HEAD_DIM:HEAD_DIM + 1]], axis=0)
        o_ref[g] = o_t.T.astype(o_ref.dtype)


def _moba_attention(q_t, k, v_t, k_means, batch):
    n_pair, _, m = q_t.shape
    seq = m // batch
    nb = seq // MOBA_BLOCK
    g = _tile(n_pair, MOBA_SLABS)
    return pl.pallas_call(
        functools.partial(_moba_kernel, HEAD_DIM ** -0.5),
        grid=(n_pair // g, batch, nb),
        in_specs=[pl.BlockSpec((g, LANES, MOBA_BLOCK), lambda p, b, i: (p, 0, b * nb + i)),
                  pl.BlockSpec((g, seq, LANES), lambda p, b, i: (p, b, 0)),
                  pl.BlockSpec((g, LANES, seq), lambda p, b, i: (p, 0, b)),
                  pl.BlockSpec((g, nb, LANES), lambda p, b, i: (p, b, 0))],
        out_specs=pl.BlockSpec((g, MOBA_BLOCK, LANES), lambda p, b, i: (p, b * nb + i, 0)),
        out_shape=jax.ShapeDtypeStruct((n_pair, m, LANES), BF16),
        scratch_shapes=[pltpu.VMEM((2 * g, nb, MOBA_BLOCK), F32)],
        compiler_params=_cparams(("parallel", "parallel", "arbitrary")),
        name="moba_attention",
    )(q_t, k, v_t, k_means)


def kernel(x, rwkv_mu, rwkv_w_rkv, rwkv_w_out, rwkv_decay_w0, rwkv_decay_w1, rwkv_decay_w2, rwkv_iclr_a0, rwkv_iclr_a1, rwkv_iclr_a2, rwkv_vres_v0, rwkv_vres_v1, rwkv_vres_v2, rwkv_gate_g1, rwkv_gate_g2, rwkv_k_k, rwkv_k_a, rwkv_r_k, rwkv_gn_g, rwkv_gn_b, moba_w_k, moba_w_v, moba_w_q, moba_w_o, ffn_w_gate, ffn_w_up, ffn_w_down, moe_router, moe_w_gate, moe_w_up, moe_w_down, ln_g, ln_b):
    batch, seq, d = x.shape
    assert d % (2 * LANES) == 0 and seq % MOBA_BLOCK == 0 and seq % WKV_CHUNK == 0
    depth = ln_g.shape[0]
    n_rwkv = rwkv_mu.shape[0]
    alpha = (2.0 * depth) ** 0.25
    h = x.reshape(batch * seq, d)
    v_first = None
    kv = None
    for layer in range(depth):
        if layer < n_rwkv:
            i = layer
            vres = None if i == 0 else (rwkv_vres_v0[i - 1], rwkv_vres_v1[i - 1], rwkv_vres_v2[i - 1])
            r, e, k, v, a, b, g = _rwkv_proj(
                h, seq, rwkv_mu[i], rwkv_w_rkv[i], rwkv_decay_w1[i], rwkv_iclr_a1[i],
                rwkv_gate_g1[i], rwkv_decay_w2[i], rwkv_iclr_a2[i], rwkv_gate_g2[i],
                rwkv_decay_w0[i], rwkv_iclr_a0[i], rwkv_k_k[i], rwkv_k_a[i],
                vres=vres, v_first=v_first)
            if i == 0:
                v_first = v
            y = _wkv_scan(r, e, k, v, a, b, batch)
            mix = _rwkv_post(y, r, k, v, g, rwkv_r_k[i], rwkv_gn_g[i], rwkv_gn_b[i])
            w_out = rwkv_w_out[i]
        else:
            jdx = layer - n_rwkv
            k_pairs, v_t, k_means = kv
            q_t = _proj_pairs(h, moba_w_q[jdx], transposed=True)
            mix = _moba_attention(q_t, k_pairs, v_t, k_means, batch)
            w_out = moba_w_o[jdx]
        h = _out_proj_deepnorm(mix, w_out, h, ln_g[layer, 0], ln_b[layer, 0], alpha)
        ex = layer // 2
        if layer % 2 == 0:
            h = _ffn_deepnorm(h, ffn_w_gate[ex], ffn_w_up[ex], ffn_w_down[ex],
                              ln_g[layer, 1], ln_b[layer, 1], alpha)
        else:
            h = _moe_deepnorm(h, moe_router[ex], moe_w_gate[ex], moe_w_up[ex], moe_w_down[ex],
                              ln_g[layer, 1], ln_b[layer, 1], alpha)
        if layer == n_rwkv - 1:
            k_pairs = _proj_pairs(h, moba_w_k, transposed=False)
            v_t = _proj_pairs(h, moba_w_v, transposed=True, out_dtype=BF16)
            kv = (k_pairs.astype(BF16), v_t, _block_means(k_pairs, batch))
    return h.reshape(batch, seq, d)
```

```python
import functools
import math

import jax
import jax.numpy as jnp
from jax import lax
from jax.experimental import pallas as pl
from jax.experimental.pallas import tpu as pltpu

HEAD_DIM = 64
LANES = 128
GN_EPS = 64e-5
LN_EPS = 1e-5
MOBA_BLOCK = 256
MOBA_TOPK = 3
MOE_ROWS = 512
WKV_CHUNK = 64
WKV_SLABS = 8
NEG_BIG = -1e30
MOBA_SLABS = 4
DEN_ROWS = 16
LOG2E = 1.4426950408889634

F32 = jnp.float32
BF16 = jnp.bfloat16
VMEM_LIMIT = 56 * 1024 * 1024


def _cparams(sem):
    return pltpu.CompilerParams(dimension_semantics=sem, vmem_limit_bytes=VMEM_LIMIT)


def _dot(a, b):
    return jnp.dot(a, b, preferred_element_type=F32)


def _dot_nt(a, b):
    return lax.dot_general(a, b, (((1,), (1,)), ((), ())), preferred_element_type=F32)


def _split(x):
    hi = x.astype(BF16)
    lo = (x - hi.astype(F32)).astype(BF16)
    return hi, lo


def _dot_sp(a, b, nt=False):
    d = _dot_nt if nt else _dot
    return d(a[0], b[0]) + (d(a[0], b[1]) + d(a[1], b[0]))


def _dot_exact_rhs(a, b_exact, nt=False):
    d = _dot_nt if nt else _dot
    hi = a.astype(BF16)
    r1 = a - hi.astype(F32)
    mid = r1.astype(BF16)
    lo = (r1 - mid.astype(F32)).astype(BF16)
    return d(hi, b_exact) + (d(mid, b_exact) + d(lo, b_exact))


def _sigmoid(x):
    return 1.0 / (1.0 + jnp.exp(-x))


def _layer_norm(y, g, b):
    mu = jnp.mean(y, axis=-1, keepdims=True)
    yc = y - mu
    var = jnp.mean(yc * yc, axis=-1, keepdims=True)
    return yc * lax.rsqrt(var + LN_EPS) * g + b


def _head_blockdiag(n):
    i = jnp.arange(n) // HEAD_DIM
    return (i[:, None] == i[None, :]).astype(BF16)


def _tile(n, want):
    t = min(n, want)
    assert n % t == 0, (n, want)
    return t


def _rwkv_proj_kernel(has_vres, steps_per_seq, *refs):
    if has_vres:
        (x_ref, xp_ref, mu_ref, wrkv_ref, wd1_ref, wa1_ref, wg1_ref, wv1_ref,
         wd2_ref, wa2_ref, wg2_ref, wv2_ref, w0_ref, a0_ref, v0_ref, kk_ref, ka_ref,
         bd_ref, vf_ref,
         r_out, e_out, k_out, v_out, a_out, b_out, g_out,
         xm_s, hd_s, ha_s, hg_s, hv_s) = refs
    else:
        (x_ref, xp_ref, mu_ref, wrkv_ref, wd1_ref, wa1_ref, wg1_ref,
         wd2_ref, wa2_ref, wg2_ref, w0_ref, a0_ref, kk_ref, ka_ref,
         bd_ref,
         r_out, e_out, k_out, v_out, a_out, b_out, g_out,
         xm_s, hd_s, ha_s, hg_s) = refs
    i = pl.program_id(0)
    j = pl.program_id(1)

    @pl.when(j == 0)
    def _():
        x = x_ref[...]
        tm = x.shape[0]
        prev_row = jnp.where(i % steps_per_seq == 0, 0.0, xp_ref[7:8, :])
        rolled = pltpu.roll(x, 1, axis=0)
        row = lax.broadcasted_iota(jnp.int32, (tm, 1), 0)
        x_prev = jnp.where(row == 0, prev_row, rolled)
        xx = x_prev - x
        for c in range(3):
            xm_s[c] = (x + xx * mu_ref[c:c + 1, :]).astype(BF16)
        xw = (x + xx * mu_ref[3:4, :]).astype(BF16)
        xa = (x + xx * mu_ref[4:5, :]).astype(BF16)
        xg = (x + xx * mu_ref[5:6, :]).astype(BF16)
        hd_s[...] = jnp.tanh(_dot(xw, wd1_ref[...])).astype(BF16)
        ha_s[...] = _dot(xa, wa1_ref[...]).astype(BF16)
        hg_s[...] = _sigmoid(_dot(xg, wg1_ref[...])).astype(BF16)
        if has_vres:
            hv_s[...] = _dot(xm_s[2], wv1_ref[...]).astype(BF16)

    r = _dot(xm_s[0], wrkv_ref[0])
    k = _dot(xm_s[1], wrkv_ref[1])
    v = _dot(xm_s[2], wrkv_ref[2])
    z = w0_ref[...] + _dot(hd_s[...], wd2_ref[...])
    nz = -z
    softplus = jnp.maximum(nz, 0.0) + jnp.log(1.0 + jnp.exp(-jnp.abs(nz)))
    e = jnp.exp(-softplus - 0.5)
    a = _sigmoid(a0_ref[...] + _dot(ha_s[...], wa2_ref[...]))
    g = _dot(hg_s[...], wg2_ref[...])
    n_slab = r.shape[1] // LANES
    if has_vres:
        vf = jnp.concatenate([vf_ref[q] for q in range(n_slab)], axis=1)
        v = v + (vf - v) * _sigmoid(v0_ref[...] + _dot(hv_s[...], wv2_ref[...]))
    kk = k * kk_ref[...]
    ss = _dot_exact_rhs(kk * kk, bd_ref[...])
    kk = kk * lax.rsqrt(jnp.maximum(ss, 1e-24))
    k = k * (1.0 + (a - 1.0) * ka_ref[...])
    for q in range(n_slab):
        sl = slice(q * LANES, (q + 1) * LANES)
        r_out[q] = r[:, sl]
        e_out[q] = e[:, sl]
        k_out[q] = k[:, sl]
        v_out[q] = v[:, sl]
        a_out[q] = -kk[:, sl]
        b_out[q] = (kk * a)[:, sl]
        g_out[q] = g[:, sl]


def _rwkv_proj(x, seq_len, mu, w_rkv, wd1, wa1, wg1, wd2, wa2, wg2, w0, a0, k_k, k_a,
               vres=None, v_first=None):
    m, d = x.shape
    tm = _tile(seq_len, 512)
    tn = _tile(d, 256)
    n_slab = tn // LANES
    has_vres = vres is not None
    row = lambda a: a.reshape(1, d)
    full = lambda a: pl.BlockSpec(a.shape, lambda i, j: (0,) * a.ndim)
    colblk = lambda rows: pl.BlockSpec((rows, tn), lambda i, j: (0, j))
    bd = _head_blockdiag(tn)
    wd1, wa1, wg1 = wd1.astype(BF16), wa1.astype(BF16), wg1.astype(BF16)
    wd2, wa2, wg2 = wd2.astype(BF16), wa2.astype(BF16), wg2.astype(BF16)
    w_rkv = w_rkv.astype(BF16)
    args = [x, x, mu, w_rkv, wd1, wa1, wg1]
    specs = [pl.BlockSpec((tm, d), lambda i, j: (i, 0)),
             pl.BlockSpec((8, d), lambda i, j: (jnp.maximum(i * (tm // 8) - 1, 0), 0)),
             full(mu),
             pl.BlockSpec((3, d, tn), lambda i, j: (0, 0, j)),
             full(wd1), full(wa1), full(wg1)]
    if has_vres:
        v0, wv1, wv2 = vres
        wv1, wv2 = wv1.astype(BF16), wv2.astype(BF16)
        args += [wv1]
        specs += [full(wv1)]
    args += [wd2, wa2, wg2]
    specs += [colblk(wd2.shape[0]), colblk(wa2.shape[0]), colblk(wg2.shape[0])]
    if has_vres:
        args += [wv2]
        specs += [colblk(wv2.shape[0])]
    args += [row(w0), row(a0)]
    specs += [colblk(1), colblk(1)]
    if has_vres:
        args += [row(v0)]
        specs += [colblk(1)]
    args += [row(k_k), row(k_a), bd]
    specs += [colblk(1), colblk(1), full(bd)]
    slab_spec = pl.BlockSpec((n_slab, tm, LANES), lambda i, j: (j, i, 0))
    if has_vres:
        args += [v_first]
        specs += [slab_spec]
    slab = jax.ShapeDtypeStruct((d // LANES, m, LANES), F32)
    scratch = [pltpu.VMEM((3, tm, d), BF16),
               pltpu.VMEM((tm, wd1.shape[1]), BF16),
               pltpu.VMEM((tm, wa1.shape[1]), BF16),
               pltpu.VMEM((tm, wg1.shape[1]), BF16)]
    if has_vres:
        scratch.append(pltpu.VMEM((tm, wv1.shape[1]), BF16))
    return pl.pallas_call(
        functools.partial(_rwkv_proj_kernel, has_vres, seq_len // tm),
        grid=(m // tm, d // tn),
        in_specs=specs,
        out_specs=[slab_spec] * 7,
        out_shape=[slab] * 7,
        scratch_shapes=scratch,
        compiler_params=_cparams(("parallel", "arbitrary")),
        name="rwkv_proj",
    )(*args)


def _wkv_kernel(r_ref, e_ref, k_ref, v_ref, a_ref, b_ref, y_ref, s_ref):
    c = WKV_CHUNK
    n_chunks = r_ref.shape[1] // c

    @pl.when(pl.program_id(2) == 0)
    def _():
        s_ref[...] = jnp.zeros_like(s_ref)

    lane = lax.broadcasted_iota(jnp.int32, (c, LANES), 1)
    t_idx = lax.broadcasted_iota(jnp.int32, (c, LANES), 0)
    head0 = lane < HEAD_DIM
    s_idx = jnp.bitwise_and(lane, HEAD_DIM - 1)
    strict = s_idx < t_idx
    incl = s_idx <= t_idx
    tri = jnp.where(lax.broadcasted_iota(jnp.int32, (c, c), 1)
                    <= lax.broadcasted_iota(jnp.int32, (c, c), 0), 1.0, 0.0).astype(BF16)
    rr = lax.broadcasted_iota(jnp.int32, (LANES, LANES), 0)
    cc = lax.broadcasted_iota(jnp.int32, (LANES, LANES), 1)
    same_head = (rr < HEAD_DIM) == (cc < HEAD_DIM)

    def bd(x):
        z = jnp.zeros_like(x)
        return jnp.concatenate([jnp.where(head0, x, z), jnp.where(head0, z, x)], axis=0)

    def cat(a, b, axis=0):
        return jnp.concatenate([a, b], axis=axis)

    def chunk(ci, carry):
        stages = [one_slab(ci, p) for p in range(r_ref.shape[0])]
        while stages:
            stages = [g for g in stages if next(g, "done") != "done"]
        return carry

    def one_slab(ci, p):
        sl = pl.ds(pl.multiple_of(ci * c, c), c)
        r = r_ref[p, sl, :]
        e = e_ref[p, sl, :]
        k = k_ref[p, sl, :]
        v = v_ref[p, sl, :]
        a = a_ref[p, sl, :]
        b = b_ref[p, sl, :]
        s0 = s_ref[p]
        cum = _cumsum(tri, e)
        yield
        tot = cum[c - 1:c, :]
        ar = cat(a * jnp.exp(e - cum), r * jnp.exp(-cum)).astype(BF16)
        ec = jnp.exp(cum)
        bt = (b * ec).astype(BF16)
        kt = (k * ec).astype(BF16)
        eh = jnp.exp(cum - tot)
        bk = cat(b * eh, k * eh).astype(BF16)
        bdv = bd(v.astype(BF16))
        gram = _dot_nt(ar, cat(bd(bt), bd(kt)))
        yield
        zero = jnp.zeros((c, LANES), F32)
        l_ab = jnp.where(strict, gram[:c, :LANES], zero)
        n_ak = jnp.where(strict, gram[:c, LANES:], zero)
        m_rb = jnp.where(incl, gram[c:, :LANES], zero)
        m_rk = jnp.where(incl, gram[c:, LANES:], zero)
        xy = _dot_nt(ar, s0.astype(BF16)) + _dot(cat(n_ak, m_rk).astype(BF16), bdv)
        x = xy[:c]
        y0 = xy[c:]
        yield
        lp = l_ab.astype(BF16)
        n_steps = int(math.log2(c))
        for step in range(n_steps):
            bdx = bd(x.astype(BF16))
            if step + 1 < n_steps:
                t = _dot(lp, cat(bdx, bd(lp), axis=1))
                x = x + t[:, :LANES]
                lp = t[:, LANES:].astype(BF16)
            else:
                x = x + _dot(lp, bdx)
            yield
        y_ref[p, sl, :] = y0 + _dot(m_rb.astype(BF16), bd(x.astype(BF16)))
        uv_t = cat(x, v).T.astype(BF16)
        upd = _dot(uv_t, bk)
        s_ref[p] = s0 * jnp.exp(-tot) + jnp.where(same_head, upd, jnp.zeros_like(upd))
        yield

    lax.fori_loop(0, n_chunks, chunk, 0)


def _cumsum(tri, e):
    hi = e.astype(BF16)
    r1 = e - hi.astype(F32)
    mid = r1.astype(BF16)
    lo = (r1 - mid.astype(F32)).astype(BF16)
    return _dot(tri, hi) + (_dot(tri, mid) + _dot(tri, lo))


def _wkv_scan(r, e, k, v, a, b, batch):
    n_pair, m, _ = r.shape
    seq = m // batch
    tb = _tile(seq, 256)
    steps = seq // tb
    n_slab = _tile(n_pair, WKV_SLABS)
    spec = pl.BlockSpec((n_slab, tb, LANES), lambda p, bb, t: (p, bb * steps + t, 0))
    return pl.pallas_call(
        _wkv_kernel,
        grid=(n_pair // n_slab, batch, steps),
        in_specs=[spec] * 6,
        out_specs=spec,
        out_shape=jax.ShapeDtypeStruct((n_pair, m, LANES), F32),
        scratch_shapes=[pltpu.VMEM((n_slab, LANES, LANES), F32)],
        compiler_params=_cparams(("parallel", "parallel", "arbitrary")),
        name="wkv_scan",
    )(r, e, k, v, a, b)


def _rwkv_post_kernel(y_ref, r_ref, k_ref, v_ref, g_ref, rk_ref, gg_ref, gb_ref, bd_ref, z_ref):
    y = y_ref[0]
    bdm = bd_ref[...]
    inv_n = 1.0 / HEAD_DIM
    mean = _dot_exact_rhs(y, bdm) * inv_n
    yc = y - mean
    var = _dot_exact_rhs(yc * yc, bdm) * inv_n
    yn = yc * lax.rsqrt(var + GN_EPS) * gg_ref[0] + gb_ref[0]
    bonus = _dot_exact_rhs(r_ref[0] * k_ref[0] * rk_ref[0], bdm) * v_ref[0]
    z_ref[0] = ((yn + bonus) * g_ref[0]).astype(z_ref.dtype)


def _rwkv_post(y, r, k, v, g, r_k, gn_g, gn_b):
    n_pair, m, _ = y.shape
    tm = _tile(m, 1024)
    spec = pl.BlockSpec((1, tm, LANES), lambda p, i: (p, i, 0))
    pspec = pl.BlockSpec((1, 1, LANES), lambda p, i: (p, 0, 0))
    bd = _head_blockdiag(LANES)
    slab = lambda a: a.reshape(n_pair, 1, LANES)
    return pl.pallas_call(
        _rwkv_post_kernel,
        grid=(n_pair, m // tm),
        in_specs=[spec] * 5 + [pspec] * 3 + [pl.BlockSpec(bd.shape, lambda p, i: (0, 0))],
        out_specs=spec,
        out_shape=jax.ShapeDtypeStruct((n_pair, m, LANES), BF16),
        compiler_params=_cparams(("parallel", "parallel")),
        name="rwkv_post",
    )(y, r, k, v, g, slab(r_k), slab(gn_g), slab(gn_b), bd)


def _out_proj_kernel(alpha, z_ref, w_ref, x_ref, g_ref, b_ref, o_ref):
    n_pair = z_ref.shape[0]
    acc = _dot(z_ref[0], w_ref[0])
    for p in range(1, n_pair):
        acc = acc + _dot(z_ref[p], w_ref[p])
    o_ref[...] = _layer_norm(alpha * x_ref[...] + acc, g_ref[...], b_ref[...])


def _out_proj_deepnorm(z, w, x, ln_g, ln_b, alpha):
    n_pair, m, _ = z.shape
    d = w.shape[1]
    tm = _tile(m, 512)
    w3 = w.astype(BF16).reshape(n_pair, LANES, d)
    return pl.pallas_call(
        functools.partial(_out_proj_kernel, alpha),
        grid=(m // tm,),
        in_specs=[pl.BlockSpec((n_pair, tm, LANES), lambda i: (0, i, 0)),
                  pl.BlockSpec(w3.shape, lambda i: (0, 0, 0)),
                  pl.BlockSpec((tm, d), lambda i: (i, 0)),
                  pl.BlockSpec((1, d), lambda i: (0, 0)),
                  pl.BlockSpec((1, d), lambda i: (0, 0))],
        out_specs=pl.BlockSpec((tm, d), lambda i: (i, 0)),
        out_shape=jax.ShapeDtypeStruct((m, d), F32),
        compiler_params=_cparams(("parallel",)),
        name="out_proj_deepnorm",
    )(z, w3, x, ln_g.reshape(1, d), ln_b.reshape(1, d))


def _ff_tile(ff, want):
    for cand in range(min(ff, want) // LANES, 0, -1):
        if ff % (cand * LANES) == 0:
            return cand * LANES
    return ff


def _swiglu_partial(xb, wg, wu, wd):
    h1 = _dot(xb, wg)
    h2 = _dot(xb, wu)
    act = h1 * _sigmoid(h1) * h2
    return _dot(act.astype(BF16), wd)


def _ffn_kernel(alpha, x_ref, wg_ref, wu_ref, wd_ref, g_ref, b_ref, o_ref, xb_s, acc_s):
    f = pl.program_id(1)

    @pl.when(f == 0)
    def _():
        xb_s[...] = x_ref[...].astype(BF16)
        acc_s[...] = jnp.zeros_like(acc_s)

    acc_s[...] += _swiglu_partial(xb_s[...], wg_ref[...], wu_ref[...], wd_ref[...])

    @pl.when(f == pl.num_programs(1) - 1)
    def _():
        o_ref[...] = _layer_norm(alpha * x_ref[...] + acc_s[...], g_ref[...], b_ref[...])


def _ffn_deepnorm(x, w_gate, w_up, w_down, ln_g, ln_b, alpha):
    m, d = x.shape
    ff = w_gate.shape[1]
    tm = _tile(m, 512)
    tf = _ff_tile(ff, 1408)
    return pl.pallas_call(
        functools.partial(_ffn_kernel, alpha),
        grid=(m // tm, ff // tf),
        in_specs=[pl.BlockSpec((tm, d), lambda i, f: (i, 0)),
                  pl.BlockSpec((d, tf), lambda i, f: (0, f)),
                  pl.BlockSpec((d, tf), lambda i, f: (0, f)),
                  pl.BlockSpec((tf, d), lambda i, f: (f, 0)),
                  pl.BlockSpec((1, d), lambda i, f: (0, 0)),
                  pl.BlockSpec((1, d), lambda i, f: (0, 0))],
        out_specs=pl.BlockSpec((tm, d), lambda i, f: (i, 0)),
        out_shape=jax.ShapeDtypeStruct((m, d), F32),
        scratch_shapes=[pltpu.VMEM((tm, d), BF16), pltpu.VMEM((tm, d), F32)],
        compiler_params=_cparams(("parallel", "arbitrary")),
        name="ffn_deepnorm",
    )(x, w_gate.astype(BF16), w_up.astype(BF16), w_down.astype(BF16),
      ln_g.reshape(1, d), ln_b.reshape(1, d))


def _router_kernel(n_exp, x_ref, w_ref, idx_ref, wgt_ref):
    logits = _dot_sp(_split(x_ref[...]), _split(w_ref[...]))
    lane = lax.broadcasted_iota(jnp.int32, logits.shape, 1).astype(F32)
    neg_inf = jnp.float32(-jnp.inf)
    lg = jnp.where(lane < n_exp, logits, neg_inf)
    m1 = jnp.max(lg, axis=1, keepdims=True)
    i1 = jnp.min(jnp.where(lg == m1, lane, float(LANES)), axis=1, keepdims=True)
    lg2 = jnp.where(lane == i1, neg_inf, lg)
    m2 = jnp.max(lg2, axis=1, keepdims=True)
    i2 = jnp.min(jnp.where(lg2 == m2, lane, float(LANES)), axis=1, keepdims=True)
    e2 = jnp.exp(m2 - m1)
    den = 1.0 + e2
    idx_ref[...] = jnp.where(lane == 0.0, i1, jnp.where(lane == 1.0, i2, 0.0)).astype(jnp.int32)
    wgt_ref[...] = jnp.where(lane == 0.0, 1.0 / den, jnp.where(lane == 1.0, e2 / den, 0.0))


def _router_top2(x, router):
    m, d = x.shape
    n_exp = router.shape[1]
    tm = _tile(m, 512)
    w = jnp.pad(router, ((0, 0), (0, LANES - n_exp)))
    out_spec = pl.BlockSpec((tm, LANES), lambda i: (i, 0))
    return pl.pallas_call(
        functools.partial(_router_kernel, n_exp),
        grid=(m // tm,),
        in_specs=[pl.BlockSpec((tm, d), lambda i: (i, 0)),
                  pl.BlockSpec((d, LANES), lambda i: (0, 0))],
        out_specs=[out_spec, out_spec],
        out_shape=[jax.ShapeDtypeStruct((m, LANES), jnp.int32),
                   jax.ShapeDtypeStruct((m, LANES), F32)],
        compiler_params=_cparams(("parallel",)),
        name="moe_router",
    )(x, w)


def _moe_plan(idx, m, tm, n_exp):
    n_ent = 2 * m
    expert = jnp.concatenate([idx[:, 0], idx[:, 1]])
    onehot = (expert[:, None] == jnp.arange(n_exp, dtype=jnp.int32)[None, :]).astype(jnp.int32)
    csum = jnp.cumsum(onehot, axis=0)
    counts = csum[-1]
    rank = jnp.sum(csum * onehot, axis=1) - 1
    padded = ((counts + tm - 1) // tm) * tm
    ends = jnp.cumsum(padded)
    starts = ends - padded
    dest = jnp.sum(starts[None, :] * onehot, axis=1) + rank
    n_tiles = n_ent // tm + n_exp
    tile_start = jnp.arange(n_tiles, dtype=jnp.int32) * tm
    tile_exp = jnp.minimum(jnp.sum((tile_start[:, None] >= ends[None, :]).astype(jnp.int32), axis=1),
                           n_exp - 1)
    n_valid = jnp.clip(starts[tile_exp] + counts[tile_exp] - tile_start, 0, tm)
    fill_lo = jnp.concatenate([starts + counts, ends[-1:]])
    fill_hi = jnp.concatenate([ends, jnp.full((1,), n_tiles * tm, jnp.int32)])
    return dest, tile_exp, n_valid, fill_lo, fill_hi


def _moe_permute_kernel(lo_ref, hi_ref, dest_ref, x_hbm, xs_hbm, sem):
    i = pl.program_id(0)
    te = dest_ref.shape[2]
    base = lax.rem(i * te, x_hbm.shape[0])

    def entry_row(r):
        return pltpu.make_async_copy(x_hbm.at[pl.ds(base + r, 1), :],
                                     xs_hbm.at[pl.ds(dest_ref[0, 0, r], 1), :], sem.at[0])

    def filler_row(r):
        return pltpu.make_async_copy(x_hbm.at[pl.ds(0, 1), :], xs_hbm.at[pl.ds(r, 1), :], sem.at[1])

    def each(lo, hi, fn, unroll=1):
        def body(r, c):
            fn(r)
            return c
        lax.fori_loop(lo, hi, body, 0, unroll=unroll)

    each(0, te, lambda r: entry_row(r).start(), unroll=8)

    @pl.when(i == 0)
    def _():
        for e in range(lo_ref.shape[0]):
            each(lo_ref[e], hi_ref[e], lambda r: filler_row(r).start())
        for e in range(lo_ref.shape[0]):
            each(lo_ref[e], hi_ref[e], lambda r: filler_row(r).wait())

    each(0, te, lambda r: entry_row(r).wait(), unroll=8)


def _moe_permute(x, dest, fill_lo, fill_hi, n_rows):
    m, d = x.shape
    te = _tile(m, 2048)
    dest3 = dest.reshape(-1, 1, te)
    grid_spec = pltpu.PrefetchScalarGridSpec(
        num_scalar_prefetch=2,
        grid=(dest3.shape[0],),
        in_specs=[pl.BlockSpec((1, 1, te), lambda i, lo, hi: (i, 0, 0), memory_space=pltpu.SMEM),
                  pl.BlockSpec(memory_space=pl.ANY)],
        out_specs=pl.BlockSpec(memory_space=pl.ANY),
        scratch_shapes=[pltpu.SemaphoreType.DMA((2,))],
    )
    return pl.pallas_call(
        _moe_permute_kernel,
        grid_spec=grid_spec,
        out_shape=jax.ShapeDtypeStruct((n_rows, d), F32),
        compiler_params=_cparams(("arbitrary",)),
        name="moe_permute",
    )(fill_lo, fill_hi, dest3, x)


def _moe_experts_kernel(te_ref, nv_ref, x_ref, wg_ref, wu_ref, wd_ref, y_ref, xb_s, acc_s):
    t = pl.program_id(0)
    f = pl.program_id(1)

    @pl.when(f == 0)
    def _():
        xb_s[...] = x_ref[...].astype(BF16)
        acc_s[...] = jnp.zeros_like(acc_s)

    @pl.when(nv_ref[t] > 0)
    def _():
        acc_s[...] += _swiglu_partial(xb_s[...], wg_ref[0], wu_ref[0], wd_ref[0])

    @pl.when(f == pl.num_programs(1) - 1)
    def _():
        y_ref[...] = acc_s[...]


def _moe_experts(xs, tile_exp, n_valid, tm, w_gate, w_up, w_down):
    n_rows, d = xs.shape
    ff = w_gate.shape[2]
    tf = _ff_tile(ff, 896)
    grid_spec = pltpu.PrefetchScalarGridSpec(
        num_scalar_prefetch=2,
        grid=(n_rows // tm, ff // tf),
        in_specs=[pl.BlockSpec((tm, d), lambda t, f, te, nv: (t, 0)),
                  pl.BlockSpec((1, d, tf), lambda t, f, te, nv: (te[t], 0, f)),
                  pl.BlockSpec((1, d, tf), lambda t, f, te, nv: (te[t], 0, f)),
                  pl.BlockSpec((1, tf, d), lambda t, f, te, nv: (te[t], f, 0))],
        out_specs=pl.BlockSpec((tm, d), lambda t, f, te, nv: (t, 0)),
        scratch_shapes=[pltpu.VMEM((tm, d), BF16), pltpu.VMEM((tm, d), F32)],
    )
    return pl.pallas_call(
        _moe_experts_kernel,
        grid_spec=grid_spec,
        out_shape=jax.ShapeDtypeStruct((n_rows, d), F32),
        compiler_params=_cparams(("parallel", "arbitrary")),
        name="moe_experts",
    )(tile_exp, n_valid, xs, w_gate.astype(BF16), w_up.astype(BF16), w_down.astype(BF16))


def _moe_combine_kernel(alpha, d0_ref, d1_ref, d0n_ref, d1n_ref, x_ref, w_ref, g_ref, b_ref,
                        ys_hbm, o_ref, yg_s, sem):
    i = pl.program_id(0)
    tm = x_ref.shape[0]
    buf = i % 2

    def slot_row(idx_ref, k, r, b):
        return pltpu.make_async_copy(ys_hbm.at[pl.ds(idx_ref[0, 0, r], 1), :],
                                     yg_s.at[b, k, pl.ds(r, 1), :], sem.at[b])

    def each_row(fn):
        def body(r, c):
            fn(r)
            return c
        lax.fori_loop(0, tm, body, 0, unroll=8)

    def start(i0, i1, b):
        each_row(lambda r: (slot_row(i0, 0, r, b).start(), slot_row(i1, 1, r, b).start()))

    @pl.when(i == 0)
    def _():
        start(d0_ref, d1_ref, 0)

    @pl.when(i + 1 < pl.num_programs(0))
    def _():
        start(d0n_ref, d1n_ref, 1 - buf)

    each_row(lambda r: (slot_row(d0_ref, 0, r, buf).wait(), slot_row(d1_ref, 1, r, buf).wait()))
    w = w_ref[...]
    mix = yg_s[buf, 0] * w[:, 0:1] + yg_s[buf, 1] * w[:, 1:2]
    o_ref[...] = _layer_norm(alpha * x_ref[...] + mix, g_ref[...], b_ref[...])


def _moe_combine_deepnorm(x, ys, dest, wgt, ln_g, ln_b, alpha):
    m, d = x.shape
    tm = _tile(m, 512)
    n = m // tm
    dest3 = dest.reshape(2 * n, 1, tm)
    smem_rows = lambda imap: pl.BlockSpec((1, 1, tm), imap, memory_space=pltpu.SMEM)
    nxt = lambda i: jnp.minimum(i + 1, n - 1)
    return pl.pallas_call(
        functools.partial(_moe_combine_kernel, alpha),
        grid=(n,),
        in_specs=[smem_rows(lambda i: (i, 0, 0)),
                  smem_rows(lambda i: (n + i, 0, 0)),
                  smem_rows(lambda i: (nxt(i), 0, 0)),
                  smem_rows(lambda i: (n + nxt(i), 0, 0)),
                  pl.BlockSpec((tm, d), lambda i: (i, 0)),
                  pl.BlockSpec((tm, LANES), lambda i: (i, 0)),
                  pl.BlockSpec((1, d), lambda i: (0, 0)),
                  pl.BlockSpec((1, d), lambda i: (0, 0)),
                  pl.BlockSpec(memory_space=pl.ANY)],
        out_specs=pl.BlockSpec((tm, d), lambda i: (i, 0)),
        out_shape=jax.ShapeDtypeStruct((m, d), F32),
        scratch_shapes=[pltpu.VMEM((2, 2, tm, d), F32), pltpu.SemaphoreType.DMA((2,))],
        compiler_params=_cparams(("arbitrary",)),
        name="moe_combine_deepnorm",
    )(dest3, dest3, dest3, dest3, x, wgt, ln_g.reshape(1, d), ln_b.reshape(1, d), ys)


def _moe_deepnorm(x, router, w_gate, w_up, w_down, ln_g, ln_b, alpha):
    m = x.shape[0]
    n_exp = router.shape[1]
    tm = _tile(m, MOE_ROWS)
    idx, wgt = _router_top2(x, router)
    dest, tile_exp, n_valid, fill_lo, fill_hi = _moe_plan(idx, m, tm, n_exp)
    xs = _moe_permute(x, dest, fill_lo, fill_hi, 2 * m + n_exp * tm)
    ys = _moe_experts(xs, tile_exp, n_valid, tm, w_gate, w_up, w_down)
    return _moe_combine_deepnorm(x, ys, dest, wgt, ln_g, ln_b, alpha)


def _proj_pairs_kernel(transposed, x_ref, w_ref, o_ref, xb_s):
    @pl.when(pl.program_id(1) == 0)
    def _():
        xb_s[...] = x_ref[...].astype(BF16)

    n_slab = o_ref.shape[0]
    if transposed:
        res = _dot_nt(w_ref[...], xb_s[...])
        for q in range(n_slab):
            o_ref[q] = res[q * LANES:(q + 1) * LANES, :].astype(o_ref.dtype)
    else:
        res = _dot(xb_s[...], w_ref[...])
        for q in range(n_slab):
            o_ref[q] = res[:, q * LANES:(q + 1) * LANES].astype(o_ref.dtype)


def _proj_pairs(x, w, transposed, out_dtype=F32):
    m, d = x.shape
    n = w.shape[1]
    tm = _tile(m, 512)
    tn = _tile(n, 1024)
    n_slab = tn // LANES
    if transposed:
        wb = w.T.astype(BF16)
        w_spec = pl.BlockSpec((tn, d), lambda i, j: (j, 0))
        o_spec = pl.BlockSpec((n_slab, LANES, tm), lambda i, j: (j, 0, i))
        o_shape = jax.ShapeDtypeStruct((n // LANES, LANES, m), out_dtype)
    else:
        wb = w.astype(BF16)
        w_spec = pl.BlockSpec((d, tn), lambda i, j: (0, j))
        o_spec = pl.BlockSpec((n_slab, tm, LANES), lambda i, j: (j, i, 0))
        o_shape = jax.ShapeDtypeStruct((n // LANES, m, LANES), out_dtype)
    return pl.pallas_call(
        functools.partial(_proj_pairs_kernel, transposed),
        grid=(m // tm, n // tn),
        in_specs=[pl.BlockSpec((tm, d), lambda i, j: (i, 0)), w_spec],
        out_specs=o_spec,
        out_shape=o_shape,
        scratch_shapes=[pltpu.VMEM((tm, d), BF16)],
        compiler_params=_cparams(("parallel", "arbitrary")),
        name="proj_pairs_t" if transposed else "proj_pairs",
    )(x, wb)


def _block_mean_kernel(k_ref, o_ref):
    k = k_ref[0]
    nb = k.shape[0] // MOBA_BLOCK
    o_ref[0] = jnp.mean(k.reshape(nb, MOBA_BLOCK, LANES), axis=1)


def _block_means(k_pairs, batch):
    n_pair, m, _ = k_pairs.shape
    seq = m // batch
    nb = seq // MOBA_BLOCK
    return pl.pallas_call(
        _block_mean_kernel,
        grid=(n_pair, batch),
        in_specs=[pl.BlockSpec((1, seq, LANES), lambda p, b: (p, b, 0))],
        out_specs=pl.BlockSpec((1, nb, LANES), lambda p, b: (p, b, 0)),
        out_shape=jax.ShapeDtypeStruct((n_pair, batch * nb, LANES), F32),
        compiler_params=_cparams(("parallel", "parallel")),
        name="moba_block_means",
    )(k_pairs)


def _moba_kernel(scale, qt_ref, k_ref, vt_ref, km_ref, o_ref, neg_s, sca_s, scb_s, m_s, acc_s):
    own = pl.program_id(2)
    blk = MOBA_BLOCK
    n_slab = qt_ref.shape[0]
    nb = km_ref.shape[1]
    row = lax.broadcasted_iota(jnp.int32, (LANES, blk), 0)
    zero_q = jnp.zeros((LANES, blk), F32)
    n_iota = lax.broadcasted_iota(jnp.int32, (nb, blk), 0).astype(F32)
    past = n_iota < own.astype(F32)
    neg_inf = jnp.float32(-jnp.inf)

    streams = [(g, h) for g in range(n_slab) for h in range(2)]
    qh = []
    for s, (g, h) in enumerate(streams):
        in_head = (row < HEAD_DIM) if h == 0 else (row >= HEAD_DIM)
        q_h = jnp.where(in_head, qt_ref[g], zero_q)
        qh.append((q_h * (scale * LOG2E)).astype(BF16))
        gate = _dot_sp(_split(km_ref[g]), _split(q_h))
        gate = jnp.where(past, gate, neg_inf)
        neg = jnp.full((nb, blk), NEG_BIG, F32)
        for _ in range(min(MOBA_TOPK, nb)):
            mx = jnp.max(gate, axis=0, keepdims=True)
            idx = jnp.min(jnp.where(gate == mx, n_iota, float(nb)), axis=0, keepdims=True)
            pick = n_iota == idx
            neg = jnp.where(jnp.logical_and(pick, past), 0.0, neg)
            gate = jnp.where(pick, neg_inf, gate)
        neg_s[s] = neg

    ones_rows = jnp.ones((DEN_ROWS, blk), BF16)

    def block_scores(n, s):
        start = pl.multiple_of(n * blk, blk)
        return _dot(k_ref[streams[s][0], pl.ds(start, blk), :], qh[s])

    def absorb(n, st, bias, s):
        g, h = streams[s]
        start = pl.multiple_of(n * blk, blk)
        mx = m_s[s]
        mblk = jnp.max(st, axis=0, keepdims=True)
        if bias is not None:
            mblk = mblk + bias
        mx_new = jnp.maximum(mx, mblk)
        alpha = jnp.exp2(mx - mx_new)
        shift = mx_new if bias is None else mx_new - bias
        p = jnp.exp2(st - shift).astype(BF16)
        vtb = vt_ref[g, h * HEAD_DIM:(h + 1) * HEAD_DIM, pl.ds(start, blk)]
        m_s[s] = mx_new
        acc_s[s] = alpha * acc_s[s] + _dot(jnp.concatenate([vtb, ones_rows], axis=0), p)

    kpos = lax.broadcasted_iota(jnp.int32, (blk, blk), 0)
    qpos = lax.broadcasted_iota(jnp.int32, (blk, blk), 1)
    causal = kpos <= qpos
    own_scores = [block_scores(own, s) for s in range(len(streams))]
    for s in range(len(streams)):
        sca_s[s] = block_scores(0, s)
        m_s[s] = jnp.full((1, blk), neg_inf, F32)
        acc_s[s] = jnp.zeros((HEAD_DIM + DEN_ROWS, blk), F32)
    for s in range(len(streams)):
        absorb(own, jnp.where(causal, own_scores[s], neg_inf), None, s)

    def body(j, carry):
        n0 = 2 * j
        n1 = n0 + 1
        n2 = jnp.minimum(n0 + 2, nb - 1)
        for s in range(len(streams)):
            scb_s[s] = block_scores(n1, s)
            absorb(n0, sca_s[s], neg_s[s, pl.ds(n0, 1), :], s)
        for s in range(len(streams)):
            sca_s[s] = block_scores(n2, s)
            absorb(n1, scb_s[s], neg_s[s, pl.ds(n1, 1), :], s)
        return carry

    lax.fori_loop(0, (own + 1) // 2, body, 0)
    for g in range(n_slab):
        a0, a1 = acc_s[2 * g], acc_s[2 * g + 1]
        o_t = jnp.concatenate([a0[:HEAD_DIM] / a0[HEAD_DIM:HEAD_DIM + 1],
                               a1[:HEAD_DIM] / a1[HEAD_DIM:HEAD_DIM + 1]], axis=0)
        o_ref[g] = o_t.T.astype(o_ref.dtype)


def _moba_attention(q_t, k, v_t, k_means, batch):
    n_pair, _, m = q_t.shape
    seq = m // batch
    nb = seq // MOBA_BLOCK
    g = _tile(n_pair, MOBA_SLABS)
    return pl.pallas_call(
        functools.partial(_moba_kernel, HEAD_DIM ** -0.5),
        grid=(n_pair // g, batch, nb),
        in_specs=[pl.BlockSpec((g, LANES, MOBA_BLOCK), lambda p, b, i: (p, 0, b * nb + i)),
                  pl.BlockSpec((g, seq, LANES), lambda p, b, i: (p, b, 0)),
                  pl.BlockSpec((g, LANES, seq), lambda p, b, i: (p, 0, b)),
                  pl.BlockSpec((g, nb, LANES), lambda p, b, i: (p, b, 0))],
        out_specs=pl.BlockSpec((g, MOBA_BLOCK, LANES), lambda p, b, i: (p, b * nb + i, 0)),
        out_shape=jax.ShapeDtypeStruct((n_pair, m, LANES), BF16),
        scratch_shapes=[pltpu.VMEM((2 * g, nb, MOBA_BLOCK), F32),
                        pltpu.VMEM((2 * g, MOBA_BLOCK, MOBA_BLOCK), F32),
                        pltpu.VMEM((2 * g, MOBA_BLOCK, MOBA_BLOCK), F32),
                        pltpu.VMEM((2 * g, 1, MOBA_BLOCK), F32),
                        pltpu.VMEM((2 * g, HEAD_DIM + DEN_ROWS, MOBA_BLOCK), F32)],
        compiler_params=_cparams(("parallel", "parallel", "arbitrary")),
        name="moba_attention",
    )(q_t, k, v_t, k_means)


def kernel(x, rwkv_mu, rwkv_w_rkv, rwkv_w_out, rwkv_decay_w0, rwkv_decay_w1, rwkv_decay_w2, rwkv_iclr_a0, rwkv_iclr_a1, rwkv_iclr_a2, rwkv_vres_v0, rwkv_vres_v1, rwkv_vres_v2, rwkv_gate_g1, rwkv_gate_g2, rwkv_k_k, rwkv_k_a, rwkv_r_k, rwkv_gn_g, rwkv_gn_b, moba_w_k, moba_w_v, moba_w_q, moba_w_o, ffn_w_gate, ffn_w_up, ffn_w_down, moe_router, moe_w_gate, moe_w_up, moe_w_down, ln_g, ln_b):
    batch, seq, d = x.shape
    assert d % (2 * LANES) == 0 and seq % MOBA_BLOCK == 0 and seq % WKV_CHUNK == 0
    depth = ln_g.shape[0]
    n_rwkv = rwkv_mu.shape[0]
    alpha = (2.0 * depth) ** 0.25
    h = x.reshape(batch * seq, d)
    v_first = None
    kv = None
    for layer in range(depth):
        if layer < n_rwkv:
            i = layer
            vres = None if i == 0 else (rwkv_vres_v0[i - 1], rwkv_vres_v1[i - 1], rwkv_vres_v2[i - 1])
            r, e, k, v, a, b, g = _rwkv_proj(
                h, seq, rwkv_mu[i], rwkv_w_rkv[i], rwkv_decay_w1[i], rwkv_iclr_a1[i],
                rwkv_gate_g1[i], rwkv_decay_w2[i], rwkv_iclr_a2[i], rwkv_gate_g2[i],
                rwkv_decay_w0[i], rwkv_iclr_a0[i], rwkv_k_k[i], rwkv_k_a[i],
                vres=vres, v_first=v_first)
            if i == 0:
                v_first = v
            y = _wkv_scan(r, e, k, v, a, b, batch)
            mix = _rwkv_post(y, r, k, v, g, rwkv_r_k[i], rwkv_gn_g[i], rwkv_gn_b[i])
            w_out = rwkv_w_out[i]
        else:
            jdx = layer - n_rwkv
            k_pairs, v_t, k_means = kv
            q_t = _proj_pairs(h, moba_w_q[jdx], transposed=True)
            mix = _moba_attention(q_t, k_pairs, v_t, k_means, batch)
            w_out = moba_w_o[jdx]
        h = _out_proj_deepnorm(mix, w_out, h, ln_g[layer, 0], ln_b[layer, 0], alpha)
        ex = layer // 2
        if layer % 2 == 0:
            h = _ffn_deepnorm(h, ffn_w_gate[ex], ffn_w_up[ex], ffn_w_down[ex],
                              ln_g[layer, 1], ln_b[layer, 1], alpha)
        else:
            h = _moe_deepnorm(h, moe_router[ex], moe_w_gate[ex], moe_w_up[ex], moe_w_down[ex],
                              ln_g[layer, 1], ln_b[layer, 1], alpha)
        if layer == n_rwkv - 1:
            k_pairs = _proj_pairs(h, moba_w_k, transposed=False)
            v_t = _proj_pairs(h, moba_w_v, transposed=True, out_dtype=BF16)
            kv = (k_pairs.astype(BF16), v_t, _block_means(k_pairs, batch))
    return h.reshape(batch, seq, d)
```

```python
import functools
import math

import jax
import jax.numpy as jnp
from jax import lax
from jax.experimental import pallas as pl
from jax.experimental.pallas import tpu as pltpu

HEAD_DIM = 64
LANES = 128
GN_EPS = 64e-5
LN_EPS = 1e-5
MOBA_BLOCK = 256
MOBA_TOPK = 3
MOE_ROWS = 512
WKV_CHUNK = 64
WKV_SLABS = 8
NEG_BIG = -1e30
MOBA_SLABS = 8
DEN_ROWS = 16
LOG2E = 1.4426950408889634

F32 = jnp.float32
BF16 = jnp.bfloat16
VMEM_LIMIT = 56 * 1024 * 1024


def _cparams(sem):
    return pltpu.CompilerParams(dimension_semantics=sem, vmem_limit_bytes=VMEM_LIMIT)


def _dot(a, b):
    return jnp.dot(a, b, preferred_element_type=F32)


def _dot_nt(a, b):
    return lax.dot_general(a, b, (((1,), (1,)), ((), ())), preferred_element_type=F32)


def _split(x):
    hi = x.astype(BF16)
    lo = (x - hi.astype(F32)).astype(BF16)
    return hi, lo


def _dot_sp(a, b, nt=False):
    d = _dot_nt if nt else _dot
    return d(a[0], b[0]) + (d(a[0], b[1]) + d(a[1], b[0]))


def _dot_exact_rhs(a, b_exact, nt=False):
    d = _dot_nt if nt else _dot
    hi = a.astype(BF16)
    r1 = a - hi.astype(F32)
    mid = r1.astype(BF16)
    lo = (r1 - mid.astype(F32)).astype(BF16)
    return d(hi, b_exact) + (d(mid, b_exact) + d(lo, b_exact))


def _sigmoid(x):
    return 1.0 / (1.0 + jnp.exp(-x))


def _layer_norm(y, g, b):
    mu = jnp.mean(y, axis=-1, keepdims=True)
    yc = y - mu
    var = jnp.mean(yc * yc, axis=-1, keepdims=True)
    return yc * lax.rsqrt(var + LN_EPS) * g + b


def _head_blockdiag(n):
    i = jnp.arange(n) // HEAD_DIM
    return (i[:, None] == i[None, :]).astype(BF16)


def _tile(n, want):
    t = min(n, want)
    assert n % t == 0, (n, want)
    return t


def _rwkv_proj_kernel(has_vres, steps_per_seq, *refs):
    if has_vres:
        (x_ref, xp_ref, mu_ref, wrkv_ref, wd1_ref, wa1_ref, wg1_ref, wv1_ref,
         wd2_ref, wa2_ref, wg2_ref, wv2_ref, w0_ref, a0_ref, v0_ref, kk_ref, ka_ref,
         bd_ref, vf_ref,
         r_out, e_out, k_out, v_out, a_out, b_out, g_out,
         xm_s, hd_s, ha_s, hg_s, hv_s) = refs
    else:
        (x_ref, xp_ref, mu_ref, wrkv_ref, wd1_ref, wa1_ref, wg1_ref,
         wd2_ref, wa2_ref, wg2_ref, w0_ref, a0_ref, kk_ref, ka_ref,
         bd_ref,
         r_out, e_out, k_out, v_out, a_out, b_out, g_out,
         xm_s, hd_s, ha_s, hg_s) = refs
    i = pl.program_id(0)
    j = pl.program_id(1)

    @pl.when(j == 0)
    def _():
        x = x_ref[...]
        tm = x.shape[0]
        prev_row = jnp.where(i % steps_per_seq == 0, 0.0, xp_ref[7:8, :])
        rolled = pltpu.roll(x, 1, axis=0)
        row = lax.broadcasted_iota(jnp.int32, (tm, 1), 0)
        x_prev = jnp.where(row == 0, prev_row, rolled)
        xx = x_prev - x
        for c in range(3):
            xm_s[c] = (x + xx * mu_ref[c:c + 1, :]).astype(BF16)
        xw = (x + xx * mu_ref[3:4, :]).astype(BF16)
        xa = (x + xx * mu_ref[4:5, :]).astype(BF16)
        xg = (x + xx * mu_ref[5:6, :]).astype(BF16)
        hd_s[...] = jnp.tanh(_dot(xw, wd1_ref[...])).astype(BF16)
        ha_s[...] = _dot(xa, wa1_ref[...]).astype(BF16)
        hg_s[...] = _sigmoid(_dot(xg, wg1_ref[...])).astype(BF16)
        if has_vres:
            hv_s[...] = _dot(xm_s[2], wv1_ref[...]).astype(BF16)

    r = _dot(xm_s[0], wrkv_ref[0])
    k = _dot(xm_s[1], wrkv_ref[1])
    v = _dot(xm_s[2], wrkv_ref[2])
    z = w0_ref[...] + _dot(hd_s[...], wd2_ref[...])
    nz = -z
    softplus = jnp.maximum(nz, 0.0) + jnp.log(1.0 + jnp.exp(-jnp.abs(nz)))
    e = jnp.exp(-softplus - 0.5)
    a = _sigmoid(a0_ref[...] + _dot(ha_s[...], wa2_ref[...]))
    g = _dot(hg_s[...], wg2_ref[...])
    n_slab = r.shape[1] // LANES
    if has_vres:
        vf = jnp.concatenate([vf_ref[q] for q in range(n_slab)], axis=1)
        v = v + (vf - v) * _sigmoid(v0_ref[...] + _dot(hv_s[...], wv2_ref[...]))
    kk = k * kk_ref[...]
    ss = _dot_exact_rhs(kk * kk, bd_ref[...])
    kk = kk * lax.rsqrt(jnp.maximum(ss, 1e-24))
    k = k * (1.0 + (a - 1.0) * ka_ref[...])
    for q in range(n_slab):
        sl = slice(q * LANES, (q + 1) * LANES)
        r_out[q] = r[:, sl]
        e_out[q] = e[:, sl]
        k_out[q] = k[:, sl]
        v_out[q] = v[:, sl]
        a_out[q] = -kk[:, sl]
        b_out[q] = (kk * a)[:, sl]
        g_out[q] = g[:, sl]


def _rwkv_proj(x, seq_len, mu, w_rkv, wd1, wa1, wg1, wd2, wa2, wg2, w0, a0, k_k, k_a,
               vres=None, v_first=None):
    m, d = x.shape
    tm = _tile(seq_len, 512)
    tn = _tile(d, 256)
    n_slab = tn // LANES
    has_vres = vres is not None
    row = lambda a: a.reshape(1, d)
    full = lambda a: pl.BlockSpec(a.shape, lambda i, j: (0,) * a.ndim)
    colblk = lambda rows: pl.BlockSpec((rows, tn), lambda i, j: (0, j))
    bd = _head_blockdiag(tn)
    wd1, wa1, wg1 = wd1.astype(BF16), wa1.astype(BF16), wg1.astype(BF16)
    wd2, wa2, wg2 = wd2.astype(BF16), wa2.astype(BF16), wg2.astype(BF16)
    w_rkv = w_rkv.astype(BF16)
    args = [x, x, mu, w_rkv, wd1, wa1, wg1]
    specs = [pl.BlockSpec((tm, d), lambda i, j: (i, 0)),
             pl.BlockSpec((8, d), lambda i, j: (jnp.maximum(i * (tm // 8) - 1, 0), 0)),
             full(mu),
             pl.BlockSpec((3, d, tn), lambda i, j: (0, 0, j)),
             full(wd1), full(wa1), full(wg1)]
    if has_vres:
        v0, wv1, wv2 = vres
        wv1, wv2 = wv1.astype(BF16), wv2.astype(BF16)
        args += [wv1]
        specs += [full(wv1)]
    args += [wd2, wa2, wg2]
    specs += [colblk(wd2.shape[0]), colblk(wa2.shape[0]), colblk(wg2.shape[0])]
    if has_vres:
        args += [wv2]
        specs += [colblk(wv2.shape[0])]
    args += [row(w0), row(a0)]
    specs += [colblk(1), colblk(1)]
    if has_vres:
        args += [row(v0)]
        specs += [colblk(1)]
    args += [row(k_k), row(k_a), bd]
    specs += [colblk(1), colblk(1), full(bd)]
    slab_spec = pl.BlockSpec((n_slab, tm, LANES), lambda i, j: (j, i, 0))
    if has_vres:
        args += [v_first]
        specs += [slab_spec]
    slab = jax.ShapeDtypeStruct((d // LANES, m, LANES), F32)
    scratch = [pltpu.VMEM((3, tm, d), BF16),
               pltpu.VMEM((tm, wd1.shape[1]), BF16),
               pltpu.VMEM((tm, wa1.shape[1]), BF16),
               pltpu.VMEM((tm, wg1.shape[1]), BF16)]
    if has_vres:
        scratch.append(pltpu.VMEM((tm, wv1.shape[1]), BF16))
    return pl.pallas_call(
        functools.partial(_rwkv_proj_kernel, has_vres, seq_len // tm),
        grid=(m // tm, d // tn),
        in_specs=specs,
        out_specs=[slab_spec] * 7,
        out_shape=[slab] * 7,
        scratch_shapes=scratch,
        compiler_params=_cparams(("parallel", "arbitrary")),
        name="rwkv_proj",
    )(*args)


def _wkv_kernel(r_ref, e_ref, k_ref, v_ref, a_ref, b_ref, y_ref, s_ref):
    c = WKV_CHUNK
    n_chunks = r_ref.shape[1] // c

    @pl.when(pl.program_id(2) == 0)
    def _():
        s_ref[...] = jnp.zeros_like(s_ref)

    lane = lax.broadcasted_iota(jnp.int32, (c, LANES), 1)
    t_idx = lax.broadcasted_iota(jnp.int32, (c, LANES), 0)
    head0 = lane < HEAD_DIM
    s_idx = jnp.bitwise_and(lane, HEAD_DIM - 1)
    strict = s_idx < t_idx
    incl = s_idx <= t_idx
    tri = jnp.where(lax.broadcasted_iota(jnp.int32, (c, c), 1)
                    <= lax.broadcasted_iota(jnp.int32, (c, c), 0), 1.0, 0.0).astype(BF16)
    rr = lax.broadcasted_iota(jnp.int32, (LANES, LANES), 0)
    cc = lax.broadcasted_iota(jnp.int32, (LANES, LANES), 1)
    same_head = (rr < HEAD_DIM) == (cc < HEAD_DIM)

    def bd(x):
        z = jnp.zeros_like(x)
        return jnp.concatenate([jnp.where(head0, x, z), jnp.where(head0, z, x)], axis=0)

    def cat(a, b, axis=0):
        return jnp.concatenate([a, b], axis=axis)

    def chunk(ci, carry):
        stages = [one_slab(ci, p) for p in range(r_ref.shape[0])]
        while stages:
            stages = [g for g in stages if next(g, "done") != "done"]
        return carry

    def one_slab(ci, p):
        sl = pl.ds(pl.multiple_of(ci * c, c), c)
        r = r_ref[p, sl, :]
        e = e_ref[p, sl, :]
        k = k_ref[p, sl, :]
        v = v_ref[p, sl, :]
        a = a_ref[p, sl, :]
        b = b_ref[p, sl, :]
        s0 = s_ref[p]
        cum = _cumsum(tri, e)
        yield
        tot = cum[c - 1:c, :]
        ar = cat(a * jnp.exp(e - cum), r * jnp.exp(-cum)).astype(BF16)
        ec = jnp.exp(cum)
        bt = (b * ec).astype(BF16)
        kt = (k * ec).astype(BF16)
        eh = jnp.exp(cum - tot)
        bk = cat(b * eh, k * eh).astype(BF16)
        bdv = bd(v.astype(BF16))
        gram = _dot_nt(ar, cat(bd(bt), bd(kt)))
        yield
        zero = jnp.zeros((c, LANES), F32)
        l_ab = jnp.where(strict, gram[:c, :LANES], zero)
        n_ak = jnp.where(strict, gram[:c, LANES:], zero)
        m_rb = jnp.where(incl, gram[c:, :LANES], zero)
        m_rk = jnp.where(incl, gram[c:, LANES:], zero)
        xy = _dot_nt(ar, s0.astype(BF16)) + _dot(cat(n_ak, m_rk).astype(BF16), bdv)
        x = xy[:c]
        y0 = xy[c:]
        yield
        lp = l_ab.astype(BF16)
        n_steps = int(math.log2(c))
        for step in range(n_steps):
            bdx = bd(x.astype(BF16))
            if step + 1 < n_steps:
                t = _dot(lp, cat(bdx, bd(lp), axis=1))
                x = x + t[:, :LANES]
                lp = t[:, LANES:].astype(BF16)
            else:
                x = x + _dot(lp, bdx)
            yield
        y_ref[p, sl, :] = y0 + _dot(m_rb.astype(BF16), bd(x.astype(BF16)))
        uv_t = cat(x, v).T.astype(BF16)
        upd = _dot(uv_t, bk)
        s_ref[p] = s0 * jnp.exp(-tot) + jnp.where(same_head, upd, jnp.zeros_like(upd))
        yield

    lax.fori_loop(0, n_chunks, chunk, 0)


def _cumsum(tri, e):
    hi = e.astype(BF16)
    r1 = e - hi.astype(F32)
    mid = r1.astype(BF16)
    lo = (r1 - mid.astype(F32)).astype(BF16)
    return _dot(tri, hi) + (_dot(tri, mid) + _dot(tri, lo))


def _wkv_scan(r, e, k, v, a, b, batch):
    n_pair, m, _ = r.shape
    seq = m // batch
    tb = _tile(seq, 256)
    steps = seq // tb
    n_slab = _tile(n_pair, WKV_SLABS)
    spec = pl.BlockSpec((n_slab, tb, LANES), lambda p, bb, t: (p, bb * steps + t, 0))
    return pl.pallas_call(
        _wkv_kernel,
        grid=(n_pair // n_slab, batch, steps),
        in_specs=[spec] * 6,
        out_specs=spec,
        out_shape=jax.ShapeDtypeStruct((n_pair, m, LANES), F32),
        scratch_shapes=[pltpu.VMEM((n_slab, LANES, LANES), F32)],
        compiler_params=_cparams(("parallel", "parallel", "arbitrary")),
        name="wkv_scan",
    )(r, e, k, v, a, b)


def _rwkv_post_kernel(y_ref, r_ref, k_ref, v_ref, g_ref, rk_ref, gg_ref, gb_ref, bd_ref, z_ref):
    y = y_ref[0]
    bdm = bd_ref[...]
    inv_n = 1.0 / HEAD_DIM
    mean = _dot_exact_rhs(y, bdm) * inv_n
    yc = y - mean
    var = _dot_exact_rhs(yc * yc, bdm) * inv_n
    yn = yc * lax.rsqrt(var + GN_EPS) * gg_ref[0] + gb_ref[0]
    bonus = _dot_exact_rhs(r_ref[0] * k_ref[0] * rk_ref[0], bdm) * v_ref[0]
    z_ref[0] = ((yn + bonus) * g_ref[0]).astype(z_ref.dtype)


def _rwkv_post(y, r, k, v, g, r_k, gn_g, gn_b):
    n_pair, m, _ = y.shape
    tm = _tile(m, 1024)
    spec = pl.BlockSpec((1, tm, LANES), lambda p, i: (p, i, 0))
    pspec = pl.BlockSpec((1, 1, LANES), lambda p, i: (p, 0, 0))
    bd = _head_blockdiag(LANES)
    slab = lambda a: a.reshape(n_pair, 1, LANES)
    return pl.pallas_call(
        _rwkv_post_kernel,
        grid=(n_pair, m // tm),
        in_specs=[spec] * 5 + [pspec] * 3 + [pl.BlockSpec(bd.shape, lambda p, i: (0, 0))],
        out_specs=spec,
        out_shape=jax.ShapeDtypeStruct((n_pair, m, LANES), BF16),
        compiler_params=_cparams(("parallel", "parallel")),
        name="rwkv_post",
    )(y, r, k, v, g, slab(r_k), slab(gn_g), slab(gn_b), bd)


def _out_proj_kernel(alpha, z_ref, w_ref, x_ref, g_ref, b_ref, o_ref):
    n_pair = z_ref.shape[0]
    acc = _dot(z_ref[0], w_ref[0])
    for p in range(1, n_pair):
        acc = acc + _dot(z_ref[p], w_ref[p])
    o_ref[...] = _layer_norm(alpha * x_ref[...] + acc, g_ref[...], b_ref[...])


def _out_proj_deepnorm(z, w, x, ln_g, ln_b, alpha):
    n_pair, m, _ = z.shape
    d = w.shape[1]
    tm = _tile(m, 512)
    w3 = w.astype(BF16).reshape(n_pair, LANES, d)
    return pl.pallas_call(
        functools.partial(_out_proj_kernel, alpha),
        grid=(m // tm,),
        in_specs=[pl.BlockSpec((n_pair, tm, LANES), lambda i: (0, i, 0)),
                  pl.BlockSpec(w3.shape, lambda i: (0, 0, 0)),
                  pl.BlockSpec((tm, d), lambda i: (i, 0)),
                  pl.BlockSpec((1, d), lambda i: (0, 0)),
                  pl.BlockSpec((1, d), lambda i: (0, 0))],
        out_specs=pl.BlockSpec((tm, d), lambda i: (i, 0)),
        out_shape=jax.ShapeDtypeStruct((m, d), F32),
        compiler_params=_cparams(("parallel",)),
        name="out_proj_deepnorm",
    )(z, w3, x, ln_g.reshape(1, d), ln_b.reshape(1, d))


def _ff_tile(ff, want):
    for cand in range(min(ff, want) // LANES, 0, -1):
        if ff % (cand * LANES) == 0:
            return cand * LANES
    return ff


def _swiglu_partial(xb, wg, wu, wd):
    h1 = _dot(xb, wg)
    h2 = _dot(xb, wu)
    act = h1 * _sigmoid(h1) * h2
    return _dot(act.astype(BF16), wd)


def _ffn_kernel(alpha, x_ref, wg_ref, wu_ref, wd_ref, g_ref, b_ref, o_ref, xb_s, acc_s):
    f = pl.program_id(1)

    @pl.when(f == 0)
    def _():
        xb_s[...] = x_ref[...].astype(BF16)
        acc_s[...] = jnp.zeros_like(acc_s)

    acc_s[...] += _swiglu_partial(xb_s[...], wg_ref[...], wu_ref[...], wd_ref[...])

    @pl.when(f == pl.num_programs(1) - 1)
    def _():
        o_ref[...] = _layer_norm(alpha * x_ref[...] + acc_s[...], g_ref[...], b_ref[...])


def _ffn_deepnorm(x, w_gate, w_up, w_down, ln_g, ln_b, alpha):
    m, d = x.shape
    ff = w_gate.shape[1]
    tm = _tile(m, 512)
    tf = _ff_tile(ff, 1408)
    return pl.pallas_call(
        functools.partial(_ffn_kernel, alpha),
        grid=(m // tm, ff // tf),
        in_specs=[pl.BlockSpec((tm, d), lambda i, f: (i, 0)),
                  pl.BlockSpec((d, tf), lambda i, f: (0, f)),
                  pl.BlockSpec((d, tf), lambda i, f: (0, f)),
                  pl.BlockSpec((tf, d), lambda i, f: (f, 0)),
                  pl.BlockSpec((1, d), lambda i, f: (0, 0)),
                  pl.BlockSpec((1, d), lambda i, f: (0, 0))],
        out_specs=pl.BlockSpec((tm, d), lambda i, f: (i, 0)),
        out_shape=jax.ShapeDtypeStruct((m, d), F32),
        scratch_shapes=[pltpu.VMEM((tm, d), BF16), pltpu.VMEM((tm, d), F32)],
        compiler_params=_cparams(("parallel", "arbitrary")),
        name="ffn_deepnorm",
    )(x, w_gate.astype(BF16), w_up.astype(BF16), w_down.astype(BF16),
      ln_g.reshape(1, d), ln_b.reshape(1, d))


def _router_kernel(n_exp, x_ref, w_ref, idx_ref, wgt_ref):
    logits = _dot_sp(_split(x_ref[...]), _split(w_ref[...]))
    lane = lax.broadcasted_iota(jnp.int32, logits.shape, 1).astype(F32)
    neg_inf = jnp.float32(-jnp.inf)
    lg = jnp.where(lane < n_exp, logits, neg_inf)
    m1 = jnp.max(lg, axis=1, keepdims=True)
    i1 = jnp.min(jnp.where(lg == m1, lane, float(LANES)), axis=1, keepdims=True)
    lg2 = jnp.where(lane == i1, neg_inf, lg)
    m2 = jnp.max(lg2, axis=1, keepdims=True)
    i2 = jnp.min(jnp.where(lg2 == m2, lane, float(LANES)), axis=1, keepdims=True)
    e2 = jnp.exp(m2 - m1)
    den = 1.0 + e2
    idx_ref[...] = jnp.where(lane == 0.0, i1, jnp.where(lane == 1.0, i2, 0.0)).astype(jnp.int32)
    wgt_ref[...] = jnp.where(lane == 0.0, 1.0 / den, jnp.where(lane == 1.0, e2 / den, 0.0))


def _router_top2(x, router):
    m, d = x.shape
    n_exp = router.shape[1]
    tm = _tile(m, 512)
    w = jnp.pad(router, ((0, 0), (0, LANES - n_exp)))
    out_spec = pl.BlockSpec((tm, LANES), lambda i: (i, 0))
    return pl.pallas_call(
        functools.partial(_router_kernel, n_exp),
        grid=(m // tm,),
        in_specs=[pl.BlockSpec((tm, d), lambda i: (i, 0)),
                  pl.BlockSpec((d, LANES), lambda i: (0, 0))],
        out_specs=[out_spec, out_spec],
        out_shape=[jax.ShapeDtypeStruct((m, LANES), jnp.int32),
                   jax.ShapeDtypeStruct((m, LANES), F32)],
        compiler_params=_cparams(("parallel",)),
        name="moe_router",
    )(x, w)


def _moe_plan(idx, m, tm, n_exp):
    n_ent = 2 * m
    expert = jnp.concatenate([idx[:, 0], idx[:, 1]])
    onehot = (expert[:, None] == jnp.arange(n_exp, dtype=jnp.int32)[None, :]).astype(jnp.int32)
    csum = jnp.cumsum(onehot, axis=0)
    counts = csum[-1]
    rank = jnp.sum(csum * onehot, axis=1) - 1
    padded = ((counts + tm - 1) // tm) * tm
    ends = jnp.cumsum(padded)
    starts = ends - padded
    dest = jnp.sum(starts[None, :] * onehot, axis=1) + rank
    n_tiles = n_ent // tm + n_exp
    tile_start = jnp.arange(n_tiles, dtype=jnp.int32) * tm
    tile_exp = jnp.minimum(jnp.sum((tile_start[:, None] >= ends[None, :]).astype(jnp.int32), axis=1),
                           n_exp - 1)
    n_valid = jnp.clip(starts[tile_exp] + counts[tile_exp] - tile_start, 0, tm)
    fill_lo = jnp.concatenate([starts + counts, ends[-1:]])
    fill_hi = jnp.concatenate([ends, jnp.full((1,), n_tiles * tm, jnp.int32)])
    return dest, tile_exp, n_valid, fill_lo, fill_hi


def _moe_permute_kernel(lo_ref, hi_ref, dest_ref, x_ref, xs_hbm, sem):
    i = pl.program_id(0)
    te = dest_ref.shape[2]

    def entry_row(r):
        return pltpu.make_async_copy(x_ref.at[pl.ds(r, 1), :],
                                     xs_hbm.at[pl.ds(dest_ref[0, 0, r], 1), :], sem.at[0])

    def filler_row(r):
        return pltpu.make_async_copy(x_ref.at[pl.ds(0, 1), :], xs_hbm.at[pl.ds(r, 1), :], sem.at[1])

    def each(lo, hi, fn, unroll=1):
        def body(r, c):
            fn(r)
            return c
        lax.fori_loop(lo, hi, body, 0, unroll=unroll)

    each(0, te, lambda r: entry_row(r).start(), unroll=8)

    @pl.when(i == 0)
    def _():
        for e in range(lo_ref.shape[0]):
            each(lo_ref[e], hi_ref[e], lambda r: filler_row(r).start())
        for e in range(lo_ref.shape[0]):
            each(lo_ref[e], hi_ref[e], lambda r: filler_row(r).wait())

    each(0, te, lambda r: entry_row(r).wait(), unroll=8)


def _moe_permute(x, dest, fill_lo, fill_hi, n_rows):
    m, d = x.shape
    te = _tile(m, 512)
    n_x = m // te
    dest3 = dest.reshape(-1, 1, te)
    grid_spec = pltpu.PrefetchScalarGridSpec(
        num_scalar_prefetch=2,
        grid=(dest3.shape[0],),
        in_specs=[pl.BlockSpec((1, 1, te), lambda i, lo, hi: (i, 0, 0), memory_space=pltpu.SMEM),
                  pl.BlockSpec((te, d), lambda i, lo, hi: (i % n_x, 0))],
        out_specs=pl.BlockSpec(memory_space=pl.ANY),
        scratch_shapes=[pltpu.SemaphoreType.DMA((2,))],
    )
    return pl.pallas_call(
        _moe_permute_kernel,
        grid_spec=grid_spec,
        out_shape=jax.ShapeDtypeStruct((n_rows, d), F32),
        compiler_params=_cparams(("arbitrary",)),
        name="moe_permute",
    )(fill_lo, fill_hi, dest3, x)


def _moe_experts_kernel(te_ref, nv_ref, x_ref, wg_ref, wu_ref, wd_ref, y_ref, xb_s, acc_s):
    t = pl.program_id(0)
    f = pl.program_id(1)

    @pl.when(f == 0)
    def _():
        xb_s[...] = x_ref[...].astype(BF16)
        acc_s[...] = jnp.zeros_like(acc_s)

    @pl.when(nv_ref[t] > 0)
    def _():
        acc_s[...] += _swiglu_partial(xb_s[...], wg_ref[0], wu_ref[0], wd_ref[0])

    @pl.when(f == pl.num_programs(1) - 1)
    def _():
        y_ref[...] = acc_s[...]


def _moe_experts(xs, tile_exp, n_valid, tm, w_gate, w_up, w_down):
    n_rows, d = xs.shape
    ff = w_gate.shape[2]
    tf = _ff_tile(ff, 896)
    grid_spec = pltpu.PrefetchScalarGridSpec(
        num_scalar_prefetch=2,
        grid=(n_rows // tm, ff // tf),
        in_specs=[pl.BlockSpec((tm, d), lambda t, f, te, nv: (t, 0)),
                  pl.BlockSpec((1, d, tf), lambda t, f, te, nv: (te[t], 0, f)),
                  pl.BlockSpec((1, d, tf), lambda t, f, te, nv: (te[t], 0, f)),
                  pl.BlockSpec((1, tf, d), lambda t, f, te, nv: (te[t], f, 0))],
        out_specs=pl.BlockSpec((tm, d), lambda t, f, te, nv: (t, 0)),
        scratch_shapes=[pltpu.VMEM((tm, d), BF16), pltpu.VMEM((tm, d), F32)],
    )
    return pl.pallas_call(
        _moe_experts_kernel,
        grid_spec=grid_spec,
        out_shape=jax.ShapeDtypeStruct((n_rows, d), F32),
        compiler_params=_cparams(("parallel", "arbitrary")),
        name="moe_experts",
    )(tile_exp, n_valid, xs, w_gate.astype(BF16), w_up.astype(BF16), w_down.astype(BF16))


def _moe_combine_kernel(alpha, d0_ref, d1_ref, d0n_ref, d1n_ref, x_ref, w_ref, g_ref, b_ref,
                        ys_hbm, o_ref, yg_s, sem):
    i = pl.program_id(0)
    tm = x_ref.shape[0]
    buf = i % 2

    def slot_row(idx_ref, k, r, b):
        return pltpu.make_async_copy(ys_hbm.at[pl.ds(idx_ref[0, 0, r], 1), :],
                                     yg_s.at[b, k, pl.ds(r, 1), :], sem.at[b])

    def each_row(fn):
        def body(r, c):
            fn(r)
            return c
        lax.fori_loop(0, tm, body, 0, unroll=8)

    def start(i0, i1, b):
        each_row(lambda r: (slot_row(i0, 0, r, b).start(), slot_row(i1, 1, r, b).start()))

    @pl.when(i == 0)
    def _():
        start(d0_ref, d1_ref, 0)

    @pl.when(i + 1 < pl.num_programs(0))
    def _():
        start(d0n_ref, d1n_ref, 1 - buf)

    each_row(lambda r: (slot_row(d0_ref, 0, r, buf).wait(), slot_row(d1_ref, 1, r, buf).wait()))
    w = w_ref[...]
    mix = yg_s[buf, 0] * w[:, 0:1] + yg_s[buf, 1] * w[:, 1:2]
    o_ref[...] = _layer_norm(alpha * x_ref[...] + mix, g_ref[...], b_ref[...])


def _moe_combine_deepnorm(x, ys, dest, wgt, ln_g, ln_b, alpha):
    m, d = x.shape
    tm = _tile(m, 512)
    n = m // tm
    dest3 = dest.reshape(2 * n, 1, tm)
    smem_rows = lambda imap: pl.BlockSpec((1, 1, tm), imap, memory_space=pltpu.SMEM)
    nxt = lambda i: jnp.minimum(i + 1, n - 1)
    return pl.pallas_call(
        functools.partial(_moe_combine_kernel, alpha),
        grid=(n,),
        in_specs=[smem_rows(lambda i: (i, 0, 0)),
                  smem_rows(lambda i: (n + i, 0, 0)),
                  smem_rows(lambda i: (nxt(i), 0, 0)),
                  smem_rows(lambda i: (n + nxt(i), 0, 0)),
                  pl.BlockSpec((tm, d), lambda i: (i, 0)),
                  pl.BlockSpec((tm, LANES), lambda i: (i, 0)),
                  pl.BlockSpec((1, d), lambda i: (0, 0)),
                  pl.BlockSpec((1, d), lambda i: (0, 0)),
                  pl.BlockSpec(memory_space=pl.ANY)],
        out_specs=pl.BlockSpec((tm, d), lambda i: (i, 0)),
        out_shape=jax.ShapeDtypeStruct((m, d), F32),
        scratch_shapes=[pltpu.VMEM((2, 2, tm, d), F32), pltpu.SemaphoreType.DMA((2,))],
        compiler_params=_cparams(("arbitrary",)),
        name="moe_combine_deepnorm",
    )(dest3, dest3, dest3, dest3, x, wgt, ln_g.reshape(1, d), ln_b.reshape(1, d), ys)


def _moe_deepnorm(x, router, w_gate, w_up, w_down, ln_g, ln_b, alpha):
    m = x.shape[0]
    n_exp = router.shape[1]
    tm = _tile(m, MOE_ROWS)
    idx, wgt = _router_top2(x, router)
    dest, tile_exp, n_valid, fill_lo, fill_hi = _moe_plan(idx, m, tm, n_exp)
    xs = _moe_permute(x, dest, fill_lo, fill_hi, 2 * m + n_exp * tm)
    ys = _moe_experts(xs, tile_exp, n_valid, tm, w_gate, w_up, w_down)
    return _moe_combine_deepnorm(x, ys, dest, wgt, ln_g, ln_b, alpha)


def _proj_pairs_kernel(transposed, x_ref, w_ref, o_ref, xb_s):
    @pl.when(pl.program_id(1) == 0)
    def _():
        xb_s[...] = x_ref[...].astype(BF16)

    n_slab = o_ref.shape[0]
    if transposed:
        res = _dot_nt(w_ref[...], xb_s[...])
        for q in range(n_slab):
            o_ref[q] = res[q * LANES:(q + 1) * LANES, :].astype(o_ref.dtype)
    else:
        res = _dot(xb_s[...], w_ref[...])
        for q in range(n_slab):
            o_ref[q] = res[:, q * LANES:(q + 1) * LANES].astype(o_ref.dtype)


def _proj_pairs(x, w, transposed, out_dtype=F32):
    m, d = x.shape
    n = w.shape[1]
    tm = _tile(m, 512)
    tn = _tile(n, 1024)
    n_slab = tn // LANES
    if transposed:
        wb = w.T.astype(BF16)
        w_spec = pl.BlockSpec((tn, d), lambda i, j: (j, 0))
        o_spec = pl.BlockSpec((n_slab, LANES, tm), lambda i, j: (j, 0, i))
        o_shape = jax.ShapeDtypeStruct((n // LANES, LANES, m), out_dtype)
    else:
        wb = w.astype(BF16)
        w_spec = pl.BlockSpec((d, tn), lambda i, j: (0, j))
        o_spec = pl.BlockSpec((n_slab, tm, LANES), lambda i, j: (j, i, 0))
        o_shape = jax.ShapeDtypeStruct((n // LANES, m, LANES), out_dtype)
    return pl.pallas_call(
        functools.partial(_proj_pairs_kernel, transposed),
        grid=(m // tm, n // tn),
        in_specs=[pl.BlockSpec((tm, d), lambda i, j: (i, 0)), w_spec],
        out_specs=o_spec,
        out_shape=o_shape,
        scratch_shapes=[pltpu.VMEM((tm, d), BF16)],
        compiler_params=_cparams(("parallel", "arbitrary")),
        name="proj_pairs_t" if transposed else "proj_pairs",
    )(x, wb)


def _block_mean_kernel(k_ref, o_ref):
    k = k_ref[0]
    nb = k.shape[0] // MOBA_BLOCK
    o_ref[0] = jnp.mean(k.reshape(nb, MOBA_BLOCK, LANES), axis=1)


def _block_means(k_pairs, batch):
    n_pair, m, _ = k_pairs.shape
    seq = m // batch
    nb = seq // MOBA_BLOCK
    return pl.pallas_call(
        _block_mean_kernel,
        grid=(n_pair, batch),
        in_specs=[pl.BlockSpec((1, seq, LANES), lambda p, b: (p, b, 0))],
        out_specs=pl.BlockSpec((1, nb, LANES), lambda p, b: (p, b, 0)),
        out_shape=jax.ShapeDtypeStruct((n_pair, batch * nb, LANES), F32),
        compiler_params=_cparams(("parallel", "parallel")),
        name="moba_block_means",
    )(k_pairs)


def _moba_kernel(scale, qt_ref, k_ref, vt_ref, km_ref, o_ref, neg_s, sca_s, scb_s, m_s, acc_s):
    own = pl.program_id(2)
    blk = MOBA_BLOCK
    n_slab = qt_ref.shape[0]
    nb = km_ref.shape[1]
    row = lax.broadcasted_iota(jnp.int32, (LANES, blk), 0)
    zero_q = jnp.zeros((LANES, blk), F32)
    n_iota = lax.broadcasted_iota(jnp.int32, (nb, blk), 0).astype(F32)
    past = n_iota < own.astype(F32)
    neg_inf = jnp.float32(-jnp.inf)

    streams = [(g, h) for g in range(n_slab) for h in range(2)]
    qh = []
    for s, (g, h) in enumerate(streams):
        in_head = (row < HEAD_DIM) if h == 0 else (row >= HEAD_DIM)
        q_h = jnp.where(in_head, qt_ref[g], zero_q)
        qh.append((q_h * (scale * LOG2E)).astype(BF16))
        gate = _dot_sp(_split(km_ref[g]), _split(q_h))
        gate = jnp.where(past, gate, neg_inf)
        neg = jnp.full((nb, blk), NEG_BIG, F32)
        for _ in range(min(MOBA_TOPK, nb)):
            mx = jnp.max(gate, axis=0, keepdims=True)
            idx = jnp.min(jnp.where(gate == mx, n_iota, float(nb)), axis=0, keepdims=True)
            pick = n_iota == idx
            neg = jnp.where(jnp.logical_and(pick, past), 0.0, neg)
            gate = jnp.where(pick, neg_inf, gate)
        neg_s[s] = neg

    ones_rows = jnp.ones((DEN_ROWS, blk), BF16)

    def block_scores(n, s):
        start = pl.multiple_of(n * blk, blk)
        return _dot(k_ref[streams[s][0], pl.ds(start, blk), :], qh[s])

    def absorb(n, st, bias, s):
        g, h = streams[s]
        start = pl.multiple_of(n * blk, blk)
        mx = m_s[s]
        mblk = jnp.max(st, axis=0, keepdims=True)
        if bias is not None:
            mblk = mblk + bias
        mx_new = jnp.maximum(mx, mblk)
        alpha = jnp.exp2(mx - mx_new)
        shift = mx_new if bias is None else mx_new - bias
        p = jnp.exp2(st - shift).astype(BF16)
        vtb = vt_ref[g, h * HEAD_DIM:(h + 1) * HEAD_DIM, pl.ds(start, blk)]
        m_s[s] = mx_new
        acc_s[s] = alpha * acc_s[s] + _dot(jnp.concatenate([vtb, ones_rows], axis=0), p)

    kpos = lax.broadcasted_iota(jnp.int32, (blk, blk), 0)
    qpos = lax.broadcasted_iota(jnp.int32, (blk, blk), 1)
    causal = kpos <= qpos
    own_scores = [block_scores(own, s) for s in range(len(streams))]
    for s in range(len(streams)):
        sca_s[s] = block_scores(0, s)
        m_s[s] = jnp.full((1, blk), neg_inf, F32)
        acc_s[s] = jnp.zeros((HEAD_DIM + DEN_ROWS, blk), F32)
    for s in range(len(streams)):
        absorb(own, jnp.where(causal, own_scores[s], neg_inf), None, s)

    def body(j, carry):
        n0 = 2 * j
        n1 = n0 + 1
        n2 = jnp.minimum(n0 + 2, nb - 1)
        for s in range(len(streams)):
            scb_s[s] = block_scores(n1, s)
            absorb(n0, sca_s[s], neg_s[s, pl.ds(n0, 1), :], s)
        for s in range(len(streams)):
            sca_s[s] = block_scores(n2, s)
            absorb(n1, scb_s[s], neg_s[s, pl.ds(n1, 1), :], s)
        return carry

    lax.fori_loop(0, (own + 1) // 2, body, 0)
    for g in range(n_slab):
        a0, a1 = acc_s[2 * g], acc_s[2 * g + 1]
        o_t = jnp.concatenate([a0[:HEAD_DIM] / a0[HEAD_DIM:HEAD_DIM + 1],
                               a1[:HEAD_DIM] / a1[HEAD_DIM:HEAD_DIM + 1]], axis=0)
        o_ref[g] = o_t.T.astype(o_ref.dtype)


def _moba_attention(q_t, k, v_t, k_means, batch):
    n_pair, _, m = q_t.shape
    seq = m // batch
    nb = seq // MOBA_BLOCK
    g = _tile(n_pair, MOBA_SLABS)
    return pl.pallas_call(
        functools.partial(_moba_kernel, HEAD_DIM ** -0.5),
        grid=(n_pair // g, batch, nb),
        in_specs=[pl.BlockSpec((g, LANES, MOBA_BLOCK), lambda p, b, i: (p, 0, b * nb + i)),
                  pl.BlockSpec((g, seq, LANES), lambda p, b, i: (p, b, 0),
                               pipeline_mode=pl.Buffered(1)),
                  pl.BlockSpec((g, LANES, seq), lambda p, b, i: (p, 0, b),
                               pipeline_mode=pl.Buffered(1)),
                  pl.BlockSpec((g, nb, LANES), lambda p, b, i: (p, b, 0))],
        out_specs=pl.BlockSpec((g, MOBA_BLOCK, LANES), lambda p, b, i: (p, b * nb + i, 0)),
        out_shape=jax.ShapeDtypeStruct((n_pair, m, LANES), BF16),
        scratch_shapes=[pltpu.VMEM((2 * g, nb, MOBA_BLOCK), F32),
                        pltpu.VMEM((2 * g, MOBA_BLOCK, MOBA_BLOCK), F32),
                        pltpu.VMEM((2 * g, MOBA_BLOCK, MOBA_BLOCK), F32),
                        pltpu.VMEM((2 * g, 1, MOBA_BLOCK), F32),
                        pltpu.VMEM((2 * g, HEAD_DIM + DEN_ROWS, MOBA_BLOCK), F32)],
        compiler_params=_cparams(("parallel", "parallel", "arbitrary")),
        name="moba_attention",
    )(q_t, k, v_t, k_means)


def kernel(x, rwkv_mu, rwkv_w_rkv, rwkv_w_out, rwkv_decay_w0, rwkv_decay_w1, rwkv_decay_w2, rwkv_iclr_a0, rwkv_iclr_a1, rwkv_iclr_a2, rwkv_vres_v0, rwkv_vres_v1, rwkv_vres_v2, rwkv_gate_g1, rwkv_gate_g2, rwkv_k_k, rwkv_k_a, rwkv_r_k, rwkv_gn_g, rwkv_gn_b, moba_w_k, moba_w_v, moba_w_q, moba_w_o, ffn_w_gate, ffn_w_up, ffn_w_down, moe_router, moe_w_gate, moe_w_up, moe_w_down, ln_g, ln_b):
    batch, seq, d = x.shape
    assert d % (2 * LANES) == 0 and seq % MOBA_BLOCK == 0 and seq % WKV_CHUNK == 0
    depth = ln_g.shape[0]
    n_rwkv = rwkv_mu.shape[0]
    alpha = (2.0 * depth) ** 0.25
    h = x.reshape(batch * seq, d)
    v_first = None
    kv = None
    for layer in range(depth):
        if layer < n_rwkv:
            i = layer
            vres = None if i == 0 else (rwkv_vres_v0[i - 1], rwkv_vres_v1[i - 1], rwkv_vres_v2[i - 1])
            r, e, k, v, a, b, g = _rwkv_proj(
                h, seq, rwkv_mu[i], rwkv_w_rkv[i], rwkv_decay_w1[i], rwkv_iclr_a1[i],
                rwkv_gate_g1[i], rwkv_decay_w2[i], rwkv_iclr_a2[i], rwkv_gate_g2[i],
                rwkv_decay_w0[i], rwkv_iclr_a0[i], rwkv_k_k[i], rwkv_k_a[i],
                vres=vres, v_first=v_first)
            if i == 0:
                v_first = v
            y = _wkv_scan(r, e, k, v, a, b, batch)
            mix = _rwkv_post(y, r, k, v, g, rwkv_r_k[i], rwkv_gn_g[i], rwkv_gn_b[i])
            w_out = rwkv_w_out[i]
        else:
            jdx = layer - n_rwkv
            k_pairs, v_t, k_means = kv
            q_t = _proj_pairs(h, moba_w_q[jdx], transposed=True)
            mix = _moba_attention(q_t, k_pairs, v_t, k_means, batch)
            w_out = moba_w_o[jdx]
        h = _out_proj_deepnorm(mix, w_out, h, ln_g[layer, 0], ln_b[layer, 0], alpha)
        ex = layer // 2
        if layer % 2 == 0:
            h = _ffn_deepnorm(h, ffn_w_gate[ex], ffn_w_up[ex], ffn_w_down[ex],
                              ln_g[layer, 1], ln_b[layer, 1], alpha)
        else:
            h = _moe_deepnorm(h, moe_router[ex], moe_w_gate[ex], moe_w_up[ex], moe_w_down[ex],
                              ln_g[layer, 1], ln_b[layer, 1], alpha)
        if layer == n_rwkv - 1:
            k_pairs = _proj_pairs(h, moba_w_k, transposed=False)
            v_t = _proj_pairs(h, moba_w_v, transposed=True, out_dtype=BF16)
            kv = (k_pairs.astype(BF16), v_t, _block_means(k_pairs, batch))
    return h.reshape(batch, seq, d)
```

```python
import functools
import math

import jax
import jax.numpy as jnp
from jax import lax
from jax.experimental import pallas as pl
from jax.experimental.pallas import tpu as pltpu

HEAD_DIM = 64
LANES = 128
GN_EPS = 64e-5
LN_EPS = 1e-5
MOBA_BLOCK = 256
MOBA_TOPK = 3
MOE_ROWS = 512
WKV_CHUNK = 64
WKV_SLABS = 8
NEG_BIG = -1e30
MOBA_SLABS = 8
DEN_ROWS = 16
LOG2E = 1.4426950408889634

F32 = jnp.float32
BF16 = jnp.bfloat16
VMEM_LIMIT = 56 * 1024 * 1024


def _cparams(sem):
    return pltpu.CompilerParams(dimension_semantics=sem, vmem_limit_bytes=VMEM_LIMIT)


def _dot(a, b):
    return jnp.dot(a, b, preferred_element_type=F32)


def _dot_nt(a, b):
    return lax.dot_general(a, b, (((1,), (1,)), ((), ())), preferred_element_type=F32)


def _split(x):
    hi = x.astype(BF16)
    lo = (x - hi.astype(F32)).astype(BF16)
    return hi, lo


def _dot_sp(a, b, nt=False):
    d = _dot_nt if nt else _dot
    return d(a[0], b[0]) + (d(a[0], b[1]) + d(a[1], b[0]))


def _dot_exact_rhs(a, b_exact, nt=False):
    d = _dot_nt if nt else _dot
    hi = a.astype(BF16)
    r1 = a - hi.astype(F32)
    mid = r1.astype(BF16)
    lo = (r1 - mid.astype(F32)).astype(BF16)
    return d(hi, b_exact) + (d(mid, b_exact) + d(lo, b_exact))


def _sigmoid(x):
    return 1.0 / (1.0 + jnp.exp(-x))


def _layer_norm(y, g, b):
    mu = jnp.mean(y, axis=-1, keepdims=True)
    yc = y - mu
    var = jnp.mean(yc * yc, axis=-1, keepdims=True)
    return yc * lax.rsqrt(var + LN_EPS) * g + b


def _head_blockdiag(n):
    i = jnp.arange(n) // HEAD_DIM
    return (i[:, None] == i[None, :]).astype(BF16)


def _tile(n, want):
    t = min(n, want)
    assert n % t == 0, (n, want)
    return t


def _rwkv_proj_kernel(has_vres, steps_per_seq, *refs):
    if has_vres:
        (x_ref, xp_ref, mu_ref, wrkv_ref, wd1_ref, wa1_ref, wg1_ref, wv1_ref,
         wd2_ref, wa2_ref, wg2_ref, wv2_ref, w0_ref, a0_ref, v0_ref, kk_ref, ka_ref,
         bd_ref, vf_ref,
         r_out, e_out, k_out, v_out, a_out, b_out, g_out,
         xm_s, hd_s, ha_s, hg_s, hv_s) = refs
    else:
        (x_ref, xp_ref, mu_ref, wrkv_ref, wd1_ref, wa1_ref, wg1_ref,
         wd2_ref, wa2_ref, wg2_ref, w0_ref, a0_ref, kk_ref, ka_ref,
         bd_ref,
         r_out, e_out, k_out, v_out, a_out, b_out, g_out,
         xm_s, hd_s, ha_s, hg_s) = refs
    i = pl.program_id(0)
    j = pl.program_id(1)

    @pl.when(j == 0)
    def _():
        x = x_ref[...]
        tm = x.shape[0]
        prev_row = jnp.where(i % steps_per_seq == 0, 0.0, xp_ref[7:8, :])
        rolled = pltpu.roll(x, 1, axis=0)
        row = lax.broadcasted_iota(jnp.int32, (tm, 1), 0)
        x_prev = jnp.where(row == 0, prev_row, rolled)
        xx = x_prev - x
        for c in range(3):
            xm_s[c] = (x + xx * mu_ref[c:c + 1, :]).astype(BF16)
        xw = (x + xx * mu_ref[3:4, :]).astype(BF16)
        xa = (x + xx * mu_ref[4:5, :]).astype(BF16)
        xg = (x + xx * mu_ref[5:6, :]).astype(BF16)
        hd_s[...] = jnp.tanh(_dot(xw, wd1_ref[...])).astype(BF16)
        ha_s[...] = _dot(xa, wa1_ref[...]).astype(BF16)
        hg_s[...] = _sigmoid(_dot(xg, wg1_ref[...])).astype(BF16)
        if has_vres:
            hv_s[...] = _dot(xm_s[2], wv1_ref[...]).astype(BF16)

    r = _dot(xm_s[0], wrkv_ref[0])
    k = _dot(xm_s[1], wrkv_ref[1])
    v = _dot(xm_s[2], wrkv_ref[2])
    z = w0_ref[...] + _dot(hd_s[...], wd2_ref[...])
    nz = -z
    softplus = jnp.maximum(nz, 0.0) + jnp.log(1.0 + jnp.exp(-jnp.abs(nz)))
    e = jnp.exp(-softplus - 0.5)
    a = _sigmoid(a0_ref[...] + _dot(ha_s[...], wa2_ref[...]))
    g = _dot(hg_s[...], wg2_ref[...])
    n_slab = r.shape[1] // LANES
    if has_vres:
        vf = jnp.concatenate([vf_ref[q] for q in range(n_slab)], axis=1)
        v = v + (vf - v) * _sigmoid(v0_ref[...] + _dot(hv_s[...], wv2_ref[...]))
    kk = k * kk_ref[...]
    ss = _dot_exact_rhs(kk * kk, bd_ref[...])
    kk = kk * lax.rsqrt(jnp.maximum(ss, 1e-24))
    k = k * (1.0 + (a - 1.0) * ka_ref[...])
    for q in range(n_slab):
        sl = slice(q * LANES, (q + 1) * LANES)
        r_out[q] = r[:, sl].astype(r_out.dtype)
        e_out[q] = e[:, sl]
        k_out[q] = k[:, sl].astype(k_out.dtype)
        v_out[q] = v[:, sl].astype(v_out.dtype)
        a_out[q] = (-kk)[:, sl].astype(a_out.dtype)
        b_out[q] = (kk * a)[:, sl].astype(b_out.dtype)
        g_out[q] = g[:, sl].astype(g_out.dtype)


def _rwkv_proj(x, seq_len, mu, w_rkv, wd1, wa1, wg1, wd2, wa2, wg2, w0, a0, k_k, k_a,
               vres=None, v_first=None):
    m, d = x.shape
    tm = _tile(seq_len, 512)
    tn = _tile(d, 256)
    n_slab = tn // LANES
    has_vres = vres is not None
    row = lambda a: a.reshape(1, d)
    full = lambda a: pl.BlockSpec(a.shape, lambda i, j: (0,) * a.ndim)
    colblk = lambda rows: pl.BlockSpec((rows, tn), lambda i, j: (0, j))
    bd = _head_blockdiag(tn)
    wd1, wa1, wg1 = wd1.astype(BF16), wa1.astype(BF16), wg1.astype(BF16)
    wd2, wa2, wg2 = wd2.astype(BF16), wa2.astype(BF16), wg2.astype(BF16)
    w_rkv = w_rkv.astype(BF16)
    args = [x, x, mu, w_rkv, wd1, wa1, wg1]
    specs = [pl.BlockSpec((tm, d), lambda i, j: (i, 0)),
             pl.BlockSpec((8, d), lambda i, j: (jnp.maximum(i * (tm // 8) - 1, 0), 0)),
             full(mu),
             pl.BlockSpec((3, d, tn), lambda i, j: (0, 0, j)),
             full(wd1), full(wa1), full(wg1)]
    if has_vres:
        v0, wv1, wv2 = vres
        wv1, wv2 = wv1.astype(BF16), wv2.astype(BF16)
        args += [wv1]
        specs += [full(wv1)]
    args += [wd2, wa2, wg2]
    specs += [colblk(wd2.shape[0]), colblk(wa2.shape[0]), colblk(wg2.shape[0])]
    if has_vres:
        args += [wv2]
        specs += [colblk(wv2.shape[0])]
    args += [row(w0), row(a0)]
    specs += [colblk(1), colblk(1)]
    if has_vres:
        args += [row(v0)]
        specs += [colblk(1)]
    args += [row(k_k), row(k_a), bd]
    specs += [colblk(1), colblk(1), full(bd)]
    slab_spec = pl.BlockSpec((n_slab, tm, LANES), lambda i, j: (j, i, 0))
    if has_vres:
        args += [v_first]
        specs += [slab_spec]
    slab = lambda dt: jax.ShapeDtypeStruct((d // LANES, m, LANES), dt)
    scratch = [pltpu.VMEM((3, tm, d), BF16),
               pltpu.VMEM((tm, wd1.shape[1]), BF16),
               pltpu.VMEM((tm, wa1.shape[1]), BF16),
               pltpu.VMEM((tm, wg1.shape[1]), BF16)]
    if has_vres:
        scratch.append(pltpu.VMEM((tm, wv1.shape[1]), BF16))
    return pl.pallas_call(
        functools.partial(_rwkv_proj_kernel, has_vres, seq_len // tm),
        grid=(m // tm, d // tn),
        in_specs=specs,
        out_specs=[slab_spec] * 7,
        out_shape=[slab(BF16), slab(F32)] + [slab(BF16)] * 5,
        scratch_shapes=scratch,
        compiler_params=_cparams(("parallel", "arbitrary")),
        name="rwkv_proj",
    )(*args)


def _wkv_kernel(r_ref, e_ref, k_ref, v_ref, a_ref, b_ref, y_ref, s_ref):
    c = WKV_CHUNK
    n_slab, n_batch, tb, _ = r_ref.shape
    n_chunks = tb // c

    @pl.when(pl.program_id(1) == 0)
    def _():
        s_ref[...] = jnp.zeros_like(s_ref)

    lane = lax.broadcasted_iota(jnp.int32, (c, LANES), 1)
    t_idx = lax.broadcasted_iota(jnp.int32, (c, LANES), 0)
    head0 = lane < HEAD_DIM
    s_idx = jnp.bitwise_and(lane, HEAD_DIM - 1)
    strict = s_idx < t_idx
    incl = s_idx <= t_idx
    rr = lax.broadcasted_iota(jnp.int32, (LANES, LANES), 0)
    cc = lax.broadcasted_iota(jnp.int32, (LANES, LANES), 1)
    same_head = (rr < HEAD_DIM) == (cc < HEAD_DIM)

    def bd(x):
        z = jnp.zeros_like(x)
        return jnp.concatenate([jnp.where(head0, x, z), jnp.where(head0, z, x)], axis=0)

    def cat(a, b, axis=0):
        return jnp.concatenate([a, b], axis=axis)

    def chunk(ci, carry):
        stages = [one_slab(ci, p, bb) for p in range(n_slab) for bb in range(n_batch)]
        while stages:
            stages = [g for g in stages if next(g, "done") != "done"]
        return carry

    def one_slab(ci, p, bb):
        sl = pl.ds(pl.multiple_of(ci * c, c), c)
        r = r_ref[p, bb, sl, :]
        e = e_ref[p, bb, sl, :]
        k = k_ref[p, bb, sl, :]
        v = v_ref[p, bb, sl, :]
        a = a_ref[p, bb, sl, :]
        b = b_ref[p, bb, sl, :]
        s0 = s_ref[p * n_batch + bb]
        cum = _cumsum_rows(e, t_idx)
        yield
        tot = cum[c - 1:c, :]
        ar = cat(a * jnp.exp(e - cum), r * jnp.exp(-cum)).astype(BF16)
        ec = jnp.exp(cum)
        bt = (b * ec).astype(BF16)
        kt = (k * ec).astype(BF16)
        eh = jnp.exp(cum - tot)
        bk = cat(b * eh, k * eh).astype(BF16)
        bdv = bd(v.astype(BF16))
        gram = _dot_nt(ar, cat(bd(bt), bd(kt)))
        yield
        zero = jnp.zeros((c, LANES), F32)
        l_ab = jnp.where(strict, gram[:c, :LANES], zero)
        n_ak = jnp.where(strict, gram[:c, LANES:], zero)
        m_rb = jnp.where(incl, gram[c:, :LANES], zero)
        m_rk = jnp.where(incl, gram[c:, LANES:], zero)
        xy = _dot_nt(ar, s0.astype(BF16)) + _dot(cat(n_ak, m_rk).astype(BF16), bdv)
        x = xy[:c]
        y0 = xy[c:]
        yield
        lp = l_ab.astype(BF16)
        n_steps = int(math.log2(c))
        for step in range(n_steps):
            bdx = bd(x.astype(BF16))
            if step + 1 < n_steps:
                t = _dot(lp, cat(bdx, bd(lp), axis=1))
                x = x + t[:, :LANES]
                lp = t[:, LANES:].astype(BF16)
            else:
                x = x + _dot(lp, bdx)
            yield
        y_ref[p, bb, sl, :] = y0 + _dot(m_rb.astype(BF16), bd(x.astype(BF16)))
        uv_t = cat(x, v.astype(F32)).T.astype(BF16)
        upd = _dot(uv_t, bk)
        s_ref[p * n_batch + bb] = (s0 * jnp.exp(-tot)
                                   + jnp.where(same_head, upd, jnp.zeros_like(upd)))
        yield

    lax.fori_loop(0, n_chunks, chunk, 0)


def _cumsum_rows(e, row_idx):
    cum = e
    shift = 1
    while shift < e.shape[0]:
        cum = cum + jnp.where(row_idx >= shift, pltpu.roll(cum, shift, axis=0), 0.0)
        shift *= 2
    return cum


def _wkv_scan(r, e, k, v, a, b, batch):
    n_pair, m, _ = r.shape
    seq = m // batch
    tb = _tile(seq, 256)
    n_slab = _tile(n_pair, WKV_SLABS)
    view = lambda t: t.reshape(n_pair, batch, seq, LANES)
    spec = pl.BlockSpec((n_slab, batch, tb, LANES), lambda p, t: (p, 0, t, 0))
    y = pl.pallas_call(
        _wkv_kernel,
        grid=(n_pair // n_slab, seq // tb),
        in_specs=[spec] * 6,
        out_specs=spec,
        out_shape=jax.ShapeDtypeStruct((n_pair, batch, seq, LANES), F32),
        scratch_shapes=[pltpu.VMEM((n_slab * batch, LANES, LANES), F32)],
        compiler_params=_cparams(("parallel", "arbitrary")),
        name="wkv_scan",
    )(view(r), view(e), view(k), view(v), view(a), view(b))
    return y.reshape(n_pair, m, LANES)


def _rwkv_post_kernel(y_ref, r_ref, k_ref, v_ref, g_ref, rk_ref, gg_ref, gb_ref, bd_ref, z_ref):
    y = y_ref[0]
    bdm = bd_ref[...]
    inv_n = 1.0 / HEAD_DIM
    mean = _dot_exact_rhs(y, bdm) * inv_n
    yc = y - mean
    var = _dot_exact_rhs(yc * yc, bdm) * inv_n
    yn = yc * lax.rsqrt(var + GN_EPS) * gg_ref[0] + gb_ref[0]
    rk = r_ref[0].astype(F32) * k_ref[0].astype(F32) * rk_ref[0]
    bonus = _dot_exact_rhs(rk, bdm) * v_ref[0].astype(F32)
    z_ref[0] = ((yn + bonus) * g_ref[0].astype(F32)).astype(z_ref.dtype)


def _rwkv_post(y, r, k, v, g, r_k, gn_g, gn_b):
    n_pair, m, _ = y.shape
    tm = _tile(m, 2048)
    spec = pl.BlockSpec((1, tm, LANES), lambda p, i: (p, i, 0))
    pspec = pl.BlockSpec((1, 1, LANES), lambda p, i: (p, 0, 0))
    bd = _head_blockdiag(LANES)
    slab = lambda a: a.reshape(n_pair, 1, LANES)
    return pl.pallas_call(
        _rwkv_post_kernel,
        grid=(n_pair, m // tm),
        in_specs=[spec] * 5 + [pspec] * 3 + [pl.BlockSpec(bd.shape, lambda p, i: (0, 0))],
        out_specs=spec,
        out_shape=jax.ShapeDtypeStruct((n_pair, m, LANES), BF16),
        compiler_params=_cparams(("parallel", "parallel")),
        name="rwkv_post",
    )(y, r, k, v, g, slab(r_k), slab(gn_g), slab(gn_b), bd)


def _out_proj_kernel(alpha, z_ref, w_ref, x_ref, g_ref, b_ref, o_ref):
    n_pair = z_ref.shape[0]
    acc = _dot(z_ref[0], w_ref[0])
    for p in range(1, n_pair):
        acc = acc + _dot(z_ref[p], w_ref[p])
    o_ref[...] = _layer_norm(alpha * x_ref[...] + acc, g_ref[...], b_ref[...])


def _out_proj_deepnorm(z, w, x, ln_g, ln_b, alpha):
    n_pair, m, _ = z.shape
    d = w.shape[1]
    tm = _tile(m, 512)
    w3 = w.astype(BF16).reshape(n_pair, LANES, d)
    return pl.pallas_call(
        functools.partial(_out_proj_kernel, alpha),
        grid=(m // tm,),
        in_specs=[pl.BlockSpec((n_pair, tm, LANES), lambda i: (0, i, 0)),
                  pl.BlockSpec(w3.shape, lambda i: (0, 0, 0)),
                  pl.BlockSpec((tm, d), lambda i: (i, 0)),
                  pl.BlockSpec((1, d), lambda i: (0, 0)),
                  pl.BlockSpec((1, d), lambda i: (0, 0))],
        out_specs=pl.BlockSpec((tm, d), lambda i: (i, 0)),
        out_shape=jax.ShapeDtypeStruct((m, d), F32),
        compiler_params=_cparams(("parallel",)),
        name="out_proj_deepnorm",
    )(z, w3, x, ln_g.reshape(1, d), ln_b.reshape(1, d))


def _ff_tile(ff, want):
    for cand in range(min(ff, want) // LANES, 0, -1):
        if ff % (cand * LANES) == 0:
            return cand * LANES
    return ff


def _swiglu_partial(xb, wg, wu, wd):
    h1 = _dot(xb, wg)
    h2 = _dot(xb, wu)
    act = h1 * _sigmoid(h1) * h2
    return _dot(act.astype(BF16), wd)


def _ffn_kernel(alpha, x_ref, wg_ref, wu_ref, wd_ref, g_ref, b_ref, o_ref, xb_s, acc_s):
    f = pl.program_id(1)

    @pl.when(f == 0)
    def _():
        xb_s[...] = x_ref[...].astype(BF16)
        acc_s[...] = jnp.zeros_like(acc_s)

    acc_s[...] += _swiglu_partial(xb_s[...], wg_ref[...], wu_ref[...], wd_ref[...])

    @pl.when(f == pl.num_programs(1) - 1)
    def _():
        o_ref[...] = _layer_norm(alpha * x_ref[...] + acc_s[...], g_ref[...], b_ref[...])


def _ffn_deepnorm(x, w_gate, w_up, w_down, ln_g, ln_b, alpha):
    m, d = x.shape
    ff = w_gate.shape[1]
    tm = _tile(m, 512)
    tf = _ff_tile(ff, 1408)
    return pl.pallas_call(
        functools.partial(_ffn_kernel, alpha),
        grid=(m // tm, ff // tf),
        in_specs=[pl.BlockSpec((tm, d), lambda i, f: (i, 0)),
                  pl.BlockSpec((d, tf), lambda i, f: (0, f)),
                  pl.BlockSpec((d, tf), lambda i, f: (0, f)),
                  pl.BlockSpec((tf, d), lambda i, f: (f, 0)),
                  pl.BlockSpec((1, d), lambda i, f: (0, 0)),
                  pl.BlockSpec((1, d), lambda i, f: (0, 0))],
        out_specs=pl.BlockSpec((tm, d), lambda i, f: (i, 0)),
        out_shape=jax.ShapeDtypeStruct((m, d), F32),
        scratch_shapes=[pltpu.VMEM((tm, d), BF16), pltpu.VMEM((tm, d), F32)],
        compiler_params=_cparams(("parallel", "arbitrary")),
        name="ffn_deepnorm",
    )(x, w_gate.astype(BF16), w_up.astype(BF16), w_down.astype(BF16),
      ln_g.reshape(1, d), ln_b.reshape(1, d))


def _router_kernel(n_exp, x_ref, w_ref, idx_ref, wgt_ref):
    logits = _dot_sp(_split(x_ref[...]), _split(w_ref[...]))
    lane = lax.broadcasted_iota(jnp.int32, logits.shape, 1).astype(F32)
    neg_inf = jnp.float32(-jnp.inf)
    lg = jnp.where(lane < n_exp, logits, neg_inf)
    m1 = jnp.max(lg, axis=1, keepdims=True)
    i1 = jnp.min(jnp.where(lg == m1, lane, float(LANES)), axis=1, keepdims=True)
    lg2 = jnp.where(lane == i1, neg_inf, lg)
    m2 = jnp.max(lg2, axis=1, keepdims=True)
    i2 = jnp.min(jnp.where(lg2 == m2, lane, float(LANES)), axis=1, keepdims=True)
    e2 = jnp.exp(m2 - m1)
    den = 1.0 + e2
    idx_ref[...] = jnp.where(lane == 0.0, i1, jnp.where(lane == 1.0, i2, 0.0)).astype(jnp.int32)
    wgt_ref[...] = jnp.where(lane == 0.0, 1.0 / den, jnp.where(lane == 1.0, e2 / den, 0.0))


def _router_top2(x, router):
    m, d = x.shape
    n_exp = router.shape[1]
    tm = _tile(m, 512)
    w = jnp.pad(router, ((0, 0), (0, LANES - n_exp)))
    out_spec = pl.BlockSpec((tm, LANES), lambda i: (i, 0))
    return pl.pallas_call(
        functools.partial(_router_kernel, n_exp),
        grid=(m // tm,),
        in_specs=[pl.BlockSpec((tm, d), lambda i: (i, 0)),
                  pl.BlockSpec((d, LANES), lambda i: (0, 0))],
        out_specs=[out_spec, out_spec],
        out_shape=[jax.ShapeDtypeStruct((m, LANES), jnp.int32),
                   jax.ShapeDtypeStruct((m, LANES), F32)],
        compiler_params=_cparams(("parallel",)),
        name="moe_router",
    )(x, w)


def _moe_plan(idx, m, tm, n_exp):
    n_ent = 2 * m
    expert = jnp.concatenate([idx[:, 0], idx[:, 1]])
    onehot = (expert[:, None] == jnp.arange(n_exp, dtype=jnp.int32)[None, :]).astype(jnp.int32)
    csum = jnp.cumsum(onehot, axis=0)
    counts = csum[-1]
    rank = jnp.sum(csum * onehot, axis=1) - 1
    padded = ((counts + tm - 1) // tm) * tm
    ends = jnp.cumsum(padded)
    starts = ends - padded
    dest = jnp.sum(starts[None, :] * onehot, axis=1) + rank
    n_tiles = n_ent // tm + n_exp
    tile_start = jnp.arange(n_tiles, dtype=jnp.int32) * tm
    tile_exp = jnp.minimum(jnp.sum((tile_start[:, None] >= ends[None, :]).astype(jnp.int32), axis=1),
                           n_exp - 1)
    n_valid = jnp.clip(starts[tile_exp] + counts[tile_exp] - tile_start, 0, tm)
    fill_lo = jnp.concatenate([starts + counts, ends[-1:]])
    fill_hi = jnp.concatenate([ends, jnp.full((1,), n_tiles * tm, jnp.int32)])
    return dest, tile_exp, n_valid, fill_lo, fill_hi


def _moe_permute_kernel(lo_ref, hi_ref, dest_ref, x_ref, xs_hbm, sem):
    i = pl.program_id(0)
    te = dest_ref.shape[2]

    def entry_row(r):
        return pltpu.make_async_copy(x_ref.at[pl.ds(r, 1), :],
                                     xs_hbm.at[pl.ds(dest_ref[0, 0, r], 1), :], sem.at[0])

    def filler_row(r):
        return pltpu.make_async_copy(x_ref.at[pl.ds(0, 1), :], xs_hbm.at[pl.ds(r, 1), :], sem.at[1])

    def each(lo, hi, fn, unroll=1):
        def body(r, c):
            fn(r)
            return c
        lax.fori_loop(lo, hi, body, 0, unroll=unroll)

    each(0, te, lambda r: entry_row(r).start(), unroll=8)

    @pl.when(i == 0)
    def _():
        for e in range(lo_ref.shape[0]):
            each(lo_ref[e], hi_ref[e], lambda r: filler_row(r).start())
        for e in range(lo_ref.shape[0]):
            each(lo_ref[e], hi_ref[e], lambda r: filler_row(r).wait())

    each(0, te, lambda r: entry_row(r).wait(), unroll=8)


def _moe_permute(x, dest, fill_lo, fill_hi, n_rows):
    m, d = x.shape
    te = _tile(m, 512)
    n_x = m // te
    dest3 = dest.reshape(-1, 1, te)
    grid_spec = pltpu.PrefetchScalarGridSpec(
        num_scalar_prefetch=2,
        grid=(dest3.shape[0],),
        in_specs=[pl.BlockSpec((1, 1, te), lambda i, lo, hi: (i, 0, 0), memory_space=pltpu.SMEM),
                  pl.BlockSpec((te, d), lambda i, lo, hi: (i % n_x, 0))],
        out_specs=pl.BlockSpec(memory_space=pl.ANY),
        scratch_shapes=[pltpu.SemaphoreType.DMA((2,))],
    )
    return pl.pallas_call(
        _moe_permute_kernel,
        grid_spec=grid_spec,
        out_shape=jax.ShapeDtypeStruct((n_rows, d), F32),
        compiler_params=_cparams(("arbitrary",)),
        name="moe_permute",
    )(fill_lo, fill_hi, dest3, x)


def _moe_experts_kernel(te_ref, nv_ref, x_ref, wg_ref, wu_ref, wd_ref, y_ref, xb_s, acc_s):
    t = pl.program_id(0)
    f = pl.program_id(1)

    @pl.when(f == 0)
    def _():
        xb_s[...] = x_ref[...].astype(BF16)
        acc_s[...] = jnp.zeros_like(acc_s)

    @pl.when(nv_ref[t] > 0)
    def _():
        acc_s[...] += _swiglu_partial(xb_s[...], wg_ref[0], wu_ref[0], wd_ref[0])

    @pl.when(f == pl.num_programs(1) - 1)
    def _():
        y_ref[...] = acc_s[...]


def _moe_experts(xs, tile_exp, n_valid, tm, w_gate, w_up, w_down):
    n_rows, d = xs.shape
    ff = w_gate.shape[2]
    tf = _ff_tile(ff, 1792)
    grid_spec = pltpu.PrefetchScalarGridSpec(
        num_scalar_prefetch=2,
        grid=(n_rows // tm, ff // tf),
        in_specs=[pl.BlockSpec((tm, d), lambda t, f, te, nv: (t, 0)),
                  pl.BlockSpec((1, d, tf), lambda t, f, te, nv: (te[t], 0, f)),
                  pl.BlockSpec((1, d, tf), lambda t, f, te, nv: (te[t], 0, f)),
                  pl.BlockSpec((1, tf, d), lambda t, f, te, nv: (te[t], f, 0))],
        out_specs=pl.BlockSpec((tm, d), lambda t, f, te, nv: (t, 0)),
        scratch_shapes=[pltpu.VMEM((tm, d), BF16), pltpu.VMEM((tm, d), F32)],
    )
    return pl.pallas_call(
        _moe_experts_kernel,
        grid_spec=grid_spec,
        out_shape=jax.ShapeDtypeStruct((n_rows, d), F32),
        compiler_params=_cparams(("parallel", "arbitrary")),
        name="moe_experts",
    )(tile_exp, n_valid, xs, w_gate.astype(BF16), w_up.astype(BF16), w_down.astype(BF16))


def _moe_combine_kernel(alpha, d0_ref, d1_ref, d0n_ref, d1n_ref, x_ref, w_ref, g_ref, b_ref,
                        ys_hbm, o_ref, yg_s, sem):
    i = pl.program_id(0)
    tm = x_ref.shape[0]
    buf = i % 2

    def slot_row(idx_ref, k, r, b):
        return pltpu.make_async_copy(ys_hbm.at[pl.ds(idx_ref[0, 0, r], 1), :],
                                     yg_s.at[b, k, pl.ds(r, 1), :], sem.at[b])

    def each_row(fn):
        def body(r, c):
            fn(r)
            return c
        lax.fori_loop(0, tm, body, 0, unroll=8)

    def start(i0, i1, b):
        each_row(lambda r: (slot_row(i0, 0, r, b).start(), slot_row(i1, 1, r, b).start()))

    @pl.when(i == 0)
    def _():
        start(d0_ref, d1_ref, 0)

    @pl.when(i + 1 < pl.num_programs(0))
    def _():
        start(d0n_ref, d1n_ref, 1 - buf)

    each_row(lambda r: (slot_row(d0_ref, 0, r, buf).wait(), slot_row(d1_ref, 1, r, buf).wait()))
    w = w_ref[...]
    mix = yg_s[buf, 0] * w[:, 0:1] + yg_s[buf, 1] * w[:, 1:2]
    o_ref[...] = _layer_norm(alpha * x_ref[...] + mix, g_ref[...], b_ref[...])


def _moe_combine_deepnorm(x, ys, dest, wgt, ln_g, ln_b, alpha):
    m, d = x.shape
    tm = _tile(m, 512)
    n = m // tm
    dest3 = dest.reshape(2 * n, 1, tm)
    smem_rows = lambda imap: pl.BlockSpec((1, 1, tm), imap, memory_space=pltpu.SMEM)
    nxt = lambda i: jnp.minimum(i + 1, n - 1)
    return pl.pallas_call(
        functools.partial(_moe_combine_kernel, alpha),
        grid=(n,),
        in_specs=[smem_rows(lambda i: (i, 0, 0)),
                  smem_rows(lambda i: (n + i, 0, 0)),
                  smem_rows(lambda i: (nxt(i), 0, 0)),
                  smem_rows(lambda i: (n + nxt(i), 0, 0)),
                  pl.BlockSpec((tm, d), lambda i: (i, 0)),
                  pl.BlockSpec((tm, LANES), lambda i: (i, 0)),
                  pl.BlockSpec((1, d), lambda i: (0, 0)),
                  pl.BlockSpec((1, d), lambda i: (0, 0)),
                  pl.BlockSpec(memory_space=pl.ANY)],
        out_specs=pl.BlockSpec((tm, d), lambda i: (i, 0)),
        out_shape=jax.ShapeDtypeStruct((m, d), F32),
        scratch_shapes=[pltpu.VMEM((2, 2, tm, d), F32), pltpu.SemaphoreType.DMA((2,))],
        compiler_params=_cparams(("arbitrary",)),
        name="moe_combine_deepnorm",
    )(dest3, dest3, dest3, dest3, x, wgt, ln_g.reshape(1, d), ln_b.reshape(1, d), ys)


def _moe_deepnorm(x, router, w_gate, w_up, w_down, ln_g, ln_b, alpha):
    m = x.shape[0]
    n_exp = router.shape[1]
    tm = _tile(m, MOE_ROWS)
    idx, wgt = _router_top2(x, router)
    dest, tile_exp, n_valid, fill_lo, fill_hi = _moe_plan(idx, m, tm, n_exp)
    xs = _moe_permute(x, dest, fill_lo, fill_hi, 2 * m + n_exp * tm)
    ys = _moe_experts(xs, tile_exp, n_valid, tm, w_gate, w_up, w_down)
    return _moe_combine_deepnorm(x, ys, dest, wgt, ln_g, ln_b, alpha)


def _proj_pairs_kernel(transposed, x_ref, w_ref, o_ref, xb_s):
    @pl.when(pl.program_id(1) == 0)
    def _():
        xb_s[...] = x_ref[...].astype(BF16)

    n_slab = o_ref.shape[0]
    if transposed:
        res = _dot_nt(w_ref[...], xb_s[...])
        for q in range(n_slab):
            o_ref[q] = res[q * LANES:(q + 1) * LANES, :].astype(o_ref.dtype)
    else:
        res = _dot(xb_s[...], w_ref[...])
        for q in range(n_slab):
            o_ref[q] = res[:, q * LANES:(q + 1) * LANES].astype(o_ref.dtype)


def _proj_pairs(x, w, transposed, out_dtype=F32):
    m, d = x.shape
    n = w.shape[1]
    tm = _tile(m, 512)
    tn = _tile(n, 1024)
    n_slab = tn // LANES
    if transposed:
        wb = w.T.astype(BF16)
        w_spec = pl.BlockSpec((tn, d), lambda i, j: (j, 0))
        o_spec = pl.BlockSpec((n_slab, LANES, tm), lambda i, j: (j, 0, i))
        o_shape = jax.ShapeDtypeStruct((n // LANES, LANES, m), out_dtype)
    else:
        wb = w.astype(BF16)
        w_spec = pl.BlockSpec((d, tn), lambda i, j: (0, j))
        o_spec = pl.BlockSpec((n_slab, tm, LANES), lambda i, j: (j, i, 0))
        o_shape = jax.ShapeDtypeStruct((n // LANES, m, LANES), out_dtype)
    return pl.pallas_call(
        functools.partial(_proj_pairs_kernel, transposed),
        grid=(m // tm, n // tn),
        in_specs=[pl.BlockSpec((tm, d), lambda i, j: (i, 0)), w_spec],
        out_specs=o_spec,
        out_shape=o_shape,
        scratch_shapes=[pltpu.VMEM((tm, d), BF16)],
        compiler_params=_cparams(("parallel", "arbitrary")),
        name="proj_pairs_t" if transposed else "proj_pairs",
    )(x, wb)


def _block_mean_kernel(k_ref, o_ref):
    k = k_ref[0]
    nb = k.shape[0] // MOBA_BLOCK
    o_ref[0] = jnp.mean(k.reshape(nb, MOBA_BLOCK, LANES), axis=1)


def _block_means(k_pairs, batch):
    n_pair, m, _ = k_pairs.shape
    seq = m // batch
    nb = seq // MOBA_BLOCK
    return pl.pallas_call(
        _block_mean_kernel,
        grid=(n_pair, batch),
        in_specs=[pl.BlockSpec((1, seq, LANES), lambda p, b: (p, b, 0))],
        out_specs=pl.BlockSpec((1, nb, LANES), lambda p, b: (p, b, 0)),
        out_shape=jax.ShapeDtypeStruct((n_pair, batch * nb, LANES), F32),
        compiler_params=_cparams(("parallel", "parallel")),
        name="moba_block_means",
    )(k_pairs)


def _moba_kernel(scale, qt_ref, k_ref, vt_ref, km_ref, o_ref, neg_s, sca_s, scb_s, m_s, acc_s):
    own = pl.program_id(2)
    blk = MOBA_BLOCK
    n_slab = qt_ref.shape[0]
    nb = km_ref.shape[1]
    row = lax.broadcasted_iota(jnp.int32, (LANES, blk), 0)
    zero_q = jnp.zeros((LANES, blk), F32)
    n_iota = lax.broadcasted_iota(jnp.int32, (nb, blk), 0).astype(F32)
    past = n_iota < own.astype(F32)
    neg_inf = jnp.float32(-jnp.inf)

    streams = [(g, h) for g in range(n_slab) for h in range(2)]
    qh = []
    for s, (g, h) in enumerate(streams):
        in_head = (row < HEAD_DIM) if h == 0 else (row >= HEAD_DIM)
        q_h = jnp.where(in_head, qt_ref[g], zero_q)
        qh.append((q_h * (scale * LOG2E)).astype(BF16))
        gate = _dot_sp(_split(km_ref[g]), _split(q_h))
        gate = jnp.where(past, gate, neg_inf)
        neg = jnp.full((nb, blk), NEG_BIG, F32)
        for _ in range(min(MOBA_TOPK, nb)):
            mx = jnp.max(gate, axis=0, keepdims=True)
            idx = jnp.min(jnp.where(gate == mx, n_iota, float(nb)), axis=0, keepdims=True)
            pick = n_iota == idx
            neg = jnp.where(jnp.logical_and(pick, past), 0.0, neg)
            gate = jnp.where(pick, neg_inf, gate)
        neg_s[s] = neg

    ones_rows = jnp.ones((DEN_ROWS, blk), BF16)

    def block_scores(n, s):
        start = pl.multiple_of(n * blk, blk)
        return _dot(k_ref[streams[s][0], pl.ds(start, blk), :], qh[s])

    def absorb(n, st, bias, s):
        g, h = streams[s]
        start = pl.multiple_of(n * blk, blk)
        mx = m_s[s]
        mblk = jnp.max(st, axis=0, keepdims=True)
        if bias is not None:
            mblk = mblk + bias
        mx_new = jnp.maximum(mx, mblk)
        alpha = jnp.exp2(mx - mx_new)
        shift = mx_new if bias is None else mx_new - bias
        p = jnp.exp2(st - shift).astype(BF16)
        vtb = vt_ref[g, h * HEAD_DIM:(h + 1) * HEAD_DIM, pl.ds(start, blk)]
        m_s[s] = mx_new
        acc_s[s] = alpha * acc_s[s] + _dot(jnp.concatenate([vtb, ones_rows], axis=0), p)

    kpos = lax.broadcasted_iota(jnp.int32, (blk, blk), 0)
    qpos = lax.broadcasted_iota(jnp.int32, (blk, blk), 1)
    causal = kpos <= qpos
    own_scores = [block_scores(own, s) for s in range(len(streams))]
    for s in range(len(streams)):
        sca_s[s] = block_scores(0, s)
        m_s[s] = jnp.full((1, blk), neg_inf, F32)
        acc_s[s] = jnp.zeros((HEAD_DIM + DEN_ROWS, blk), F32)
    for s in range(len(streams)):
        absorb(own, jnp.where(causal, own_scores[s], neg_inf), None, s)

    def body(j, carry):
        n0 = 2 * j
        n1 = n0 + 1
        n2 = jnp.minimum(n0 + 2, nb - 1)
        for s in range(len(streams)):
            scb_s[s] = block_scores(n1, s)
            absorb(n0, sca_s[s], neg_s[s, pl.ds(n0, 1), :], s)
        for s in range(len(streams)):
            sca_s[s] = block_scores(n2, s)
            absorb(n1, scb_s[s], neg_s[s, pl.ds(n1, 1), :], s)
        return carry

    lax.fori_loop(0, (own + 1) // 2, body, 0)
    for g in range(n_slab):
        a0, a1 = acc_s[2 * g], acc_s[2 * g + 1]
        o_t = jnp.concatenate([a0[:HEAD_DIM] / a0[HEAD_DIM:HEAD_DIM + 1],
                               a1[:HEAD_DIM] / a1[HEAD_DIM:HEAD_DIM + 1]], axis=0)
        o_ref[g] = o_t.T.astype(o_ref.dtype)


def _moba_attention(q_t, k, v_t, k_means, batch):
    n_pair, _, m = q_t.shape
    seq = m // batch
    nb = seq // MOBA_BLOCK
    g = _tile(n_pair, MOBA_SLABS)
    return pl.pallas_call(
        functools.partial(_moba_kernel, HEAD_DIM ** -0.5),
        grid=(n_pair // g, batch, nb),
        in_specs=[pl.BlockSpec((g, LANES, MOBA_BLOCK), lambda p, b, i: (p, 0, b * nb + i)),
                  pl.BlockSpec((g, seq, LANES), lambda p, b, i: (p, b, 0),
                               pipeline_mode=pl.Buffered(1)),
                  pl.BlockSpec((g, LANES, seq), lambda p, b, i: (p, 0, b),
                               pipeline_mode=pl.Buffered(1)),
                  pl.BlockSpec((g, nb, LANES), lambda p, b, i: (p, b, 0))],
        out_specs=pl.BlockSpec((g, MOBA_BLOCK, LANES), lambda p, b, i: (p, b * nb + i, 0)),
        out_shape=jax.ShapeDtypeStruct((n_pair, m, LANES), BF16),
        scratch_shapes=[pltpu.VMEM((2 * g, nb, MOBA_BLOCK), F32),
                        pltpu.VMEM((2 * g, MOBA_BLOCK, MOBA_BLOCK), F32),
                        pltpu.VMEM((2 * g, MOBA_BLOCK, MOBA_BLOCK), F32),
                        pltpu.VMEM((2 * g, 1, MOBA_BLOCK), F32),
                        pltpu.VMEM((2 * g, HEAD_DIM + DEN_ROWS, MOBA_BLOCK), F32)],
        compiler_params=_cparams(("parallel", "parallel", "arbitrary")),
        name="moba_attention",
    )(q_t, k, v_t, k_means)


def kernel(x, rwkv_mu, rwkv_w_rkv, rwkv_w_out, rwkv_decay_w0, rwkv_decay_w1, rwkv_decay_w2, rwkv_iclr_a0, rwkv_iclr_a1, rwkv_iclr_a2, rwkv_vres_v0, rwkv_vres_v1, rwkv_vres_v2, rwkv_gate_g1, rwkv_gate_g2, rwkv_k_k, rwkv_k_a, rwkv_r_k, rwkv_gn_g, rwkv_gn_b, moba_w_k, moba_w_v, moba_w_q, moba_w_o, ffn_w_gate, ffn_w_up, ffn_w_down, moe_router, moe_w_gate, moe_w_up, moe_w_down, ln_g, ln_b):
    batch, seq, d = x.shape
    assert d % (2 * LANES) == 0 and seq % MOBA_BLOCK == 0 and seq % WKV_CHUNK == 0
    depth = ln_g.shape[0]
    n_rwkv = rwkv_mu.shape[0]
    alpha = (2.0 * depth) ** 0.25
    h = x.reshape(batch * seq, d)
    v_first = None
    kv = None
    for layer in range(depth):
        if layer < n_rwkv:
            i = layer
            vres = None if i == 0 else (rwkv_vres_v0[i - 1], rwkv_vres_v1[i - 1], rwkv_vres_v2[i - 1])
            r, e, k, v, a, b, g = _rwkv_proj(
                h, seq, rwkv_mu[i], rwkv_w_rkv[i], rwkv_decay_w1[i], rwkv_iclr_a1[i],
                rwkv_gate_g1[i], rwkv_decay_w2[i], rwkv_iclr_a2[i], rwkv_gate_g2[i],
                rwkv_decay_w0[i], rwkv_iclr_a0[i], rwkv_k_k[i], rwkv_k_a[i],
                vres=vres, v_first=v_first)
            if i == 0:
                v_first = v
            y = _wkv_scan(r, e, k, v, a, b, batch)
            mix = _rwkv_post(y, r, k, v, g, rwkv_r_k[i], rwkv_gn_g[i], rwkv_gn_b[i])
            w_out = rwkv_w_out[i]
        else:
            jdx = layer - n_rwkv
            k_pairs, v_t, k_means = kv
            q_t = _proj_pairs(h, moba_w_q[jdx], transposed=True)
            mix = _moba_attention(q_t, k_pairs, v_t, k_means, batch)
            w_out = moba_w_o[jdx]
        h = _out_proj_deepnorm(mix, w_out, h, ln_g[layer, 0], ln_b[layer, 0], alpha)
        ex = layer // 2
        if layer % 2 == 0:
            h = _ffn_deepnorm(h, ffn_w_gate[ex], ffn_w_up[ex], ffn_w_down[ex],
                              ln_g[layer, 1], ln_b[layer, 1], alpha)
        else:
            h = _moe_deepnorm(h, moe_router[ex], moe_w_gate[ex], moe_w_up[ex], moe_w_down[ex],
                              ln_g[layer, 1], ln_b[layer, 1], alpha)
        if layer == n_rwkv - 1:
            k_pairs = _proj_pairs(h, moba_w_k, transposed=False)
            v_t = _proj_pairs(h, moba_w_v, transposed=True, out_dtype=BF16)
            kv = (k_pairs.astype(BF16), v_t, _block_means(k_pairs, batch))
    return h.reshape(batch, seq, d)
```

```python
import functools
import math

import jax
import jax.numpy as jnp
from jax import lax
from jax.experimental import pallas as pl
from jax.experimental.pallas import tpu as pltpu

HEAD_DIM = 64
LANES = 128
GN_EPS = 64e-5
LN_EPS = 1e-5
MOBA_BLOCK = 256
MOBA_TOPK = 3
MOE_ROWS = 512
FFN_TILE = 2816
CAST_BLOCK_ELEMS = 1 << 20
WKV_CHUNK = 64
WKV_SLABS = 8
NEG_BIG = -1e30
MOBA_SLABS = 8
DEN_ROWS = 16
LOG2E = 1.4426950408889634

F32 = jnp.float32
BF16 = jnp.bfloat16
VMEM_LIMIT = 56 * 1024 * 1024


def _cparams(sem):
    return pltpu.CompilerParams(dimension_semantics=sem, vmem_limit_bytes=VMEM_LIMIT)


def _dot(a, b):
    return jnp.dot(a, b, preferred_element_type=F32)


def _dot_nt(a, b):
    return lax.dot_general(a, b, (((1,), (1,)), ((), ())), preferred_element_type=F32)


def _split(x):
    hi = x.astype(BF16)
    lo = (x - hi.astype(F32)).astype(BF16)
    return hi, lo


def _dot_sp(a, b, nt=False):
    d = _dot_nt if nt else _dot
    return d(a[0], b[0]) + (d(a[0], b[1]) + d(a[1], b[0]))


def _dot_exact_rhs(a, b_exact, nt=False):
    d = _dot_nt if nt else _dot
    hi = a.astype(BF16)
    r1 = a - hi.astype(F32)
    mid = r1.astype(BF16)
    lo = (r1 - mid.astype(F32)).astype(BF16)
    return d(hi, b_exact) + (d(mid, b_exact) + d(lo, b_exact))


def _sigmoid(x):
    return 1.0 / (1.0 + jnp.exp(-x))


def _layer_norm(y, g, b):
    mu = jnp.mean(y, axis=-1, keepdims=True)
    yc = y - mu
    var = jnp.mean(yc * yc, axis=-1, keepdims=True)
    return yc * lax.rsqrt(var + LN_EPS) * g + b


def _head_blockdiag(n):
    i = jnp.arange(n) // HEAD_DIM
    return (i[:, None] == i[None, :]).astype(BF16)


def _tile(n, want):
    t = min(n, want)
    assert n % t == 0, (n, want)
    return t


def _cast_kernel(w_ref, o_ref):
    o_ref[...] = w_ref[...].astype(o_ref.dtype)


def _layer_bf16(w, layer):
    shape = w.shape[1:]
    c = shape[-1]
    rows = math.prod(shape[:-1])
    tr = rows
    for cand in range(min(rows, CAST_BLOCK_ELEMS // c) // 16, 0, -1):
        if rows % (cand * 16) == 0:
            tr = cand * 16
            break
    out = pl.pallas_call(
        _cast_kernel,
        grid=(rows // tr,),
        in_specs=[pl.BlockSpec((1, tr, c), lambda i: (layer, i, 0))],
        out_specs=pl.BlockSpec((1, tr, c), lambda i: (0, i, 0)),
        out_shape=jax.ShapeDtypeStruct((1, rows, c), BF16),
        compiler_params=_cparams(("parallel",)),
        name="weight_to_bf16",
    )(w.reshape(w.shape[0], rows, c))
    return out.reshape(shape)


def _rwkv_proj_kernel(has_vres, steps_per_seq, *refs):
    if has_vres:
        (x_ref, xp_ref, mu_ref, wrkv_ref, wd1_ref, wa1_ref, wg1_ref, wv1_ref,
         wd2_ref, wa2_ref, wg2_ref, wv2_ref, w0_ref, a0_ref, v0_ref, kk_ref, ka_ref,
         bd_ref, vf_ref,
         r_out, e_out, k_out, v_out, a_out, b_out, g_out,
         xm_s, hd_s, ha_s, hg_s, hv_s) = refs
    else:
        (x_ref, xp_ref, mu_ref, wrkv_ref, wd1_ref, wa1_ref, wg1_ref,
         wd2_ref, wa2_ref, wg2_ref, w0_ref, a0_ref, kk_ref, ka_ref,
         bd_ref,
         r_out, e_out, k_out, v_out, a_out, b_out, g_out,
         xm_s, hd_s, ha_s, hg_s) = refs
    i = pl.program_id(0)
    j = pl.program_id(1)

    @pl.when(j == 0)
    def _():
        x = x_ref[...]
        tm = x.shape[0]
        prev_row = jnp.where(i % steps_per_seq == 0, 0.0, xp_ref[7:8, :])
        rolled = pltpu.roll(x, 1, axis=0)
        row = lax.broadcasted_iota(jnp.int32, (tm, 1), 0)
        x_prev = jnp.where(row == 0, prev_row, rolled)
        xx = x_prev - x
        for c in range(3):
            xm_s[c] = (x + xx * mu_ref[c:c + 1, :]).astype(BF16)
        xw = (x + xx * mu_ref[3:4, :]).astype(BF16)
        xa = (x + xx * mu_ref[4:5, :]).astype(BF16)
        xg = (x + xx * mu_ref[5:6, :]).astype(BF16)
        hd_s[...] = jnp.tanh(_dot(xw, wd1_ref[...])).astype(BF16)
        ha_s[...] = _dot(xa, wa1_ref[...]).astype(BF16)
        hg_s[...] = _sigmoid(_dot(xg, wg1_ref[...])).astype(BF16)
        if has_vres:
            hv_s[...] = _dot(xm_s[2], wv1_ref[...]).astype(BF16)

    r = _dot(xm_s[0], wrkv_ref[0])
    k = _dot(xm_s[1], wrkv_ref[1])
    v = _dot(xm_s[2], wrkv_ref[2])
    z = w0_ref[...] + _dot(hd_s[...], wd2_ref[...])
    nz = -z
    softplus = jnp.maximum(nz, 0.0) + jnp.log(1.0 + jnp.exp(-jnp.abs(nz)))
    e = jnp.exp(-softplus - 0.5)
    a = _sigmoid(a0_ref[...] + _dot(ha_s[...], wa2_ref[...]))
    g = _dot(hg_s[...], wg2_ref[...])
    n_slab = r.shape[1] // LANES
    if has_vres:
        vf = jnp.concatenate([vf_ref[q] for q in range(n_slab)], axis=1)
        v = v + (vf - v) * _sigmoid(v0_ref[...] + _dot(hv_s[...], wv2_ref[...]))
    kk = k * kk_ref[...]
    ss = _dot_exact_rhs(kk * kk, bd_ref[...])
    kk = kk * lax.rsqrt(jnp.maximum(ss, 1e-24))
    k = k * (1.0 + (a - 1.0) * ka_ref[...])
    for q in range(n_slab):
        sl = slice(q * LANES, (q + 1) * LANES)
        r_out[q] = r[:, sl].astype(r_out.dtype)
        e_out[q] = e[:, sl]
        k_out[q] = k[:, sl].astype(k_out.dtype)
        v_out[q] = v[:, sl].astype(v_out.dtype)
        a_out[q] = (-kk)[:, sl].astype(a_out.dtype)
        b_out[q] = (kk * a)[:, sl].astype(b_out.dtype)
        g_out[q] = g[:, sl].astype(g_out.dtype)


def _rwkv_proj(x, seq_len, mu, w_rkv, wd1, wa1, wg1, wd2, wa2, wg2, w0, a0, k_k, k_a,
               vres=None, v_first=None):
    m, d = x.shape
    tm = _tile(seq_len, 512)
    tn = _tile(d, 256)
    n_slab = tn // LANES
    has_vres = vres is not None
    row = lambda a: a.reshape(1, d)
    full = lambda a: pl.BlockSpec(a.shape, lambda i, j: (0,) * a.ndim)
    colblk = lambda rows: pl.BlockSpec((rows, tn), lambda i, j: (0, j))
    bd = _head_blockdiag(tn)
    wd1, wa1, wg1 = wd1.astype(BF16), wa1.astype(BF16), wg1.astype(BF16)
    wd2, wa2, wg2 = wd2.astype(BF16), wa2.astype(BF16), wg2.astype(BF16)
    w_rkv = w_rkv.astype(BF16)
    args = [x, x, mu, w_rkv, wd1, wa1, wg1]
    specs = [pl.BlockSpec((tm, d), lambda i, j: (i, 0)),
             pl.BlockSpec((8, d), lambda i, j: (jnp.maximum(i * (tm // 8) - 1, 0), 0)),
             full(mu),
             pl.BlockSpec((3, d, tn), lambda i, j: (0, 0, j)),
             full(wd1), full(wa1), full(wg1)]
    if has_vres:
        v0, wv1, wv2 = vres
        wv1, wv2 = wv1.astype(BF16), wv2.astype(BF16)
        args += [wv1]
        specs += [full(wv1)]
    args += [wd2, wa2, wg2]
    specs += [colblk(wd2.shape[0]), colblk(wa2.shape[0]), colblk(wg2.shape[0])]
    if has_vres:
        args += [wv2]
        specs += [colblk(wv2.shape[0])]
    args += [row(w0), row(a0)]
    specs += [colblk(1), colblk(1)]
    if has_vres:
        args += [row(v0)]
        specs += [colblk(1)]
    args += [row(k_k), row(k_a), bd]
    specs += [colblk(1), colblk(1), full(bd)]
    slab_spec = pl.BlockSpec((n_slab, tm, LANES), lambda i, j: (j, i, 0))
    if has_vres:
        args += [v_first]
        specs += [slab_spec]
    slab = lambda dt: jax.ShapeDtypeStruct((d // LANES, m, LANES), dt)
    scratch = [pltpu.VMEM((3, tm, d), BF16),
               pltpu.VMEM((tm, wd1.shape[1]), BF16),
               pltpu.VMEM((tm, wa1.shape[1]), BF16),
               pltpu.VMEM((tm, wg1.shape[1]), BF16)]
    if has_vres:
        scratch.append(pltpu.VMEM((tm, wv1.shape[1]), BF16))
    return pl.pallas_call(
        functools.partial(_rwkv_proj_kernel, has_vres, seq_len // tm),
        grid=(m // tm, d // tn),
        in_specs=specs,
        out_specs=[slab_spec] * 7,
        out_shape=[slab(BF16), slab(F32)] + [slab(BF16)] * 5,
        scratch_shapes=scratch,
        compiler_params=_cparams(("parallel", "arbitrary")),
        name="rwkv_proj",
    )(*args)


def _wkv_kernel(r_ref, e_ref, k_ref, v_ref, a_ref, b_ref, y_ref, s_ref):
    c = WKV_CHUNK
    n_slab, n_batch, tb, _ = r_ref.shape
    n_chunks = tb // c

    @pl.when(pl.program_id(1) == 0)
    def _():
        s_ref[...] = jnp.zeros_like(s_ref)

    lane = lax.broadcasted_iota(jnp.int32, (c, LANES), 1)
    t_idx = lax.broadcasted_iota(jnp.int32, (c, LANES), 0)
    head0 = lane < HEAD_DIM
    s_idx = jnp.bitwise_and(lane, HEAD_DIM - 1)
    strict = s_idx < t_idx
    incl = s_idx <= t_idx
    rr = lax.broadcasted_iota(jnp.int32, (LANES, LANES), 0)
    cc = lax.broadcasted_iota(jnp.int32, (LANES, LANES), 1)
    same_head = (rr < HEAD_DIM) == (cc < HEAD_DIM)

    def bd(x):
        z = jnp.zeros_like(x)
        return jnp.concatenate([jnp.where(head0, x, z), jnp.where(head0, z, x)], axis=0)

    def cat(a, b, axis=0):
        return jnp.concatenate([a, b], axis=axis)

    def chunk(ci, carry):
        stages = [one_slab(ci, p, bb) for p in range(n_slab) for bb in range(n_batch)]
        while stages:
            stages = [g for g in stages if next(g, "done") != "done"]
        return carry

    def one_slab(ci, p, bb):
        sl = pl.ds(pl.multiple_of(ci * c, c), c)
        r = r_ref[p, bb, sl, :]
        e = e_ref[p, bb, sl, :]
        k = k_ref[p, bb, sl, :]
        v = v_ref[p, bb, sl, :]
        a = a_ref[p, bb, sl, :]
        b = b_ref[p, bb, sl, :]
        s0 = s_ref[p * n_batch + bb]
        cum = _cumsum_rows(e, t_idx)
        yield
        tot = cum[c - 1:c, :]
        ar = cat(a * jnp.exp(e - cum), r * jnp.exp(-cum)).astype(BF16)
        ec = jnp.exp(cum)
        bt = (b * ec).astype(BF16)
        kt = (k * ec).astype(BF16)
        eh = jnp.exp(cum - tot)
        bk = cat(b * eh, k * eh).astype(BF16)
        bdv = bd(v.astype(BF16))
        gram = _dot_nt(ar, cat(bd(bt), bd(kt)))
        yield
        zero = jnp.zeros((c, LANES), F32)
        l_ab = jnp.where(strict, gram[:c, :LANES], zero)
        n_ak = jnp.where(strict, gram[:c, LANES:], zero)
        m_rb = jnp.where(incl, gram[c:, :LANES], zero)
        m_rk = jnp.where(incl, gram[c:, LANES:], zero)
        xy = _dot_nt(ar, s0.astype(BF16)) + _dot(cat(n_ak, m_rk).astype(BF16), bdv)
        x = xy[:c]
        y0 = xy[c:]
        yield
        lp = l_ab.astype(BF16)
        n_steps = int(math.log2(c))
        for step in range(n_steps):
            bdx = bd(x.astype(BF16))
            if step + 1 < n_steps:
                t = _dot(lp, cat(bdx, bd(lp), axis=1))
                x = x + t[:, :LANES]
                lp = t[:, LANES:].astype(BF16)
            else:
                x = x + _dot(lp, bdx)
            yield
        y_ref[p, bb, sl, :] = y0 + _dot(m_rb.astype(BF16), bd(x.astype(BF16)))
        uv_t = cat(x, v.astype(F32)).T.astype(BF16)
        upd = _dot(uv_t, bk)
        s_ref[p * n_batch + bb] = (s0 * jnp.exp(-tot)
                                   + jnp.where(same_head, upd, jnp.zeros_like(upd)))
        yield

    lax.fori_loop(0, n_chunks, chunk, 0)


def _cumsum_rows(e, row_idx):
    cum = e
    shift = 1
    while shift < e.shape[0]:
        cum = cum + jnp.where(row_idx >= shift, pltpu.roll(cum, shift, axis=0), 0.0)
        shift *= 2
    return cum


def _wkv_scan(r, e, k, v, a, b, batch):
    n_pair, m, _ = r.shape
    seq = m // batch
    tb = _tile(seq, 256)
    n_slab = _tile(n_pair, WKV_SLABS)
    view = lambda t: t.reshape(n_pair, batch, seq, LANES)
    spec = pl.BlockSpec((n_slab, batch, tb, LANES), lambda p, t: (p, 0, t, 0))
    y = pl.pallas_call(
        _wkv_kernel,
        grid=(n_pair // n_slab, seq // tb),
        in_specs=[spec] * 6,
        out_specs=spec,
        out_shape=jax.ShapeDtypeStruct((n_pair, batch, seq, LANES), F32),
        scratch_shapes=[pltpu.VMEM((n_slab * batch, LANES, LANES), F32)],
        compiler_params=_cparams(("parallel", "arbitrary")),
        name="wkv_scan",
    )(view(r), view(e), view(k), view(v), view(a), view(b))
    return y.reshape(n_pair, m, LANES)


def _rwkv_post_kernel(y_ref, r_ref, k_ref, v_ref, g_ref, rk_ref, gg_ref, gb_ref, bd_ref, z_ref):
    y = y_ref[0]
    bdm = bd_ref[...]
    inv_n = 1.0 / HEAD_DIM
    mean = _dot_exact_rhs(y, bdm) * inv_n
    yc = y - mean
    var = _dot_exact_rhs(yc * yc, bdm) * inv_n
    yn = yc * lax.rsqrt(var + GN_EPS) * gg_ref[0] + gb_ref[0]
    rk = r_ref[0].astype(F32) * k_ref[0].astype(F32) * rk_ref[0]
    bonus = _dot_exact_rhs(rk, bdm) * v_ref[0].astype(F32)
    z_ref[0] = ((yn + bonus) * g_ref[0].astype(F32)).astype(z_ref.dtype)


def _rwkv_post(y, r, k, v, g, r_k, gn_g, gn_b):
    n_pair, m, _ = y.shape
    tm = _tile(m, 2048)
    spec = pl.BlockSpec((1, tm, LANES), lambda p, i: (p, i, 0))
    pspec = pl.BlockSpec((1, 1, LANES), lambda p, i: (p, 0, 0))
    bd = _head_blockdiag(LANES)
    slab = lambda a: a.reshape(n_pair, 1, LANES)
    return pl.pallas_call(
        _rwkv_post_kernel,
        grid=(n_pair, m // tm),
        in_specs=[spec] * 5 + [pspec] * 3 + [pl.BlockSpec(bd.shape, lambda p, i: (0, 0))],
        out_specs=spec,
        out_shape=jax.ShapeDtypeStruct((n_pair, m, LANES), BF16),
        compiler_params=_cparams(("parallel", "parallel")),
        name="rwkv_post",
    )(y, r, k, v, g, slab(r_k), slab(gn_g), slab(gn_b), bd)


def _out_proj_kernel(alpha, z_ref, w_ref, x_ref, g_ref, b_ref, o_ref):
    n_pair = z_ref.shape[0]
    acc = _dot(z_ref[0], w_ref[0])
    for p in range(1, n_pair):
        acc = acc + _dot(z_ref[p], w_ref[p])
    o_ref[...] = _layer_norm(alpha * x_ref[...] + acc, g_ref[...], b_ref[...])


def _out_proj_deepnorm(z, w, x, ln_g, ln_b, alpha):
    n_pair, m, _ = z.shape
    d = w.shape[1]
    tm = _tile(m, 512)
    w3 = w.astype(BF16).reshape(n_pair, LANES, d)
    return pl.pallas_call(
        functools.partial(_out_proj_kernel, alpha),
        grid=(m // tm,),
        in_specs=[pl.BlockSpec((n_pair, tm, LANES), lambda i: (0, i, 0)),
                  pl.BlockSpec(w3.shape, lambda i: (0, 0, 0)),
                  pl.BlockSpec((tm, d), lambda i: (i, 0)),
                  pl.BlockSpec((1, d), lambda i: (0, 0)),
                  pl.BlockSpec((1, d), lambda i: (0, 0))],
        out_specs=pl.BlockSpec((tm, d), lambda i: (i, 0)),
        out_shape=jax.ShapeDtypeStruct((m, d), F32),
        compiler_params=_cparams(("parallel",)),
        name="out_proj_deepnorm",
    )(z, w3, x, ln_g.reshape(1, d), ln_b.reshape(1, d))


def _ff_tile(ff, want):
    for cand in range(min(ff, want) // LANES, 0, -1):
        if ff % (cand * LANES) == 0:
            return cand * LANES
    return ff


def _swiglu_partial(xb, wg, wu, wd):
    h1 = _dot(xb, wg)
    h2 = _dot(xb, wu)
    act = h1 * _sigmoid(h1) * h2
    return _dot(act.astype(BF16), wd)


def _ffn_kernel(alpha, x_ref, wg_ref, wu_ref, wd_ref, g_ref, b_ref, o_ref, xb_s, acc_s):
    f = pl.program_id(1)

    @pl.when(f == 0)
    def _():
        xb_s[...] = x_ref[...].astype(BF16)
        acc_s[...] = jnp.zeros_like(acc_s)

    acc_s[...] += _swiglu_partial(xb_s[...], wg_ref[...], wu_ref[...], wd_ref[...])

    @pl.when(f == pl.num_programs(1) - 1)
    def _():
        o_ref[...] = _layer_norm(alpha * x_ref[...] + acc_s[...], g_ref[...], b_ref[...])


def _ffn_deepnorm(x, w_gate, w_up, w_down, ln_g, ln_b, alpha):
    m, d = x.shape
    ff = w_gate.shape[1]
    tm = _tile(m, 512)
    tf = _ff_tile(ff, FFN_TILE)
    wmode = dict(pipeline_mode=pl.Buffered(1)) if tf == ff else {}
    return pl.pallas_call(
        functools.partial(_ffn_kernel, alpha),
        grid=(m // tm, ff // tf),
        in_specs=[pl.BlockSpec((tm, d), lambda i, f: (i, 0)),
                  pl.BlockSpec((d, tf), lambda i, f: (0, f), **wmode),
                  pl.BlockSpec((d, tf), lambda i, f: (0, f), **wmode),
                  pl.BlockSpec((tf, d), lambda i, f: (f, 0), **wmode),
                  pl.BlockSpec((1, d), lambda i, f: (0, 0)),
                  pl.BlockSpec((1, d), lambda i, f: (0, 0))],
        out_specs=pl.BlockSpec((tm, d), lambda i, f: (i, 0)),
        out_shape=jax.ShapeDtypeStruct((m, d), F32),
        scratch_shapes=[pltpu.VMEM((tm, d), BF16), pltpu.VMEM((tm, d), F32)],
        compiler_params=_cparams(("parallel", "arbitrary")),
        name="ffn_deepnorm",
    )(x, w_gate.astype(BF16), w_up.astype(BF16), w_down.astype(BF16),
      ln_g.reshape(1, d), ln_b.reshape(1, d))


def _router_kernel(n_exp, x_ref, w_ref, idx_ref, wgt_ref):
    logits = _dot_sp(_split(x_ref[...]), _split(w_ref[...]))
    lane = lax.broadcasted_iota(jnp.int32, logits.shape, 1).astype(F32)
    neg_inf = jnp.float32(-jnp.inf)
    lg = jnp.where(lane < n_exp, logits, neg_inf)
    m1 = jnp.max(lg, axis=1, keepdims=True)
    i1 = jnp.min(jnp.where(lg == m1, lane, float(LANES)), axis=1, keepdims=True)
    lg2 = jnp.where(lane == i1, neg_inf, lg)
    m2 = jnp.max(lg2, axis=1, keepdims=True)
    i2 = jnp.min(jnp.where(lg2 == m2, lane, float(LANES)), axis=1, keepdims=True)
    e2 = jnp.exp(m2 - m1)
    den = 1.0 + e2
    idx_ref[...] = jnp.where(lane == 0.0, i1, jnp.where(lane == 1.0, i2, 0.0)).astype(jnp.int32)
    wgt_ref[...] = jnp.where(lane == 0.0, 1.0 / den, jnp.where(lane == 1.0, e2 / den, 0.0))


def _router_top2(x, router):
    m, d = x.shape
    n_exp = router.shape[1]
    tm = _tile(m, 512)
    w = jnp.pad(router, ((0, 0), (0, LANES - n_exp)))
    out_spec = pl.BlockSpec((tm, LANES), lambda i: (i, 0))
    return pl.pallas_call(
        functools.partial(_router_kernel, n_exp),
        grid=(m // tm,),
        in_specs=[pl.BlockSpec((tm, d), lambda i: (i, 0)),
                  pl.BlockSpec((d, LANES), lambda i: (0, 0))],
        out_specs=[out_spec, out_spec],
        out_shape=[jax.ShapeDtypeStruct((m, LANES), jnp.int32),
                   jax.ShapeDtypeStruct((m, LANES), F32)],
        compiler_params=_cparams(("parallel",)),
        name="moe_router",
    )(x, w)


def _moe_plan(idx, m, tm, n_exp):
    n_ent = 2 * m
    expert = jnp.concatenate([idx[:, 0], idx[:, 1]])
    onehot = (expert[:, None] == jnp.arange(n_exp, dtype=jnp.int32)[None, :]).astype(jnp.int32)
    csum = jnp.cumsum(onehot, axis=0)
    counts = csum[-1]
    rank = jnp.sum(csum * onehot, axis=1) - 1
    padded = ((counts + tm - 1) // tm) * tm
    ends = jnp.cumsum(padded)
    starts = ends - padded
    dest = jnp.sum(starts[None, :] * onehot, axis=1) + rank
    n_tiles = n_ent // tm + n_exp
    tile_start = jnp.arange(n_tiles, dtype=jnp.int32) * tm
    tile_exp = jnp.minimum(jnp.sum((tile_start[:, None] >= ends[None, :]).astype(jnp.int32), axis=1),
                           n_exp - 1)
    n_valid = jnp.clip(starts[tile_exp] + counts[tile_exp] - tile_start, 0, tm)
    fill_lo = jnp.concatenate([starts + counts, ends[-1:]])
    fill_hi = jnp.concatenate([ends, jnp.full((1,), n_tiles * tm, jnp.int32)])
    return dest, tile_exp, n_valid, fill_lo, fill_hi


def _moe_permute_kernel(lo_ref, hi_ref, dest_ref, x_ref, xs_hbm, sem):
    i = pl.program_id(0)
    te = dest_ref.shape[2]

    def entry_row(r):
        return pltpu.make_async_copy(x_ref.at[pl.ds(r, 1), :],
                                     xs_hbm.at[pl.ds(dest_ref[0, 0, r], 1), :], sem.at[0])

    def filler_row(r):
        return pltpu.make_async_copy(x_ref.at[pl.ds(0, 1), :], xs_hbm.at[pl.ds(r, 1), :], sem.at[1])

    def each(lo, hi, fn, unroll=1):
        def body(r, c):
            fn(r)
            return c
        lax.fori_loop(lo, hi, body, 0, unroll=unroll)

    each(0, te, lambda r: entry_row(r).start(), unroll=8)

    @pl.when(i == 0)
    def _():
        for e in range(lo_ref.shape[0]):
            each(lo_ref[e], hi_ref[e], lambda r: filler_row(r).start())
        for e in range(lo_ref.shape[0]):
            each(lo_ref[e], hi_ref[e], lambda r: filler_row(r).wait())

    each(0, te, lambda r: entry_row(r).wait(), unroll=8)


def _moe_permute(x, dest, fill_lo, fill_hi, n_rows):
    m, d = x.shape
    te = _tile(m, 512)
    n_x = m // te
    dest3 = dest.reshape(-1, 1, te)
    grid_spec = pltpu.PrefetchScalarGridSpec(
        num_scalar_prefetch=2,
        grid=(dest3.shape[0],),
        in_specs=[pl.BlockSpec((1, 1, te), lambda i, lo, hi: (i, 0, 0), memory_space=pltpu.SMEM),
                  pl.BlockSpec((te, d), lambda i, lo, hi: (i % n_x, 0))],
        out_specs=pl.BlockSpec(memory_space=pl.ANY),
        scratch_shapes=[pltpu.SemaphoreType.DMA((2,))],
    )
    return pl.pallas_call(
        _moe_permute_kernel,
        grid_spec=grid_spec,
        out_shape=jax.ShapeDtypeStruct((n_rows, d), F32),
        compiler_params=_cparams(("arbitrary",)),
        name="moe_permute",
    )(fill_lo, fill_hi, dest3, x)


def _moe_experts_kernel(te_ref, nv_ref, x_ref, wg_ref, wu_ref, wd_ref, y_ref, xb_s, acc_s):
    t = pl.program_id(0)
    f = pl.program_id(1)

    @pl.when(f == 0)
    def _():
        xb_s[...] = x_ref[...].astype(BF16)
        acc_s[...] = jnp.zeros_like(acc_s)

    @pl.when(nv_ref[t] > 0)
    def _():
        acc_s[...] += _swiglu_partial(xb_s[...], wg_ref[0], wu_ref[0], wd_ref[0])

    @pl.when(f == pl.num_programs(1) - 1)
    def _():
        y_ref[...] = acc_s[...]


def _moe_experts(xs, tile_exp, n_valid, tm, w_gate, w_up, w_down):
    n_rows, d = xs.shape
    ff = w_gate.shape[2]
    tf = _ff_tile(ff, 1792)
    grid_spec = pltpu.PrefetchScalarGridSpec(
        num_scalar_prefetch=2,
        grid=(n_rows // tm, ff // tf),
        in_specs=[pl.BlockSpec((tm, d), lambda t, f, te, nv: (t, 0)),
                  pl.BlockSpec((1, d, tf), lambda t, f, te, nv: (te[t], 0, f)),
                  pl.BlockSpec((1, d, tf), lambda t, f, te, nv: (te[t], 0, f)),
                  pl.BlockSpec((1, tf, d), lambda t, f, te, nv: (te[t], f, 0))],
        out_specs=pl.BlockSpec((tm, d), lambda t, f, te, nv: (t, 0)),
        scratch_shapes=[pltpu.VMEM((tm, d), BF16), pltpu.VMEM((tm, d), F32)],
    )
    return pl.pallas_call(
        _moe_experts_kernel,
        grid_spec=grid_spec,
        out_shape=jax.ShapeDtypeStruct((n_rows, d), F32),
        compiler_params=_cparams(("parallel", "arbitrary")),
        name="moe_experts",
    )(tile_exp, n_valid, xs, w_gate.astype(BF16), w_up.astype(BF16), w_down.astype(BF16))


def _moe_combine_kernel(alpha, d0_ref, d1_ref, d0n_ref, d1n_ref, x_ref, w_ref, g_ref, b_ref,
                        ys_hbm, o_ref, yg_s, sem):
    i = pl.program_id(0)
    tm = x_ref.shape[0]
    buf = i % 2

    def slot_row(idx_ref, k, r, b):
        return pltpu.make_async_copy(ys_hbm.at[pl.ds(idx_ref[0, 0, r], 1), :],
                                     yg_s.at[b, k, pl.ds(r, 1), :], sem.at[b])

    def each_row(fn):
        def body(r, c):
            fn(r)
            return c
        lax.fori_loop(0, tm, body, 0, unroll=8)

    def start(i0, i1, b):
        each_row(lambda r: (slot_row(i0, 0, r, b).start(), slot_row(i1, 1, r, b).start()))

    @pl.when(i == 0)
    def _():
        start(d0_ref, d1_ref, 0)

    @pl.when(i + 1 < pl.num_programs(0))
    def _():
        start(d0n_ref, d1n_ref, 1 - buf)

    each_row(lambda r: (slot_row(d0_ref, 0, r, buf).wait(), slot_row(d1_ref, 1, r, buf).wait()))
    w = w_ref[...]
    mix = yg_s[buf, 0] * w[:, 0:1] + yg_s[buf, 1] * w[:, 1:2]
    o_ref[...] = _layer_norm(alpha * x_ref[...] + mix, g_ref[...], b_ref[...])


def _moe_combine_deepnorm(x, ys, dest, wgt, ln_g, ln_b, alpha):
    m, d = x.shape
    tm = _tile(m, 512)
    n = m // tm
    dest3 = dest.reshape(2 * n, 1, tm)
    smem_rows = lambda imap: pl.BlockSpec((1, 1, tm), imap, memory_space=pltpu.SMEM)
    nxt = lambda i: jnp.minimum(i + 1, n - 1)
    return pl.pallas_call(
        functools.partial(_moe_combine_kernel, alpha),
        grid=(n,),
        in_specs=[smem_rows(lambda i: (i, 0, 0)),
                  smem_rows(lambda i: (n + i, 0, 0)),
                  smem_rows(lambda i: (nxt(i), 0, 0)),
                  smem_rows(lambda i: (n + nxt(i), 0, 0)),
                  pl.BlockSpec((tm, d), lambda i: (i, 0)),
                  pl.BlockSpec((tm, LANES), lambda i: (i, 0)),
                  pl.BlockSpec((1, d), lambda i: (0, 0)),
                  pl.BlockSpec((1, d), lambda i: (0, 0)),
                  pl.BlockSpec(memory_space=pl.ANY)],
        out_specs=pl.BlockSpec((tm, d), lambda i: (i, 0)),
        out_shape=jax.ShapeDtypeStruct((m, d), F32),
        scratch_shapes=[pltpu.VMEM((2, 2, tm, d), F32), pltpu.SemaphoreType.DMA((2,))],
        compiler_params=_cparams(("arbitrary",)),
        name="moe_combine_deepnorm",
    )(dest3, dest3, dest3, dest3, x, wgt, ln_g.reshape(1, d), ln_b.reshape(1, d), ys)


def _moe_deepnorm(x, router, w_gate, w_up, w_down, ln_g, ln_b, alpha):
    m = x.shape[0]
    n_exp = router.shape[1]
    tm = _tile(m, MOE_ROWS)
    idx, wgt = _router_top2(x, router)
    dest, tile_exp, n_valid, fill_lo, fill_hi = _moe_plan(idx, m, tm, n_exp)
    xs = _moe_permute(x, dest, fill_lo, fill_hi, 2 * m + n_exp * tm)
    ys = _moe_experts(xs, tile_exp, n_valid, tm, w_gate, w_up, w_down)
    return _moe_combine_deepnorm(x, ys, dest, wgt, ln_g, ln_b, alpha)


def _proj_pairs_kernel(transposed, x_ref, w_ref, o_ref, xb_s):
    @pl.when(pl.program_id(1) == 0)
    def _():
        xb_s[...] = x_ref[...].astype(BF16)

    n_slab = o_ref.shape[0]
    if transposed:
        res = _dot_nt(w_ref[...], xb_s[...])
        for q in range(n_slab):
            o_ref[q] = res[q * LANES:(q + 1) * LANES, :].astype(o_ref.dtype)
    else:
        res = _dot(xb_s[...], w_ref[...])
        for q in range(n_slab):
            o_ref[q] = res[:, q * LANES:(q + 1) * LANES].astype(o_ref.dtype)


def _proj_pairs(x, w, transposed, out_dtype=F32):
    m, d = x.shape
    n = w.shape[1]
    tm = _tile(m, 512)
    tn = _tile(n, 1024)
    n_slab = tn // LANES
    if transposed:
        wb = w.T.astype(BF16)
        w_spec = pl.BlockSpec((tn, d), lambda i, j: (j, 0))
        o_spec = pl.BlockSpec((n_slab, LANES, tm), lambda i, j: (j, 0, i))
        o_shape = jax.ShapeDtypeStruct((n // LANES, LANES, m), out_dtype)
    else:
        wb = w.astype(BF16)
        w_spec = pl.BlockSpec((d, tn), lambda i, j: (0, j))
        o_spec = pl.BlockSpec((n_slab, tm, LANES), lambda i, j: (j, i, 0))
        o_shape = jax.ShapeDtypeStruct((n // LANES, m, LANES), out_dtype)
    return pl.pallas_call(
        functools.partial(_proj_pairs_kernel, transposed),
        grid=(m // tm, n // tn),
        in_specs=[pl.BlockSpec((tm, d), lambda i, j: (i, 0)), w_spec],
        out_specs=o_spec,
        out_shape=o_shape,
        scratch_shapes=[pltpu.VMEM((tm, d), BF16)],
        compiler_params=_cparams(("parallel", "arbitrary")),
        name="proj_pairs_t" if transposed else "proj_pairs",
    )(x, wb)


def _block_mean_kernel(k_ref, o_ref):
    k = k_ref[0]
    nb = k.shape[0] // MOBA_BLOCK
    o_ref[0] = jnp.mean(k.reshape(nb, MOBA_BLOCK, LANES), axis=1)


def _block_means(k_pairs, batch):
    n_pair, m, _ = k_pairs.shape
    seq = m // batch
    nb = seq // MOBA_BLOCK
    return pl.pallas_call(
        _block_mean_kernel,
        grid=(n_pair, batch),
        in_specs=[pl.BlockSpec((1, seq, LANES), lambda p, b: (p, b, 0))],
        out_specs=pl.BlockSpec((1, nb, LANES), lambda p, b: (p, b, 0)),
        out_shape=jax.ShapeDtypeStruct((n_pair, batch * nb, LANES), F32),
        compiler_params=_cparams(("parallel", "parallel")),
        name="moba_block_means",
    )(k_pairs)


def _moba_kernel(scale, qt_ref, k_ref, vt_ref, km_ref, o_ref, neg_s, sca_s, scb_s, m_s, acc_s):
    own = pl.program_id(2)
    blk = MOBA_BLOCK
    n_slab = qt_ref.shape[0]
    nb = km_ref.shape[1]
    row = lax.broadcasted_iota(jnp.int32, (LANES, blk), 0)
    zero_q = jnp.zeros((LANES, blk), F32)
    n_iota = lax.broadcasted_iota(jnp.int32, (nb, blk), 0).astype(F32)
    past = n_iota < own.astype(F32)
    neg_inf = jnp.float32(-jnp.inf)

    streams = [(g, h) for g in range(n_slab) for h in range(2)]
    qh = []
    for s, (g, h) in enumerate(streams):
        in_head = (row < HEAD_DIM) if h == 0 else (row >= HEAD_DIM)
        q_h = jnp.where(in_head, qt_ref[g], zero_q)
        qh.append((q_h * (scale * LOG2E)).astype(BF16))
        gate = _dot_sp(_split(km_ref[g]), _split(q_h))
        gate = jnp.where(past, gate, neg_inf)
        neg = jnp.full((nb, blk), NEG_BIG, F32)
        for _ in range(min(MOBA_TOPK, nb)):
            mx = jnp.max(gate, axis=0, keepdims=True)
            idx = jnp.min(jnp.where(gate == mx, n_iota, float(nb)), axis=0, keepdims=True)
            pick = n_iota == idx
            neg = jnp.where(jnp.logical_and(pick, past), 0.0, neg)
            gate = jnp.where(pick, neg_inf, gate)
        neg_s[s] = neg

    ones_rows = jnp.ones((DEN_ROWS, blk), BF16)

    def block_scores(n, s):
        start = pl.multiple_of(n * blk, blk)
        return _dot(k_ref[streams[s][0], pl.ds(start, blk), :], qh[s])

    def absorb(n, st, bias, s):
        g, h = streams[s]
        start = pl.multiple_of(n * blk, blk)
        mx = m_s[s]
        mblk = jnp.max(st, axis=0, keepdims=True)
        if bias is not None:
            mblk = mblk + bias
        mx_new = jnp.maximum(mx, mblk)
        alpha = jnp.exp2(mx - mx_new)
        shift = mx_new if bias is None else mx_new - bias
        p = jnp.exp2(st - shift).astype(BF16)
        vtb = vt_ref[g, h * HEAD_DIM:(h + 1) * HEAD_DIM, pl.ds(start, blk)]
        m_s[s] = mx_new
        acc_s[s] = alpha * acc_s[s] + _dot(jnp.concatenate([vtb, ones_rows], axis=0), p)

    kpos = lax.broadcasted_iota(jnp.int32, (blk, blk), 0)
    qpos = lax.broadcasted_iota(jnp.int32, (blk, blk), 1)
    causal = kpos <= qpos
    own_scores = [block_scores(own, s) for s in range(len(streams))]
    for s in range(len(streams)):
        sca_s[s] = block_scores(0, s)
        m_s[s] = jnp.full((1, blk), neg_inf, F32)
        acc_s[s] = jnp.zeros((HEAD_DIM + DEN_ROWS, blk), F32)
    for s in range(len(streams)):
        absorb(own, jnp.where(causal, own_scores[s], neg_inf), None, s)

    def body(j, carry):
        n0 = 2 * j
        n1 = n0 + 1
        n2 = jnp.minimum(n0 + 2, nb - 1)
        for s in range(len(streams)):
            scb_s[s] = block_scores(n1, s)
            absorb(n0, sca_s[s], neg_s[s, pl.ds(n0, 1), :], s)
        for s in range(len(streams)):
            sca_s[s] = block_scores(n2, s)
            absorb(n1, scb_s[s], neg_s[s, pl.ds(n1, 1), :], s)
        return carry

    lax.fori_loop(0, (own + 1) // 2, body, 0)
    for g in range(n_slab):
        a0, a1 = acc_s[2 * g], acc_s[2 * g + 1]
        o_t = jnp.concatenate([a0[:HEAD_DIM] / a0[HEAD_DIM:HEAD_DIM + 1],
                               a1[:HEAD_DIM] / a1[HEAD_DIM:HEAD_DIM + 1]], axis=0)
        o_ref[g] = o_t.T.astype(o_ref.dtype)


def _moba_attention(q_t, k, v_t, k_means, batch):
    n_pair, _, m = q_t.shape
    seq = m // batch
    nb = seq // MOBA_BLOCK
    g = _tile(n_pair, MOBA_SLABS)
    return pl.pallas_call(
        functools.partial(_moba_kernel, HEAD_DIM ** -0.5),
        grid=(n_pair // g, batch, nb),
        in_specs=[pl.BlockSpec((g, LANES, MOBA_BLOCK), lambda p, b, i: (p, 0, b * nb + i)),
                  pl.BlockSpec((g, seq, LANES), lambda p, b, i: (p, b, 0),
                               pipeline_mode=pl.Buffered(1)),
                  pl.BlockSpec((g, LANES, seq), lambda p, b, i: (p, 0, b),
                               pipeline_mode=pl.Buffered(1)),
                  pl.BlockSpec((g, nb, LANES), lambda p, b, i: (p, b, 0))],
        out_specs=pl.BlockSpec((g, MOBA_BLOCK, LANES), lambda p, b, i: (p, b * nb + i, 0)),
        out_shape=jax.ShapeDtypeStruct((n_pair, m, LANES), BF16),
        scratch_shapes=[pltpu.VMEM((2 * g, nb, MOBA_BLOCK), F32),
                        pltpu.VMEM((2 * g, MOBA_BLOCK, MOBA_BLOCK), F32),
                        pltpu.VMEM((2 * g, MOBA_BLOCK, MOBA_BLOCK), F32),
                        pltpu.VMEM((2 * g, 1, MOBA_BLOCK), F32),
                        pltpu.VMEM((2 * g, HEAD_DIM + DEN_ROWS, MOBA_BLOCK), F32)],
        compiler_params=_cparams(("parallel", "parallel", "arbitrary")),
        name="moba_attention",
    )(q_t, k, v_t, k_means)


def kernel(x, rwkv_mu, rwkv_w_rkv, rwkv_w_out, rwkv_decay_w0, rwkv_decay_w1, rwkv_decay_w2, rwkv_iclr_a0, rwkv_iclr_a1, rwkv_iclr_a2, rwkv_vres_v0, rwkv_vres_v1, rwkv_vres_v2, rwkv_gate_g1, rwkv_gate_g2, rwkv_k_k, rwkv_k_a, rwkv_r_k, rwkv_gn_g, rwkv_gn_b, moba_w_k, moba_w_v, moba_w_q, moba_w_o, ffn_w_gate, ffn_w_up, ffn_w_down, moe_router, moe_w_gate, moe_w_up, moe_w_down, ln_g, ln_b):
    batch, seq, d = x.shape
    assert d % (2 * LANES) == 0 and seq % MOBA_BLOCK == 0 and seq % WKV_CHUNK == 0
    depth = ln_g.shape[0]
    n_rwkv = rwkv_mu.shape[0]
    alpha = (2.0 * depth) ** 0.25
    h = x.reshape(batch * seq, d)
    v_first = None
    kv = None
    for layer in range(depth):
        if layer < n_rwkv:
            i = layer
            vres = None if i == 0 else (rwkv_vres_v0[i - 1], rwkv_vres_v1[i - 1], rwkv_vres_v2[i - 1])
            r, e, k, v, a, b, g = _rwkv_proj(
                h, seq, rwkv_mu[i], rwkv_w_rkv[i], rwkv_decay_w1[i], rwkv_iclr_a1[i],
                rwkv_gate_g1[i], rwkv_decay_w2[i], rwkv_iclr_a2[i], rwkv_gate_g2[i],
                rwkv_decay_w0[i], rwkv_iclr_a0[i], rwkv_k_k[i], rwkv_k_a[i],
                vres=vres, v_first=v_first)
            if i == 0:
                v_first = v
            y = _wkv_scan(r, e, k, v, a, b, batch)
            mix = _rwkv_post(y, r, k, v, g, rwkv_r_k[i], rwkv_gn_g[i], rwkv_gn_b[i])
            w_out = rwkv_w_out[i]
        else:
            jdx = layer - n_rwkv
            k_pairs, v_t, k_means = kv
            q_t = _proj_pairs(h, moba_w_q[jdx], transposed=True)
            mix = _moba_attention(q_t, k_pairs, v_t, k_means, batch)
            w_out = moba_w_o[jdx]
        h = _out_proj_deepnorm(mix, w_out, h, ln_g[layer, 0], ln_b[layer, 0], alpha)
        ex = layer // 2
        if layer % 2 == 0:
            h = _ffn_deepnorm(h, _layer_bf16(ffn_w_gate, ex), _layer_bf16(ffn_w_up, ex),
                              _layer_bf16(ffn_w_down, ex), ln_g[layer, 1], ln_b[layer, 1], alpha)
        else:
            h = _moe_deepnorm(h, moe_router[ex], _layer_bf16(moe_w_gate, ex),
                              _layer_bf16(moe_w_up, ex), _layer_bf16(moe_w_down, ex),
                              ln_g[layer, 1], ln_b[layer, 1], alpha)
        if layer == n_rwkv - 1:
            k_pairs = _proj_pairs(h, moba_w_k, transposed=False)
            v_t = _proj_pairs(h, moba_w_v, transposed=True, out_dtype=BF16)
            kv = (k_pairs.astype(BF16), v_t, _block_means(k_pairs, batch))
    return h.reshape(batch, seq, d)
```

```python
import functools
import math

import jax
import jax.numpy as jnp
from jax import lax
from jax.experimental import pallas as pl
from jax.experimental.pallas import tpu as pltpu

HEAD_DIM = 64
LANES = 128
MXU_DEPTH = 256
GN_EPS = 64e-5
LN_EPS = 1e-5
MOBA_BLOCK = 256
MOBA_TOPK = 3
MOE_ROWS = 512
FFN_TILE = 2816
CAST_BLOCK_ELEMS = 1 << 20
WKV_CHUNK = 64
WKV_SLABS = 8
NEG_BIG = -1e30
MOBA_SLABS = 8
DEN_ROWS = 16
LOG2E = 1.4426950408889634

F32 = jnp.float32
BF16 = jnp.bfloat16
VMEM_LIMIT = 56 * 1024 * 1024


def _cparams(sem):
    return pltpu.CompilerParams(dimension_semantics=sem, vmem_limit_bytes=VMEM_LIMIT)


def _dot(a, b):
    return jnp.dot(a, b, preferred_element_type=F32)


def _dot_nt(a, b):
    return lax.dot_general(a, b, (((1,), (1,)), ((), ())), preferred_element_type=F32)


def _split(x):
    hi = x.astype(BF16)
    lo = (x - hi.astype(F32)).astype(BF16)
    return hi, lo


def _dot_sp(a, b, nt=False):
    d = _dot_nt if nt else _dot
    return d(a[0], b[0]) + (d(a[0], b[1]) + d(a[1], b[0]))


def _dot_exact_rhs(a, b_exact, nt=False):
    d = _dot_nt if nt else _dot
    hi, lo = _split(a)
    return d(hi, b_exact) + d(lo, b_exact)


def _sigmoid(x):
    return 1.0 / (1.0 + jnp.exp(-x))


def _layer_norm(y, g, b):
    mu = jnp.mean(y, axis=-1, keepdims=True)
    yc = y - mu
    var = jnp.mean(yc * yc, axis=-1, keepdims=True)
    return yc * lax.rsqrt(var + LN_EPS) * g + b


def _head_blockdiag(n):
    i = jnp.arange(n) // HEAD_DIM
    return (i[:, None] == i[None, :]).astype(BF16)


def _tile(n, want):
    t = min(n, want)
    assert n % t == 0, (n, want)
    return t


def _cast_kernel(w_ref, o_ref):
    o_ref[...] = w_ref[...].astype(o_ref.dtype)


def _layer_bf16(w, layer):
    shape = w.shape[1:]
    c = shape[-1]
    rows = math.prod(shape[:-1])
    tr = rows
    for cand in range(min(rows, CAST_BLOCK_ELEMS // c) // 16, 0, -1):
        if rows % (cand * 16) == 0:
            tr = cand * 16
            break
    out = pl.pallas_call(
        _cast_kernel,
        grid=(rows // tr,),
        in_specs=[pl.BlockSpec((1, tr, c), lambda i: (layer, i, 0))],
        out_specs=pl.BlockSpec((1, tr, c), lambda i: (0, i, 0)),
        out_shape=jax.ShapeDtypeStruct((1, rows, c), BF16),
        compiler_params=_cparams(("parallel",)),
        name="weight_to_bf16",
    )(w.reshape(w.shape[0], rows, c))
    return out.reshape(shape)


def _rwkv_proj_kernel(has_vres, steps_per_seq, *refs):
    if has_vres:
        (x_ref, xp_ref, mu_ref, wrkv_ref, wd1_ref, wa1_ref, wg1_ref, wv1_ref,
         wd2_ref, wa2_ref, wg2_ref, wv2_ref, w0_ref, a0_ref, v0_ref, kk_ref, ka_ref,
         bd_ref, vf_ref,
         r_out, e_out, k_out, v_out, a_out, b_out, g_out,
         xm_s, hd_s, ha_s, hg_s, hv_s) = refs
    else:
        (x_ref, xp_ref, mu_ref, wrkv_ref, wd1_ref, wa1_ref, wg1_ref,
         wd2_ref, wa2_ref, wg2_ref, w0_ref, a0_ref, kk_ref, ka_ref,
         bd_ref,
         r_out, e_out, k_out, v_out, a_out, b_out, g_out,
         xm_s, hd_s, ha_s, hg_s) = refs
    i = pl.program_id(0)
    j = pl.program_id(1)

    @pl.when(j == 0)
    def _():
        x = x_ref[...]
        tm = x.shape[0]
        prev_row = jnp.where(i % steps_per_seq == 0, 0.0, xp_ref[7:8, :])
        rolled = pltpu.roll(x, 1, axis=0)
        row = lax.broadcasted_iota(jnp.int32, (tm, 1), 0)
        x_prev = jnp.where(row == 0, prev_row, rolled)
        xx = x_prev - x
        for c in range(3):
            xm_s[c] = (x + xx * mu_ref[c:c + 1, :]).astype(BF16)
        xw = (x + xx * mu_ref[3:4, :]).astype(BF16)
        xa = (x + xx * mu_ref[4:5, :]).astype(BF16)
        xg = (x + xx * mu_ref[5:6, :]).astype(BF16)
        hd_s[...] = jnp.tanh(_dot(xw, wd1_ref[...])).astype(BF16)
        ha_s[...] = _dot(xa, wa1_ref[...]).astype(BF16)
        hg_s[...] = _sigmoid(_dot(xg, wg1_ref[...])).astype(BF16)
        if has_vres:
            hv_s[...] = _dot(xm_s[2], wv1_ref[...]).astype(BF16)

    r = _dot(xm_s[0], wrkv_ref[0])
    k = _dot(xm_s[1], wrkv_ref[1])
    v = _dot(xm_s[2], wrkv_ref[2])
    z = w0_ref[...] + _dot(hd_s[...], wd2_ref[...])
    nz = -z
    softplus = jnp.maximum(nz, 0.0) + jnp.log(1.0 + jnp.exp(-jnp.abs(nz)))
    e = jnp.exp(-softplus - 0.5)
    a = _sigmoid(a0_ref[...] + _dot(ha_s[...], wa2_ref[...]))
    g = _dot(hg_s[...], wg2_ref[...])
    n_slab = r.shape[1] // LANES
    if has_vres:
        vf = jnp.concatenate([vf_ref[q] for q in range(n_slab)], axis=1)
        v = v + (vf - v) * _sigmoid(v0_ref[...] + _dot(hv_s[...], wv2_ref[...]))
    kk = k * kk_ref[...]
    ss = _dot_exact_rhs(kk * kk, bd_ref[...])
    kk = kk * lax.rsqrt(jnp.maximum(ss, 1e-24))
    k = k * (1.0 + (a - 1.0) * ka_ref[...])
    for q in range(n_slab):
        sl = slice(q * LANES, (q + 1) * LANES)
        r_out[q] = r[:, sl].astype(r_out.dtype)
        e_out[q] = e[:, sl]
        k_out[q] = k[:, sl].astype(k_out.dtype)
        v_out[q] = v[:, sl].astype(v_out.dtype)
        a_out[q] = (-kk)[:, sl].astype(a_out.dtype)
        b_out[q] = (kk * a)[:, sl].astype(b_out.dtype)
        g_out[q] = g[:, sl].astype(g_out.dtype)


def _rwkv_proj(x, seq_len, mu, w_rkv, wd1, wa1, wg1, wd2, wa2, wg2, w0, a0, k_k, k_a,
               vres=None, v_first=None):
    m, d = x.shape
    tm = _tile(seq_len, 512)
    tn = _tile(d, 256)
    n_slab = tn // LANES
    has_vres = vres is not None
    row = lambda a: a.reshape(1, d)
    full = lambda a: pl.BlockSpec(a.shape, lambda i, j: (0,) * a.ndim)
    colblk = lambda rows: pl.BlockSpec((rows, tn), lambda i, j: (0, j))
    bd = _head_blockdiag(tn)
    wd1, wa1, wg1 = wd1.astype(BF16), wa1.astype(BF16), wg1.astype(BF16)
    wd2, wa2, wg2 = wd2.astype(BF16), wa2.astype(BF16), wg2.astype(BF16)
    w_rkv = w_rkv.astype(BF16)
    args = [x, x, mu, w_rkv, wd1, wa1, wg1]
    specs = [pl.BlockSpec((tm, d), lambda i, j: (i, 0)),
             pl.BlockSpec((8, d), lambda i, j: (jnp.maximum(i * (tm // 8) - 1, 0), 0)),
             full(mu),
             pl.BlockSpec((3, d, tn), lambda i, j: (0, 0, j)),
             full(wd1), full(wa1), full(wg1)]
    if has_vres:
        v0, wv1, wv2 = vres
        wv1, wv2 = wv1.astype(BF16), wv2.astype(BF16)
        args += [wv1]
        specs += [full(wv1)]
    args += [wd2, wa2, wg2]
    specs += [colblk(wd2.shape[0]), colblk(wa2.shape[0]), colblk(wg2.shape[0])]
    if has_vres:
        args += [wv2]
        specs += [colblk(wv2.shape[0])]
    args += [row(w0), row(a0)]
    specs += [colblk(1), colblk(1)]
    if has_vres:
        args += [row(v0)]
        specs += [colblk(1)]
    args += [row(k_k), row(k_a), bd]
    specs += [colblk(1), colblk(1), full(bd)]
    slab_spec = pl.BlockSpec((n_slab, tm, LANES), lambda i, j: (j, i, 0))
    if has_vres:
        args += [v_first]
        specs += [slab_spec]
    slab = lambda dt: jax.ShapeDtypeStruct((d // LANES, m, LANES), dt)
    scratch = [pltpu.VMEM((3, tm, d), BF16),
               pltpu.VMEM((tm, wd1.shape[1]), BF16),
               pltpu.VMEM((tm, wa1.shape[1]), BF16),
               pltpu.VMEM((tm, wg1.shape[1]), BF16)]
    if has_vres:
        scratch.append(pltpu.VMEM((tm, wv1.shape[1]), BF16))
    return pl.pallas_call(
        functools.partial(_rwkv_proj_kernel, has_vres, seq_len // tm),
        grid=(m // tm, d // tn),
        in_specs=specs,
        out_specs=[slab_spec] * 7,
        out_shape=[slab(BF16), slab(F32)] + [slab(BF16)] * 5,
        scratch_shapes=scratch,
        compiler_params=_cparams(("parallel", "arbitrary")),
        name="rwkv_proj",
    )(*args)


def _wkv_kernel(r_ref, e_ref, k_ref, v_ref, a_ref, b_ref, y_ref, s_ref):
    c = WKV_CHUNK
    n_slab, n_batch, tb, _ = r_ref.shape
    n_chunks = tb // c

    @pl.when(pl.program_id(1) == 0)
    def _():
        s_ref[...] = jnp.zeros_like(s_ref)

    lane = lax.broadcasted_iota(jnp.int32, (c, LANES), 1)
    t_idx = lax.broadcasted_iota(jnp.int32, (c, LANES), 0)
    head0 = lane < HEAD_DIM
    s_idx = jnp.bitwise_and(lane, HEAD_DIM - 1)
    strict = s_idx < t_idx
    incl = s_idx <= t_idx
    rr = lax.broadcasted_iota(jnp.int32, (LANES, LANES), 0)
    cc = lax.broadcasted_iota(jnp.int32, (LANES, LANES), 1)
    same_head = (rr < HEAD_DIM) == (cc < HEAD_DIM)

    def bd(x):
        z = jnp.zeros_like(x)
        return jnp.concatenate([jnp.where(head0, x, z), jnp.where(head0, z, x)], axis=0)

    def cat(a, b, axis=0):
        return jnp.concatenate([a, b], axis=axis)

    def chunk(ci, carry):
        stages = [one_slab(ci, p, bb) for p in range(n_slab) for bb in range(n_batch)]
        while stages:
            stages = [g for g in stages if next(g, "done") != "done"]
        return carry

    def one_slab(ci, p, bb):
        sl = pl.ds(pl.multiple_of(ci * c, c), c)
        r = r_ref[p, bb, sl, :]
        e = e_ref[p, bb, sl, :]
        k = k_ref[p, bb, sl, :]
        v = v_ref[p, bb, sl, :]
        a = a_ref[p, bb, sl, :]
        b = b_ref[p, bb, sl, :]
        s0 = s_ref[p * n_batch + bb]
        cum = _cumsum_rows(e, t_idx)
        yield
        tot = cum[c - 1:c, :]
        ar = cat(a * jnp.exp(e - cum), r * jnp.exp(-cum)).astype(BF16)
        ec = jnp.exp(cum)
        bt = (b * ec).astype(BF16)
        kt = (k * ec).astype(BF16)
        eh = jnp.exp(cum - tot)
        bk = cat(b * eh, k * eh).astype(BF16)
        bdv = bd(v.astype(BF16))
        gram = _dot_nt(ar, cat(bd(bt), bd(kt)))
        yield
        zero = jnp.zeros((c, LANES), F32)
        l_ab = jnp.where(strict, gram[:c, :LANES], zero)
        n_ak = jnp.where(strict, gram[:c, LANES:], zero)
        m_rb = jnp.where(incl, gram[c:, :LANES], zero)
        m_rk = jnp.where(incl, gram[c:, LANES:], zero)
        xy = _dot_nt(ar, s0.astype(BF16)) + _dot(cat(n_ak, m_rk).astype(BF16), bdv)
        x = xy[:c]
        y0 = xy[c:]
        yield
        lp = l_ab.astype(BF16)
        n_steps = int(math.log2(c))
        for step in range(n_steps):
            bdx = bd(x.astype(BF16))
            if step + 1 < n_steps:
                t = _dot(lp, cat(bdx, bd(lp), axis=1))
                x = x + t[:, :LANES]
                lp = t[:, LANES:].astype(BF16)
            else:
                x = x + _dot(lp, bdx)
            yield
        y_ref[p, bb, sl, :] = y0 + _dot(m_rb.astype(BF16), bd(x.astype(BF16)))
        uv_t = cat(x, v.astype(F32)).T.astype(BF16)
        upd = _dot(uv_t, bk)
        s_ref[p * n_batch + bb] = (s0 * jnp.exp(-tot)
                                   + jnp.where(same_head, upd, jnp.zeros_like(upd)))
        yield

    lax.fori_loop(0, n_chunks, chunk, 0)


def _cumsum_rows(e, row_idx):
    cum = e
    shift = 1
    while shift < e.shape[0]:
        cum = cum + jnp.where(row_idx >= shift, pltpu.roll(cum, shift, axis=0), 0.0)
        shift *= 2
    return cum


def _wkv_scan(r, e, k, v, a, b, batch):
    n_pair, m, _ = r.shape
    seq = m // batch
    tb = _tile(seq, 256)
    n_slab = _tile(n_pair, WKV_SLABS)
    view = lambda t: t.reshape(n_pair, batch, seq, LANES)
    spec = pl.BlockSpec((n_slab, batch, tb, LANES), lambda p, t: (p, 0, t, 0))
    y = pl.pallas_call(
        _wkv_kernel,
        grid=(n_pair // n_slab, seq // tb),
        in_specs=[spec] * 6,
        out_specs=spec,
        out_shape=jax.ShapeDtypeStruct((n_pair, batch, seq, LANES), F32),
        scratch_shapes=[pltpu.VMEM((n_slab * batch, LANES, LANES), F32)],
        compiler_params=_cparams(("parallel", "arbitrary")),
        name="wkv_scan",
    )(view(r), view(e), view(k), view(v), view(a), view(b))
    return y.reshape(n_pair, m, LANES)


def _rwkv_post_kernel(y_ref, r_ref, k_ref, v_ref, g_ref, rk_ref, gg_ref, gb_ref, bd_ref, z_ref):
    y = y_ref[0]
    bd2 = bd_ref[...]
    bdm = bd2[:LANES, :LANES]
    inv_n = 1.0 / HEAD_DIM
    rk = r_ref[0].astype(F32) * k_ref[0].astype(F32) * rk_ref[0]
    sums = _dot_exact_rhs(jnp.concatenate([y, rk], axis=1), bd2)
    yc = y - sums[:, :LANES] * inv_n
    var = _dot_exact_rhs(yc * yc, bdm) * inv_n
    yn = yc * lax.rsqrt(var + GN_EPS) * gg_ref[0] + gb_ref[0]
    bonus = sums[:, LANES:] * v_ref[0].astype(F32)
    z_ref[0] = ((yn + bonus) * g_ref[0].astype(F32)).astype(z_ref.dtype)


def _rwkv_post(y, r, k, v, g, r_k, gn_g, gn_b):
    n_pair, m, _ = y.shape
    tm = _tile(m, 2048)
    spec = pl.BlockSpec((1, tm, LANES), lambda p, i: (p, i, 0))
    pspec = pl.BlockSpec((1, 1, LANES), lambda p, i: (p, 0, 0))
    bd = _head_blockdiag(2 * LANES)
    slab = lambda a: a.reshape(n_pair, 1, LANES)
    return pl.pallas_call(
        _rwkv_post_kernel,
        grid=(n_pair, m // tm),
        in_specs=[spec] * 5 + [pspec] * 3 + [pl.BlockSpec(bd.shape, lambda p, i: (0, 0))],
        out_specs=spec,
        out_shape=jax.ShapeDtypeStruct((n_pair, m, LANES), BF16),
        compiler_params=_cparams(("parallel", "parallel")),
        name="rwkv_post",
    )(y, r, k, v, g, slab(r_k), slab(gn_g), slab(gn_b), bd)


def _out_proj_kernel(alpha, z_ref, w_ref, x_ref, g_ref, b_ref, o_ref):
    per = w_ref.shape[1] // LANES
    acc = None
    for c in range(w_ref.shape[0]):
        zc = jnp.concatenate([z_ref[c * per + q] for q in range(per)], axis=1)
        part = _dot(zc, w_ref[c])
        acc = part if acc is None else acc + part
    o_ref[...] = _layer_norm(alpha * x_ref[...] + acc, g_ref[...], b_ref[...])


def _out_proj_deepnorm(z, w, x, ln_g, ln_b, alpha):
    n_pair, m, _ = z.shape
    d = w.shape[1]
    tm = _tile(m, 512)
    kc = _tile(d, MXU_DEPTH)
    w3 = w.astype(BF16).reshape(d // kc, kc, d)
    return pl.pallas_call(
        functools.partial(_out_proj_kernel, alpha),
        grid=(m // tm,),
        in_specs=[pl.BlockSpec((n_pair, tm, LANES), lambda i: (0, i, 0)),
                  pl.BlockSpec(w3.shape, lambda i: (0, 0, 0)),
                  pl.BlockSpec((tm, d), lambda i: (i, 0)),
                  pl.BlockSpec((1, d), lambda i: (0, 0)),
                  pl.BlockSpec((1, d), lambda i: (0, 0))],
        out_specs=pl.BlockSpec((tm, d), lambda i: (i, 0)),
        out_shape=jax.ShapeDtypeStruct((m, d), F32),
        compiler_params=_cparams(("parallel",)),
        name="out_proj_deepnorm",
    )(z, w3, x, ln_g.reshape(1, d), ln_b.reshape(1, d))


def _ff_tile(ff, want):
    for cand in range(min(ff, want) // LANES, 0, -1):
        if ff % (cand * LANES) == 0:
            return cand * LANES
    return ff


def _swiglu_partial(xb, wg, wu, wd):
    h1 = _dot(xb, wg)
    h2 = _dot(xb, wu)
    act = h1 * _sigmoid(h1) * h2
    return _dot(act.astype(BF16), wd)


def _ffn_kernel(alpha, x_ref, wg_ref, wu_ref, wd_ref, g_ref, b_ref, o_ref, xb_s, acc_s):
    f = pl.program_id(1)

    @pl.when(f == 0)
    def _():
        xb_s[...] = x_ref[...].astype(BF16)
        acc_s[...] = jnp.zeros_like(acc_s)

    acc_s[...] += _swiglu_partial(xb_s[...], wg_ref[...], wu_ref[...], wd_ref[...])

    @pl.when(f == pl.num_programs(1) - 1)
    def _():
        o_ref[...] = _layer_norm(alpha * x_ref[...] + acc_s[...], g_ref[...], b_ref[...])


def _ffn_deepnorm(x, w_gate, w_up, w_down, ln_g, ln_b, alpha):
    m, d = x.shape
    ff = w_gate.shape[1]
    tm = _tile(m, 512)
    tf = _ff_tile(ff, FFN_TILE)
    wmode = dict(pipeline_mode=pl.Buffered(1)) if tf == ff else {}
    return pl.pallas_call(
        functools.partial(_ffn_kernel, alpha),
        grid=(m // tm, ff // tf),
        in_specs=[pl.BlockSpec((tm, d), lambda i, f: (i, 0)),
                  pl.BlockSpec((d, tf), lambda i, f: (0, f), **wmode),
                  pl.BlockSpec((d, tf), lambda i, f: (0, f), **wmode),
                  pl.BlockSpec((tf, d), lambda i, f: (f, 0), **wmode),
                  pl.BlockSpec((1, d), lambda i, f: (0, 0)),
                  pl.BlockSpec((1, d), lambda i, f: (0, 0))],
        out_specs=pl.BlockSpec((tm, d), lambda i, f: (i, 0)),
        out_shape=jax.ShapeDtypeStruct((m, d), F32),
        scratch_shapes=[pltpu.VMEM((tm, d), BF16), pltpu.VMEM((tm, d), F32)],
        compiler_params=_cparams(("parallel", "arbitrary")),
        name="ffn_deepnorm",
    )(x, w_gate.astype(BF16), w_up.astype(BF16), w_down.astype(BF16),
      ln_g.reshape(1, d), ln_b.reshape(1, d))


def _router_kernel(n_exp, x_ref, w_ref, idx_ref, wgt_ref):
    logits = _dot_sp(_split(x_ref[...]), _split(w_ref[...]))
    lane = lax.broadcasted_iota(jnp.int32, logits.shape, 1).astype(F32)
    neg_inf = jnp.float32(-jnp.inf)
    lg = jnp.where(lane < n_exp, logits, neg_inf)
    m1 = jnp.max(lg, axis=1, keepdims=True)
    i1 = jnp.min(jnp.where(lg == m1, lane, float(LANES)), axis=1, keepdims=True)
    lg2 = jnp.where(lane == i1, neg_inf, lg)
    m2 = jnp.max(lg2, axis=1, keepdims=True)
    i2 = jnp.min(jnp.where(lg2 == m2, lane, float(LANES)), axis=1, keepdims=True)
    e2 = jnp.exp(m2 - m1)
    den = 1.0 + e2
    idx_ref[...] = jnp.where(lane == 0.0, i1, jnp.where(lane == 1.0, i2, 0.0)).astype(jnp.int32)
    wgt_ref[...] = jnp.where(lane == 0.0, 1.0 / den, jnp.where(lane == 1.0, e2 / den, 0.0))


def _router_top2(x, router):
    m, d = x.shape
    n_exp = router.shape[1]
    tm = _tile(m, 512)
    w = jnp.pad(router, ((0, 0), (0, LANES - n_exp)))
    out_spec = pl.BlockSpec((tm, LANES), lambda i: (i, 0))
    return pl.pallas_call(
        functools.partial(_router_kernel, n_exp),
        grid=(m // tm,),
        in_specs=[pl.BlockSpec((tm, d), lambda i: (i, 0)),
                  pl.BlockSpec((d, LANES), lambda i: (0, 0))],
        out_specs=[out_spec, out_spec],
        out_shape=[jax.ShapeDtypeStruct((m, LANES), jnp.int32),
                   jax.ShapeDtypeStruct((m, LANES), F32)],
        compiler_params=_cparams(("parallel",)),
        name="moe_router",
    )(x, w)


def _moe_plan(idx, m, tm, n_exp):
    n_ent = 2 * m
    expert = jnp.concatenate([idx[:, 0], idx[:, 1]])
    onehot = (expert[:, None] == jnp.arange(n_exp, dtype=jnp.int32)[None, :]).astype(jnp.int32)
    csum = jnp.cumsum(onehot, axis=0)
    counts = csum[-1]
    rank = jnp.sum(csum * onehot, axis=1) - 1
    padded = ((counts + tm - 1) // tm) * tm
    ends = jnp.cumsum(padded)
    starts = ends - padded
    dest = jnp.sum(starts[None, :] * onehot, axis=1) + rank
    n_tiles = n_ent // tm + n_exp
    tile_start = jnp.arange(n_tiles, dtype=jnp.int32) * tm
    tile_exp = jnp.minimum(jnp.sum((tile_start[:, None] >= ends[None, :]).astype(jnp.int32), axis=1),
                           n_exp - 1)
    n_valid = jnp.clip(starts[tile_exp] + counts[tile_exp] - tile_start, 0, tm)
    fill_lo = jnp.concatenate([starts + counts, ends[-1:]])
    fill_hi = jnp.concatenate([ends, jnp.full((1,), n_tiles * tm, jnp.int32)])
    return dest, tile_exp, n_valid, fill_lo, fill_hi


def _moe_permute_kernel(lo_ref, hi_ref, dest_ref, x_ref, xs_hbm, sem):
    i = pl.program_id(0)
    te = dest_ref.shape[2]

    def entry_row(r):
        return pltpu.make_async_copy(x_ref.at[pl.ds(r, 1), :],
                                     xs_hbm.at[pl.ds(dest_ref[0, 0, r], 1), :], sem.at[0])

    def filler_row(r):
        return pltpu.make_async_copy(x_ref.at[pl.ds(0, 1), :], xs_hbm.at[pl.ds(r, 1), :], sem.at[1])

    def each(lo, hi, fn, unroll=1):
        def body(r, c):
            fn(r)
            return c
        lax.fori_loop(lo, hi, body, 0, unroll=unroll)

    each(0, te, lambda r: entry_row(r).start(), unroll=8)

    @pl.when(i == 0)
    def _():
        for e in range(lo_ref.shape[0]):
            each(lo_ref[e], hi_ref[e], lambda r: filler_row(r).start())
        for e in range(lo_ref.shape[0]):
            each(lo_ref[e], hi_ref[e], lambda r: filler_row(r).wait())

    each(0, te, lambda r: entry_row(r).wait(), unroll=8)


def _moe_permute(x, dest, fill_lo, fill_hi, n_rows):
    m, d = x.shape
    te = _tile(m, 512)
    n_x = m // te
    dest3 = dest.reshape(-1, 1, te)
    grid_spec = pltpu.PrefetchScalarGridSpec(
        num_scalar_prefetch=2,
        grid=(dest3.shape[0],),
        in_specs=[pl.BlockSpec((1, 1, te), lambda i, lo, hi: (i, 0, 0), memory_space=pltpu.SMEM),
                  pl.BlockSpec((te, d), lambda i, lo, hi: (i % n_x, 0))],
        out_specs=pl.BlockSpec(memory_space=pl.ANY),
        scratch_shapes=[pltpu.SemaphoreType.DMA((2,))],
    )
    return pl.pallas_call(
        _moe_permute_kernel,
        grid_spec=grid_spec,
        out_shape=jax.ShapeDtypeStruct((n_rows, d), F32),
        compiler_params=_cparams(("arbitrary",)),
        name="moe_permute",
    )(fill_lo, fill_hi, dest3, x)


def _moe_experts_kernel(te_ref, nv_ref, x_ref, wg_ref, wu_ref, wd_ref, y_ref, xb_s, acc_s):
    t = pl.program_id(0)
    f = pl.program_id(1)

    @pl.when(f == 0)
    def _():
        xb_s[...] = x_ref[...].astype(BF16)
        acc_s[...] = jnp.zeros_like(acc_s)

    @pl.when(nv_ref[t] > 0)
    def _():
        acc_s[...] += _swiglu_partial(xb_s[...], wg_ref[0], wu_ref[0], wd_ref[0])

    @pl.when(f == pl.num_programs(1) - 1)
    def _():
        y_ref[...] = acc_s[...]


def _moe_experts(xs, tile_exp, n_valid, tm, w_gate, w_up, w_down):
    n_rows, d = xs.shape
    ff = w_gate.shape[2]
    tf = _ff_tile(ff, 1792)
    grid_spec = pltpu.PrefetchScalarGridSpec(
        num_scalar_prefetch=2,
        grid=(n_rows // tm, ff // tf),
        in_specs=[pl.BlockSpec((tm, d), lambda t, f, te, nv: (t, 0)),
                  pl.BlockSpec((1, d, tf), lambda t, f, te, nv: (te[t], 0, f)),
                  pl.BlockSpec((1, d, tf), lambda t, f, te, nv: (te[t], 0, f)),
                  pl.BlockSpec((1, tf, d), lambda t, f, te, nv: (te[t], f, 0))],
        out_specs=pl.BlockSpec((tm, d), lambda t, f, te, nv: (t, 0)),
        scratch_shapes=[pltpu.VMEM((tm, d), BF16), pltpu.VMEM((tm, d), F32)],
    )
    return pl.pallas_call(
        _moe_experts_kernel,
        grid_spec=grid_spec,
        out_shape=jax.ShapeDtypeStruct((n_rows, d), F32),
        compiler_params=_cparams(("parallel", "arbitrary")),
        name="moe_experts",
    )(tile_exp, n_valid, xs, w_gate.astype(BF16), w_up.astype(BF16), w_down.astype(BF16))


def _moe_combine_kernel(alpha, d0_ref, d1_ref, d0n_ref, d1n_ref, x_ref, w_ref, g_ref, b_ref,
                        ys_hbm, o_ref, yg_s, sem):
    i = pl.program_id(0)
    tm = x_ref.shape[0]
    buf = i % 2

    def slot_row(idx_ref, k, r, b):
        return pltpu.make_async_copy(ys_hbm.at[pl.ds(idx_ref[0, 0, r], 1), :],
                                     yg_s.at[b, k, pl.ds(r, 1), :], sem.at[b])

    def each_row(fn):
        def body(r, c):
            fn(r)
            return c
        lax.fori_loop(0, tm, body, 0, unroll=8)

    def start(i0, i1, b):
        each_row(lambda r: (slot_row(i0, 0, r, b).start(), slot_row(i1, 1, r, b).start()))

    @pl.when(i == 0)
    def _():
        start(d0_ref, d1_ref, 0)

    @pl.when(i + 1 < pl.num_programs(0))
    def _():
        start(d0n_ref, d1n_ref, 1 - buf)

    each_row(lambda r: (slot_row(d0_ref, 0, r, buf).wait(), slot_row(d1_ref, 1, r, buf).wait()))
    w = w_ref[...]
    mix = yg_s[buf, 0] * w[:, 0:1] + yg_s[buf, 1] * w[:, 1:2]
    o_ref[...] = _layer_norm(alpha * x_ref[...] + mix, g_ref[...], b_ref[...])


def _moe_combine_deepnorm(x, ys, dest, wgt, ln_g, ln_b, alpha):
    m, d = x.shape
    tm = _tile(m, 512)
    n = m // tm
    dest3 = dest.reshape(2 * n, 1, tm)
    smem_rows = lambda imap: pl.BlockSpec((1, 1, tm), imap, memory_space=pltpu.SMEM)
    nxt = lambda i: jnp.minimum(i + 1, n - 1)
    return pl.pallas_call(
        functools.partial(_moe_combine_kernel, alpha),
        grid=(n,),
        in_specs=[smem_rows(lambda i: (i, 0, 0)),
                  smem_rows(lambda i: (n + i, 0, 0)),
                  smem_rows(lambda i: (nxt(i), 0, 0)),
                  smem_rows(lambda i: (n + nxt(i), 0, 0)),
                  pl.BlockSpec((tm, d), lambda i: (i, 0)),
                  pl.BlockSpec((tm, LANES), lambda i: (i, 0)),
                  pl.BlockSpec((1, d), lambda i: (0, 0)),
                  pl.BlockSpec((1, d), lambda i: (0, 0)),
                  pl.BlockSpec(memory_space=pl.ANY)],
        out_specs=pl.BlockSpec((tm, d), lambda i: (i, 0)),
        out_shape=jax.ShapeDtypeStruct((m, d), F32),
        scratch_shapes=[pltpu.VMEM((2, 2, tm, d), F32), pltpu.SemaphoreType.DMA((2,))],
        compiler_params=_cparams(("arbitrary",)),
        name="moe_combine_deepnorm",
    )(dest3, dest3, dest3, dest3, x, wgt, ln_g.reshape(1, d), ln_b.reshape(1, d), ys)


def _moe_deepnorm(x, router, w_gate, w_up, w_down, ln_g, ln_b, alpha):
    m = x.shape[0]
    n_exp = router.shape[1]
    tm = _tile(m, MOE_ROWS)
    idx, wgt = _router_top2(x, router)
    dest, tile_exp, n_valid, fill_lo, fill_hi = _moe_plan(idx, m, tm, n_exp)
    xs = _moe_permute(x, dest, fill_lo, fill_hi, 2 * m + n_exp * tm)
    ys = _moe_experts(xs, tile_exp, n_valid, tm, w_gate, w_up, w_down)
    return _moe_combine_deepnorm(x, ys, dest, wgt, ln_g, ln_b, alpha)


def _proj_pairs_kernel(transposed, x_ref, w_ref, o_ref, xb_s):
    @pl.when(pl.program_id(1) == 0)
    def _():
        xb_s[...] = x_ref[...].astype(BF16)

    n_slab = o_ref.shape[0]
    if transposed:
        res = _dot_nt(w_ref[...], xb_s[...])
        for q in range(n_slab):
            o_ref[q] = res[q * LANES:(q + 1) * LANES, :].astype(o_ref.dtype)
    else:
        res = _dot(xb_s[...], w_ref[...])
        for q in range(n_slab):
            o_ref[q] = res[:, q * LANES:(q + 1) * LANES].astype(o_ref.dtype)


def _proj_pairs(x, w, transposed, out_dtype=F32):
    m, d = x.shape
    n = w.shape[1]
    tm = _tile(m, 512)
    tn = _tile(n, 1024)
    n_slab = tn // LANES
    if transposed:
        wb = w.T.astype(BF16)
        w_spec = pl.BlockSpec((tn, d), lambda i, j: (j, 0))
        o_spec = pl.BlockSpec((n_slab, LANES, tm), lambda i, j: (j, 0, i))
        o_shape = jax.ShapeDtypeStruct((n // LANES, LANES, m), out_dtype)
    else:
        wb = w.astype(BF16)
        w_spec = pl.BlockSpec((d, tn), lambda i, j: (0, j))
        o_spec = pl.BlockSpec((n_slab, tm, LANES), lambda i, j: (j, i, 0))
        o_shape = jax.ShapeDtypeStruct((n // LANES, m, LANES), out_dtype)
    return pl.pallas_call(
        functools.partial(_proj_pairs_kernel, transposed),
        grid=(m // tm, n // tn),
        in_specs=[pl.BlockSpec((tm, d), lambda i, j: (i, 0)), w_spec],
        out_specs=o_spec,
        out_shape=o_shape,
        scratch_shapes=[pltpu.VMEM((tm, d), BF16)],
        compiler_params=_cparams(("parallel", "arbitrary")),
        name="proj_pairs_t" if transposed else "proj_pairs",
    )(x, wb)


def _block_mean_kernel(k_ref, o_ref):
    k = k_ref[0]
    nb = k.shape[0] // MOBA_BLOCK
    o_ref[0] = jnp.mean(k.reshape(nb, MOBA_BLOCK, LANES), axis=1)


def _block_means(k_pairs, batch):
    n_pair, m, _ = k_pairs.shape
    seq = m // batch
    nb = seq // MOBA_BLOCK
    return pl.pallas_call(
        _block_mean_kernel,
        grid=(n_pair, batch),
        in_specs=[pl.BlockSpec((1, seq, LANES), lambda p, b: (p, b, 0))],
        out_specs=pl.BlockSpec((1, nb, LANES), lambda p, b: (p, b, 0)),
        out_shape=jax.ShapeDtypeStruct((n_pair, batch * nb, LANES), F32),
        compiler_params=_cparams(("parallel", "parallel")),
        name="moba_block_means",
    )(k_pairs)


def _moba_kernel(scale, qt_ref, k_ref, vt_ref, km_ref, o_ref, neg_s, sca_s, scb_s, m_s, acc_s):
    own = pl.program_id(2)
    blk = MOBA_BLOCK
    n_slab = qt_ref.shape[0]
    nb = km_ref.shape[1]
    row = lax.broadcasted_iota(jnp.int32, (LANES, blk), 0)
    zero_q = jnp.zeros((LANES, blk), F32)
    n_iota = lax.broadcasted_iota(jnp.int32, (nb, blk), 0).astype(F32)
    past = n_iota < own.astype(F32)
    neg_inf = jnp.float32(-jnp.inf)

    streams = [(g, h) for g in range(n_slab) for h in range(2)]
    qh = []
    for s, (g, h) in enumerate(streams):
        in_head = (row < HEAD_DIM) if h == 0 else (row >= HEAD_DIM)
        q_h = jnp.where(in_head, qt_ref[g], zero_q)
        qh.append((q_h * (scale * LOG2E)).astype(BF16))
        gate = _dot_sp(_split(km_ref[g]), _split(q_h))
        gate = jnp.where(past, gate, neg_inf)
        neg = jnp.full((nb, blk), NEG_BIG, F32)
        for _ in range(min(MOBA_TOPK, nb)):
            mx = jnp.max(gate, axis=0, keepdims=True)
            idx = jnp.min(jnp.where(gate == mx, n_iota, float(nb)), axis=0, keepdims=True)
            pick = n_iota == idx
            neg = jnp.where(jnp.logical_and(pick, past), 0.0, neg)
            gate = jnp.where(pick, neg_inf, gate)
        neg_s[s] = neg

    ones_rows = jnp.ones((DEN_ROWS, blk), BF16)

    def block_scores(n, s):
        start = pl.multiple_of(n * blk, blk)
        return _dot(k_ref[streams[s][0], pl.ds(start, blk), :], qh[s])

    def absorb(n, st, bias, s):
        g, h = streams[s]
        start = pl.multiple_of(n * blk, blk)
        mx = m_s[s]
        mblk = jnp.max(st, axis=0, keepdims=True)
        if bias is not None:
            mblk = mblk + bias
        mx_new = jnp.maximum(mx, mblk)
        alpha = jnp.exp2(mx - mx_new)
        shift = mx_new if bias is None else mx_new - bias
        p = jnp.exp2(st - shift).astype(BF16)
        vtb = vt_ref[g, h * HEAD_DIM:(h + 1) * HEAD_DIM, pl.ds(start, blk)]
        m_s[s] = mx_new
        acc_s[s] = alpha * acc_s[s] + _dot(jnp.concatenate([vtb, ones_rows], axis=0), p)

    kpos = lax.broadcasted_iota(jnp.int32, (blk, blk), 0)
    qpos = lax.broadcasted_iota(jnp.int32, (blk, blk), 1)
    causal = kpos <= qpos
    own_scores = [block_scores(own, s) for s in range(len(streams))]
    for s in range(len(streams)):
        sca_s[s] = block_scores(0, s)
        m_s[s] = jnp.full((1, blk), neg_inf, F32)
        acc_s[s] = jnp.zeros((HEAD_DIM + DEN_ROWS, blk), F32)
    for s in range(len(streams)):
        absorb(own, jnp.where(causal, own_scores[s], neg_inf), None, s)

    def body(j, carry):
        n0 = 2 * j
        n1 = n0 + 1
        n2 = jnp.minimum(n0 + 2, nb - 1)
        for s in range(len(streams)):
            scb_s[s] = block_scores(n1, s)
            absorb(n0, sca_s[s], neg_s[s, pl.ds(n0, 1), :], s)
        for s in range(len(streams)):
            sca_s[s] = block_scores(n2, s)
            absorb(n1, scb_s[s], neg_s[s, pl.ds(n1, 1), :], s)
        return carry

    lax.fori_loop(0, (own + 1) // 2, body, 0)
    for g in range(n_slab):
        a0, a1 = acc_s[2 * g], acc_s[2 * g + 1]
        o_t = jnp.concatenate([a0[:HEAD_DIM] / a0[HEAD_DIM:HEAD_DIM + 1],
                               a1[:HEAD_DIM] / a1[HEAD_DIM:HEAD_DIM + 1]], axis=0)
        o_ref[g] = o_t.T.astype(o_ref.dtype)


def _moba_attention(q_t, k, v_t, k_means, batch):
    n_pair, _, m = q_t.shape
    seq = m // batch
    nb = seq // MOBA_BLOCK
    g = _tile(n_pair, MOBA_SLABS)
    return pl.pallas_call(
        functools.partial(_moba_kernel, HEAD_DIM ** -0.5),
        grid=(n_pair // g, batch, nb),
        in_specs=[pl.BlockSpec((g, LANES, MOBA_BLOCK), lambda p, b, i: (p, 0, b * nb + i)),
                  pl.BlockSpec((g, seq, LANES), lambda p, b, i: (p, b, 0),
                               pipeline_mode=pl.Buffered(1)),
                  pl.BlockSpec((g, LANES, seq), lambda p, b, i: (p, 0, b),
                               pipeline_mode=pl.Buffered(1)),
                  pl.BlockSpec((g, nb, LANES), lambda p, b, i: (p, b, 0))],
        out_specs=pl.BlockSpec((g, MOBA_BLOCK, LANES), lambda p, b, i: (p, b * nb + i, 0)),
        out_shape=jax.ShapeDtypeStruct((n_pair, m, LANES), BF16),
        scratch_shapes=[pltpu.VMEM((2 * g, nb, MOBA_BLOCK), F32),
                        pltpu.VMEM((2 * g, MOBA_BLOCK, MOBA_BLOCK), F32),
                        pltpu.VMEM((2 * g, MOBA_BLOCK, MOBA_BLOCK), F32),
                        pltpu.VMEM((2 * g, 1, MOBA_BLOCK), F32),
                        pltpu.VMEM((2 * g, HEAD_DIM + DEN_ROWS, MOBA_BLOCK), F32)],
        compiler_params=_cparams(("parallel", "parallel", "arbitrary")),
        name="moba_attention",
    )(q_t, k, v_t, k_means)


def kernel(x, rwkv_mu, rwkv_w_rkv, rwkv_w_out, rwkv_decay_w0, rwkv_decay_w1, rwkv_decay_w2, rwkv_iclr_a0, rwkv_iclr_a1, rwkv_iclr_a2, rwkv_vres_v0, rwkv_vres_v1, rwkv_vres_v2, rwkv_gate_g1, rwkv_gate_g2, rwkv_k_k, rwkv_k_a, rwkv_r_k, rwkv_gn_g, rwkv_gn_b, moba_w_k, moba_w_v, moba_w_q, moba_w_o, ffn_w_gate, ffn_w_up, ffn_w_down, moe_router, moe_w_gate, moe_w_up, moe_w_down, ln_g, ln_b):
    batch, seq, d = x.shape
    assert d % (2 * LANES) == 0 and seq % MOBA_BLOCK == 0 and seq % WKV_CHUNK == 0
    depth = ln_g.shape[0]
    n_rwkv = rwkv_mu.shape[0]
    alpha = (2.0 * depth) ** 0.25
    h = x.reshape(batch * seq, d)
    v_first = None
    kv = None
    for layer in range(depth):
        if layer < n_rwkv:
            i = layer
            vres = None if i == 0 else (rwkv_vres_v0[i - 1], rwkv_vres_v1[i - 1], rwkv_vres_v2[i - 1])
            r, e, k, v, a, b, g = _rwkv_proj(
                h, seq, rwkv_mu[i], rwkv_w_rkv[i], rwkv_decay_w1[i], rwkv_iclr_a1[i],
                rwkv_gate_g1[i], rwkv_decay_w2[i], rwkv_iclr_a2[i], rwkv_gate_g2[i],
                rwkv_decay_w0[i], rwkv_iclr_a0[i], rwkv_k_k[i], rwkv_k_a[i],
                vres=vres, v_first=v_first)
            if i == 0:
                v_first = v
            y = _wkv_scan(r, e, k, v, a, b, batch)
            mix = _rwkv_post(y, r, k, v, g, rwkv_r_k[i], rwkv_gn_g[i], rwkv_gn_b[i])
            w_out = rwkv_w_out[i]
        else:
            jdx = layer - n_rwkv
            k_pairs, v_t, k_means = kv
            q_t = _proj_pairs(h, moba_w_q[jdx], transposed=True)
            mix = _moba_attention(q_t, k_pairs, v_t, k_means, batch)
            w_out = moba_w_o[jdx]
        h = _out_proj_deepnorm(mix, w_out, h, ln_g[layer, 0], ln_b[layer, 0], alpha)
        ex = layer // 2
        if layer % 2 == 0:
            h = _ffn_deepnorm(h, _layer_bf16(ffn_w_gate, ex), _layer_bf16(ffn_w_up, ex),
                              _layer_bf16(ffn_w_down, ex), ln_g[layer, 1], ln_b[layer, 1], alpha)
        else:
            h = _moe_deepnorm(h, moe_router[ex], _layer_bf16(moe_w_gate, ex),
                              _layer_bf16(moe_w_up, ex), _layer_bf16(moe_w_down, ex),
                              ln_g[layer, 1], ln_b[layer, 1], alpha)
        if layer == n_rwkv - 1:
            k_pairs = _proj_pairs(h, moba_w_k, transposed=False)
            v_t = _proj_pairs(h, moba_w_v, transposed=True, out_dtype=BF16)
            kv = (k_pairs.astype(BF16), v_t, _block_means(k_pairs, batch))
    return h.reshape(batch, seq, d)
```

```python
import functools
import math

import jax
import jax.numpy as jnp
from jax import lax
from jax.experimental import pallas as pl
from jax.experimental.pallas import tpu as pltpu

HEAD_DIM = 64
LANES = 128
MXU_DEPTH = 256
GN_EPS = 64e-5
LN_EPS = 1e-5
MOBA_BLOCK = 256
MOBA_TOPK = 3
MOE_ROWS = 512
FFN_TILE = 2816
CAST_BLOCK_ELEMS = 1 << 20
WKV_CHUNK = 64
WKV_SLABS = 8
NEG_BIG = -1e30
MOBA_SLABS = 8
DEN_ROWS = 16
LOG2E = 1.4426950408889634

F32 = jnp.float32
BF16 = jnp.bfloat16
VMEM_LIMIT = 56 * 1024 * 1024


def _cparams(sem):
    return pltpu.CompilerParams(dimension_semantics=sem, vmem_limit_bytes=VMEM_LIMIT)


def _dot(a, b):
    return jnp.dot(a, b, preferred_element_type=F32)


def _dot_nt(a, b):
    return lax.dot_general(a, b, (((1,), (1,)), ((), ())), preferred_element_type=F32)


def _split(x):
    hi = x.astype(BF16)
    lo = (x - hi.astype(F32)).astype(BF16)
    return hi, lo


def _dot_sp(a, b, nt=False):
    d = _dot_nt if nt else _dot
    return d(a[0], b[0]) + (d(a[0], b[1]) + d(a[1], b[0]))


def _dot_exact_rhs(a, b_exact, nt=False):
    d = _dot_nt if nt else _dot
    hi, lo = _split(a)
    return d(hi, b_exact) + d(lo, b_exact)


def _sigmoid(x):
    return 1.0 / (1.0 + jnp.exp(-x))


def _layer_norm(y, g, b):
    mu = jnp.mean(y, axis=-1, keepdims=True)
    yc = y - mu
    var = jnp.mean(yc * yc, axis=-1, keepdims=True)
    return yc * lax.rsqrt(var + LN_EPS) * g + b


def _head_blockdiag(n):
    i = jnp.arange(n) // HEAD_DIM
    return (i[:, None] == i[None, :]).astype(BF16)


def _tile(n, want):
    t = min(n, want)
    assert n % t == 0, (n, want)
    return t


def _cast_kernel(w_ref, o_ref):
    o_ref[...] = w_ref[...].astype(o_ref.dtype)


def _layer_bf16(w, layer):
    shape = w.shape[1:]
    c = shape[-1]
    rows = math.prod(shape[:-1])
    tr = rows
    for cand in range(min(rows, CAST_BLOCK_ELEMS // c) // 16, 0, -1):
        if rows % (cand * 16) == 0:
            tr = cand * 16
            break
    out = pl.pallas_call(
        _cast_kernel,
        grid=(rows // tr,),
        in_specs=[pl.BlockSpec((1, tr, c), lambda i: (layer, i, 0))],
        out_specs=pl.BlockSpec((1, tr, c), lambda i: (0, i, 0)),
        out_shape=jax.ShapeDtypeStruct((1, rows, c), BF16),
        compiler_params=_cparams(("parallel",)),
        name="weight_to_bf16",
    )(w.reshape(w.shape[0], rows, c))
    return out.reshape(shape)


def _rwkv_proj_kernel(has_vres, steps_per_seq, *refs):
    if has_vres:
        (x_ref, xp_ref, mu_ref, wrkv_ref, wd1_ref, wa1_ref, wg1_ref, wv1_ref,
         wd2_ref, wa2_ref, wg2_ref, wv2_ref, w0_ref, a0_ref, v0_ref, kk_ref, ka_ref,
         bd_ref, vf_ref,
         r_out, e_out, k_out, v_out, a_out, b_out, g_out,
         xm_s, hd_s, ha_s, hg_s, hv_s) = refs
    else:
        (x_ref, xp_ref, mu_ref, wrkv_ref, wd1_ref, wa1_ref, wg1_ref,
         wd2_ref, wa2_ref, wg2_ref, w0_ref, a0_ref, kk_ref, ka_ref,
         bd_ref,
         r_out, e_out, k_out, v_out, a_out, b_out, g_out,
         xm_s, hd_s, ha_s, hg_s) = refs
    i = pl.program_id(0)
    j = pl.program_id(1)

    @pl.when(j == 0)
    def _():
        x = x_ref[...]
        tm = x.shape[0]
        prev_row = jnp.where(i % steps_per_seq == 0, 0.0, xp_ref[7:8, :])
        rolled = pltpu.roll(x, 1, axis=0)
        row = lax.broadcasted_iota(jnp.int32, (tm, 1), 0)
        x_prev = jnp.where(row == 0, prev_row, rolled)
        xx = x_prev - x
        for c in range(3):
            xm_s[c] = (x + xx * mu_ref[c:c + 1, :]).astype(BF16)
        xw = (x + xx * mu_ref[3:4, :]).astype(BF16)
        xa = (x + xx * mu_ref[4:5, :]).astype(BF16)
        xg = (x + xx * mu_ref[5:6, :]).astype(BF16)
        hd_s[...] = jnp.tanh(_dot(xw, wd1_ref[...])).astype(BF16)
        ha_s[...] = _dot(xa, wa1_ref[...]).astype(BF16)
        hg_s[...] = _sigmoid(_dot(xg, wg1_ref[...])).astype(BF16)
        if has_vres:
            hv_s[...] = _dot(xm_s[2], wv1_ref[...]).astype(BF16)

    r = _dot(xm_s[0], wrkv_ref[0])
    k = _dot(xm_s[1], wrkv_ref[1])
    v = _dot(xm_s[2], wrkv_ref[2])
    z = w0_ref[...] + _dot(hd_s[...], wd2_ref[...])
    nz = -z
    softplus = jnp.maximum(nz, 0.0) + jnp.log(1.0 + jnp.exp(-jnp.abs(nz)))
    e = jnp.exp(-softplus - 0.5)
    a = _sigmoid(a0_ref[...] + _dot(ha_s[...], wa2_ref[...]))
    g = _dot(hg_s[...], wg2_ref[...])
    n_slab = r.shape[1] // LANES
    if has_vres:
        vf = jnp.concatenate([vf_ref[q] for q in range(n_slab)], axis=1)
        v = v + (vf - v) * _sigmoid(v0_ref[...] + _dot(hv_s[...], wv2_ref[...]))
    kk = k * kk_ref[...]
    ss = _dot_exact_rhs(kk * kk, bd_ref[...])
    kk = kk * lax.rsqrt(jnp.maximum(ss, 1e-24))
    k = k * (1.0 + (a - 1.0) * ka_ref[...])
    for q in range(n_slab):
        sl = slice(q * LANES, (q + 1) * LANES)
        r_out[q] = r[:, sl].astype(r_out.dtype)
        e_out[q] = e[:, sl]
        k_out[q] = k[:, sl].astype(k_out.dtype)
        v_out[q] = v[:, sl].astype(v_out.dtype)
        a_out[q] = (-kk)[:, sl].astype(a_out.dtype)
        b_out[q] = (kk * a)[:, sl].astype(b_out.dtype)
        g_out[q] = g[:, sl].astype(g_out.dtype)


def _rwkv_proj(x, seq_len, mu, w_rkv, wd1, wa1, wg1, wd2, wa2, wg2, w0, a0, k_k, k_a,
               vres=None, v_first=None):
    m, d = x.shape
    tm = _tile(seq_len, 512)
    tn = _tile(d, 256)
    n_slab = tn // LANES
    has_vres = vres is not None
    row = lambda a: a.reshape(1, d)
    full = lambda a: pl.BlockSpec(a.shape, lambda i, j: (0,) * a.ndim)
    colblk = lambda rows: pl.BlockSpec((rows, tn), lambda i, j: (0, j))
    bd = _head_blockdiag(tn)
    wd1, wa1, wg1 = wd1.astype(BF16), wa1.astype(BF16), wg1.astype(BF16)
    wd2, wa2, wg2 = wd2.astype(BF16), wa2.astype(BF16), wg2.astype(BF16)
    w_rkv = w_rkv.astype(BF16)
    args = [x, x, mu, w_rkv, wd1, wa1, wg1]
    specs = [pl.BlockSpec((tm, d), lambda i, j: (i, 0)),
             pl.BlockSpec((8, d), lambda i, j: (jnp.maximum(i * (tm // 8) - 1, 0), 0)),
             full(mu),
             pl.BlockSpec((3, d, tn), lambda i, j: (0, 0, j)),
             full(wd1), full(wa1), full(wg1)]
    if has_vres:
        v0, wv1, wv2 = vres
        wv1, wv2 = wv1.astype(BF16), wv2.astype(BF16)
        args += [wv1]
        specs += [full(wv1)]
    args += [wd2, wa2, wg2]
    specs += [colblk(wd2.shape[0]), colblk(wa2.shape[0]), colblk(wg2.shape[0])]
    if has_vres:
        args += [wv2]
        specs += [colblk(wv2.shape[0])]
    args += [row(w0), row(a0)]
    specs += [colblk(1), colblk(1)]
    if has_vres:
        args += [row(v0)]
        specs += [colblk(1)]
    args += [row(k_k), row(k_a), bd]
    specs += [colblk(1), colblk(1), full(bd)]
    slab_spec = pl.BlockSpec((n_slab, tm, LANES), lambda i, j: (j, i, 0))
    if has_vres:
        args += [v_first]
        specs += [slab_spec]
    slab = lambda dt: jax.ShapeDtypeStruct((d // LANES, m, LANES), dt)
    scratch = [pltpu.VMEM((3, tm, d), BF16),
               pltpu.VMEM((tm, wd1.shape[1]), BF16),
               pltpu.VMEM((tm, wa1.shape[1]), BF16),
               pltpu.VMEM((tm, wg1.shape[1]), BF16)]
    if has_vres:
        scratch.append(pltpu.VMEM((tm, wv1.shape[1]), BF16))
    return pl.pallas_call(
        functools.partial(_rwkv_proj_kernel, has_vres, seq_len // tm),
        grid=(m // tm, d // tn),
        in_specs=specs,
        out_specs=[slab_spec] * 7,
        out_shape=[slab(BF16), slab(F32)] + [slab(BF16)] * 5,
        scratch_shapes=scratch,
        compiler_params=_cparams(("parallel", "arbitrary")),
        name="rwkv_proj",
    )(*args)


def _wkv_kernel(r_ref, e_ref, k_ref, v_ref, a_ref, b_ref, y_ref, s_ref):
    c = WKV_CHUNK
    n_slab, n_batch, tb, _ = r_ref.shape
    n_chunks = tb // c

    @pl.when(pl.program_id(1) == 0)
    def _():
        s_ref[...] = jnp.zeros_like(s_ref)

    lane = lax.broadcasted_iota(jnp.int32, (c, LANES), 1)
    t_idx = lax.broadcasted_iota(jnp.int32, (c, LANES), 0)
    head0 = lane < HEAD_DIM
    s_idx = jnp.bitwise_and(lane, HEAD_DIM - 1)
    strict = s_idx < t_idx
    incl = s_idx <= t_idx
    rr = lax.broadcasted_iota(jnp.int32, (LANES, LANES), 0)
    cc = lax.broadcasted_iota(jnp.int32, (LANES, LANES), 1)
    same_head = (rr < HEAD_DIM) == (cc < HEAD_DIM)

    def bd(x):
        z = jnp.zeros_like(x)
        return jnp.concatenate([jnp.where(head0, x, z), jnp.where(head0, z, x)], axis=0)

    def cat(a, b, axis=0):
        return jnp.concatenate([a, b], axis=axis)

    def chunk(ci, carry):
        stages = [one_slab(ci, p, bb) for p in range(n_slab) for bb in range(n_batch)]
        while stages:
            stages = [g for g in stages if next(g, "done") != "done"]
        return carry

    def one_slab(ci, p, bb):
        sl = pl.ds(pl.multiple_of(ci * c, c), c)
        r = r_ref[p, bb, sl, :]
        e = e_ref[p, bb, sl, :]
        k = k_ref[p, bb, sl, :]
        v = v_ref[p, bb, sl, :]
        a = a_ref[p, bb, sl, :]
        b = b_ref[p, bb, sl, :]
        s0 = s_ref[p * n_batch + bb]
        cum = _cumsum_rows(e, t_idx)
        yield
        tot = cum[c - 1:c, :]
        ar = cat(a * jnp.exp(e - cum), r * jnp.exp(-cum)).astype(BF16)
        ec = jnp.exp(cum)
        bt = (b * ec).astype(BF16)
        kt = (k * ec).astype(BF16)
        eh = jnp.exp(cum - tot)
        bk = cat(b * eh, k * eh).astype(BF16)
        bdv = bd(v.astype(BF16))
        gram = _dot_nt(ar, cat(bd(bt), bd(kt)))
        yield
        zero = jnp.zeros((c, LANES), F32)
        l_ab = jnp.where(strict, gram[:c, :LANES], zero)
        n_ak = jnp.where(strict, gram[:c, LANES:], zero)
        m_rb = jnp.where(incl, gram[c:, :LANES], zero)
        m_rk = jnp.where(incl, gram[c:, LANES:], zero)
        xy = _dot_nt(ar, s0.astype(BF16)) + _dot(cat(n_ak, m_rk).astype(BF16), bdv)
        x = xy[:c]
        y0 = xy[c:]
        yield
        lp = l_ab.astype(BF16)
        n_steps = int(math.log2(c))
        for step in range(n_steps):
            bdx = bd(x.astype(BF16))
            if step + 1 < n_steps:
                t = _dot(lp, cat(bdx, bd(lp), axis=1))
                x = x + t[:, :LANES]
                lp = t[:, LANES:].astype(BF16)
            else:
                x = x + _dot(lp, bdx)
            yield
        y_ref[p, bb, sl, :] = y0 + _dot(m_rb.astype(BF16), bd(x.astype(BF16)))
        uv_t = cat(x, v.astype(F32)).T.astype(BF16)
        upd = _dot(uv_t, bk)
        s_ref[p * n_batch + bb] = (s0 * jnp.exp(-tot)
                                   + jnp.where(same_head, upd, jnp.zeros_like(upd)))
        yield

    lax.fori_loop(0, n_chunks, chunk, 0)


def _cumsum_rows(e, row_idx):
    cum = e
    shift = 1
    while shift < e.shape[0]:
        cum = cum + jnp.where(row_idx >= shift, pltpu.roll(cum, shift, axis=0), 0.0)
        shift *= 2
    return cum


def _wkv_scan(r, e, k, v, a, b, batch):
    n_pair, m, _ = r.shape
    seq = m // batch
    tb = _tile(seq, 256)
    n_slab = _tile(n_pair, WKV_SLABS)
    view = lambda t: t.reshape(n_pair, batch, seq, LANES)
    spec = pl.BlockSpec((n_slab, batch, tb, LANES), lambda p, t: (p, 0, t, 0))
    y = pl.pallas_call(
        _wkv_kernel,
        grid=(n_pair // n_slab, seq // tb),
        in_specs=[spec] * 6,
        out_specs=spec,
        out_shape=jax.ShapeDtypeStruct((n_pair, batch, seq, LANES), F32),
        scratch_shapes=[pltpu.VMEM((n_slab * batch, LANES, LANES), F32)],
        compiler_params=_cparams(("parallel", "arbitrary")),
        name="wkv_scan",
    )(view(r), view(e), view(k), view(v), view(a), view(b))
    return y.reshape(n_pair, m, LANES)


def _rwkv_post_kernel(y_ref, r_ref, k_ref, v_ref, g_ref, rk_ref, gg_ref, gb_ref, bd_ref, z_ref):
    y = y_ref[0]
    bd2 = bd_ref[...]
    bdm = bd2[:LANES, :LANES]
    inv_n = 1.0 / HEAD_DIM
    rk = r_ref[0].astype(F32) * k_ref[0].astype(F32) * rk_ref[0]
    sums = _dot_exact_rhs(jnp.concatenate([y, rk], axis=1), bd2)
    yc = y - sums[:, :LANES] * inv_n
    var = _dot_exact_rhs(yc * yc, bdm) * inv_n
    yn = yc * lax.rsqrt(var + GN_EPS) * gg_ref[0] + gb_ref[0]
    bonus = sums[:, LANES:] * v_ref[0].astype(F32)
    z_ref[0] = ((yn + bonus) * g_ref[0].astype(F32)).astype(z_ref.dtype)


def _rwkv_post(y, r, k, v, g, r_k, gn_g, gn_b):
    n_pair, m, _ = y.shape
    tm = _tile(m, 2048)
    spec = pl.BlockSpec((1, tm, LANES), lambda p, i: (p, i, 0))
    pspec = pl.BlockSpec((1, 1, LANES), lambda p, i: (p, 0, 0))
    bd = _head_blockdiag(2 * LANES)
    slab = lambda a: a.reshape(n_pair, 1, LANES)
    return pl.pallas_call(
        _rwkv_post_kernel,
        grid=(n_pair, m // tm),
        in_specs=[spec] * 5 + [pspec] * 3 + [pl.BlockSpec(bd.shape, lambda p, i: (0, 0))],
        out_specs=spec,
        out_shape=jax.ShapeDtypeStruct((n_pair, m, LANES), BF16),
        compiler_params=_cparams(("parallel", "parallel")),
        name="rwkv_post",
    )(y, r, k, v, g, slab(r_k), slab(gn_g), slab(gn_b), bd)


def _out_proj_kernel(alpha, z_ref, w_ref, x_ref, g_ref, b_ref, o_ref):
    per = w_ref.shape[1] // LANES
    acc = None
    for c in range(w_ref.shape[0]):
        zc = jnp.concatenate([z_ref[c * per + q] for q in range(per)], axis=1)
        part = _dot(zc, w_ref[c])
        acc = part if acc is None else acc + part
    o_ref[...] = _layer_norm(alpha * x_ref[...] + acc, g_ref[...], b_ref[...])


def _out_proj_deepnorm(z, w, x, ln_g, ln_b, alpha):
    n_pair, m, _ = z.shape
    d = w.shape[1]
    tm = _tile(m, 1024)
    kc = _tile(d, MXU_DEPTH)
    w3 = w.astype(BF16).reshape(d // kc, kc, d)
    return pl.pallas_call(
        functools.partial(_out_proj_kernel, alpha),
        grid=(m // tm,),
        in_specs=[pl.BlockSpec((n_pair, tm, LANES), lambda i: (0, i, 0)),
                  pl.BlockSpec(w3.shape, lambda i: (0, 0, 0)),
                  pl.BlockSpec((tm, d), lambda i: (i, 0)),
                  pl.BlockSpec((1, d), lambda i: (0, 0)),
                  pl.BlockSpec((1, d), lambda i: (0, 0))],
        out_specs=pl.BlockSpec((tm, d), lambda i: (i, 0)),
        out_shape=jax.ShapeDtypeStruct((m, d), F32),
        compiler_params=_cparams(("parallel",)),
        name="out_proj_deepnorm",
    )(z, w3, x, ln_g.reshape(1, d), ln_b.reshape(1, d))


def _ff_tile(ff, want):
    for cand in range(min(ff, want) // LANES, 0, -1):
        if ff % (cand * LANES) == 0:
            return cand * LANES
    return ff


def _swiglu_partial(xb, wg, wu, wd):
    h1 = _dot(xb, wg)
    h2 = _dot(xb, wu)
    act = h1 * _sigmoid(h1) * h2
    return _dot(act.astype(BF16), wd)


def _ffn_kernel(alpha, x_ref, wg_ref, wu_ref, wd_ref, g_ref, b_ref, o_ref, xb_s, acc_s):
    f = pl.program_id(1)

    @pl.when(f == 0)
    def _():
        xb_s[...] = x_ref[...].astype(BF16)
        acc_s[...] = jnp.zeros_like(acc_s)

    acc_s[...] += _swiglu_partial(xb_s[...], wg_ref[...], wu_ref[...], wd_ref[...])

    @pl.when(f == pl.num_programs(1) - 1)
    def _():
        o_ref[...] = _layer_norm(alpha * x_ref[...] + acc_s[...], g_ref[...], b_ref[...])


def _ffn_deepnorm(x, w_gate, w_up, w_down, ln_g, ln_b, alpha):
    m, d = x.shape
    ff = w_gate.shape[1]
    tm = _tile(m, 512)
    tf = _ff_tile(ff, FFN_TILE)
    wmode = dict(pipeline_mode=pl.Buffered(1)) if tf == ff else {}
    return pl.pallas_call(
        functools.partial(_ffn_kernel, alpha),
        grid=(m // tm, ff // tf),
        in_specs=[pl.BlockSpec((tm, d), lambda i, f: (i, 0)),
                  pl.BlockSpec((d, tf), lambda i, f: (0, f), **wmode),
                  pl.BlockSpec((d, tf), lambda i, f: (0, f), **wmode),
                  pl.BlockSpec((tf, d), lambda i, f: (f, 0), **wmode),
                  pl.BlockSpec((1, d), lambda i, f: (0, 0)),
                  pl.BlockSpec((1, d), lambda i, f: (0, 0))],
        out_specs=pl.BlockSpec((tm, d), lambda i, f: (i, 0)),
        out_shape=jax.ShapeDtypeStruct((m, d), F32),
        scratch_shapes=[pltpu.VMEM((tm, d), BF16), pltpu.VMEM((tm, d), F32)],
        compiler_params=_cparams(("parallel", "arbitrary")),
        name="ffn_deepnorm",
    )(x, w_gate.astype(BF16), w_up.astype(BF16), w_down.astype(BF16),
      ln_g.reshape(1, d), ln_b.reshape(1, d))


def _router_kernel(n_exp, x_ref, w_ref, idx_ref, wgt_ref):
    logits = _dot_sp(_split(x_ref[...]), _split(w_ref[...]))
    lane = lax.broadcasted_iota(jnp.int32, logits.shape, 1).astype(F32)
    neg_inf = jnp.float32(-jnp.inf)
    lg = jnp.where(lane < n_exp, logits, neg_inf)
    m1 = jnp.max(lg, axis=1, keepdims=True)
    i1 = jnp.min(jnp.where(lg == m1, lane, float(LANES)), axis=1, keepdims=True)
    lg2 = jnp.where(lane == i1, neg_inf, lg)
    m2 = jnp.max(lg2, axis=1, keepdims=True)
    i2 = jnp.min(jnp.where(lg2 == m2, lane, float(LANES)), axis=1, keepdims=True)
    e2 = jnp.exp(m2 - m1)
    den = 1.0 + e2
    idx_ref[...] = jnp.where(lane == 0.0, i1, jnp.where(lane == 1.0, i2, 0.0)).astype(jnp.int32)
    wgt_ref[...] = jnp.where(lane == 0.0, 1.0 / den, jnp.where(lane == 1.0, e2 / den, 0.0))


def _router_top2(x, router):
    m, d = x.shape
    n_exp = router.shape[1]
    tm = _tile(m, 512)
    w = jnp.pad(router, ((0, 0), (0, LANES - n_exp)))
    out_spec = pl.BlockSpec((tm, LANES), lambda i: (i, 0))
    return pl.pallas_call(
        functools.partial(_router_kernel, n_exp),
        grid=(m // tm,),
        in_specs=[pl.BlockSpec((tm, d), lambda i: (i, 0)),
                  pl.BlockSpec((d, LANES), lambda i: (0, 0))],
        out_specs=[out_spec, out_spec],
        out_shape=[jax.ShapeDtypeStruct((m, LANES), jnp.int32),
                   jax.ShapeDtypeStruct((m, LANES), F32)],
        compiler_params=_cparams(("parallel",)),
        name="moe_router",
    )(x, w)


def _moe_plan(idx, m, tm, n_exp):
    n_ent = 2 * m
    expert = jnp.concatenate([idx[:, 0], idx[:, 1]])
    onehot = (expert[:, None] == jnp.arange(n_exp, dtype=jnp.int32)[None, :]).astype(jnp.int32)
    csum = jnp.cumsum(onehot, axis=0)
    counts = csum[-1]
    rank = jnp.sum(csum * onehot, axis=1) - 1
    padded = ((counts + tm - 1) // tm) * tm
    ends = jnp.cumsum(padded)
    starts = ends - padded
    dest = jnp.sum(starts[None, :] * onehot, axis=1) + rank
    n_tiles = n_ent // tm + n_exp
    tile_start = jnp.arange(n_tiles, dtype=jnp.int32) * tm
    tile_exp = jnp.minimum(jnp.sum((tile_start[:, None] >= ends[None, :]).astype(jnp.int32), axis=1),
                           n_exp - 1)
    n_valid = jnp.clip(starts[tile_exp] + counts[tile_exp] - tile_start, 0, tm)
    fill_lo = jnp.concatenate([starts + counts, ends[-1:]])
    fill_hi = jnp.concatenate([ends, jnp.full((1,), n_tiles * tm, jnp.int32)])
    return dest, tile_exp, n_valid, fill_lo, fill_hi


def _moe_permute_kernel(lo_ref, hi_ref, dest_ref, x_ref, xs_hbm, sem):
    i = pl.program_id(0)
    te = dest_ref.shape[2]

    def entry_row(r):
        return pltpu.make_async_copy(x_ref.at[pl.ds(r, 1), :],
                                     xs_hbm.at[pl.ds(dest_ref[0, 0, r], 1), :], sem.at[0])

    def filler_row(r):
        return pltpu.make_async_copy(x_ref.at[pl.ds(0, 1), :], xs_hbm.at[pl.ds(r, 1), :], sem.at[1])

    def each(lo, hi, fn, unroll=1):
        def body(r, c):
            fn(r)
            return c
        lax.fori_loop(lo, hi, body, 0, unroll=unroll)

    each(0, te, lambda r: entry_row(r).start(), unroll=8)

    @pl.when(i == 0)
    def _():
        for e in range(lo_ref.shape[0]):
            each(lo_ref[e], hi_ref[e], lambda r: filler_row(r).start())
        for e in range(lo_ref.shape[0]):
            each(lo_ref[e], hi_ref[e], lambda r: filler_row(r).wait())

    each(0, te, lambda r: entry_row(r).wait(), unroll=8)


def _moe_permute(x, dest, fill_lo, fill_hi, n_rows):
    m, d = x.shape
    te = _tile(m, 512)
    n_x = m // te
    dest3 = dest.reshape(-1, 1, te)
    grid_spec = pltpu.PrefetchScalarGridSpec(
        num_scalar_prefetch=2,
        grid=(dest3.shape[0],),
        in_specs=[pl.BlockSpec((1, 1, te), lambda i, lo, hi: (i, 0, 0), memory_space=pltpu.SMEM),
                  pl.BlockSpec((te, d), lambda i, lo, hi: (i % n_x, 0))],
        out_specs=pl.BlockSpec(memory_space=pl.ANY),
        scratch_shapes=[pltpu.SemaphoreType.DMA((2,))],
    )
    return pl.pallas_call(
        _moe_permute_kernel,
        grid_spec=grid_spec,
        out_shape=jax.ShapeDtypeStruct((n_rows, d), F32),
        compiler_params=_cparams(("arbitrary",)),
        name="moe_permute",
    )(fill_lo, fill_hi, dest3, x)


def _moe_experts_kernel(te_ref, nv_ref, x_ref, wg_ref, wu_ref, wd_ref, y_ref, xb_s, acc_s):
    t = pl.program_id(0)
    f = pl.program_id(1)

    @pl.when(f == 0)
    def _():
        xb_s[...] = x_ref[...].astype(BF16)
        acc_s[...] = jnp.zeros_like(acc_s)

    @pl.when(nv_ref[t] > 0)
    def _():
        acc_s[...] += _swiglu_partial(xb_s[...], wg_ref[0], wu_ref[0], wd_ref[0])

    @pl.when(f == pl.num_programs(1) - 1)
    def _():
        y_ref[...] = acc_s[...]


def _moe_experts(xs, tile_exp, n_valid, tm, w_gate, w_up, w_down):
    n_rows, d = xs.shape
    ff = w_gate.shape[2]
    tf = _ff_tile(ff, 1792)
    grid_spec = pltpu.PrefetchScalarGridSpec(
        num_scalar_prefetch=2,
        grid=(n_rows // tm, ff // tf),
        in_specs=[pl.BlockSpec((tm, d), lambda t, f, te, nv: (t, 0)),
                  pl.BlockSpec((1, d, tf), lambda t, f, te, nv: (te[t], 0, f)),
                  pl.BlockSpec((1, d, tf), lambda t, f, te, nv: (te[t], 0, f)),
                  pl.BlockSpec((1, tf, d), lambda t, f, te, nv: (te[t], f, 0))],
        out_specs=pl.BlockSpec((tm, d), lambda t, f, te, nv: (t, 0)),
        scratch_shapes=[pltpu.VMEM((tm, d), BF16), pltpu.VMEM((tm, d), F32)],
    )
    return pl.pallas_call(
        _moe_experts_kernel,
        grid_spec=grid_spec,
        out_shape=jax.ShapeDtypeStruct((n_rows, d), F32),
        compiler_params=_cparams(("parallel", "arbitrary")),
        name="moe_experts",
    )(tile_exp, n_valid, xs, w_gate.astype(BF16), w_up.astype(BF16), w_down.astype(BF16))


def _moe_combine_kernel(alpha, d0_ref, d1_ref, d0n_ref, d1n_ref, x_ref, w_ref, g_ref, b_ref,
                        ys_hbm, o_ref, yg_s, sem):
    i = pl.program_id(0)
    tm = x_ref.shape[0]
    buf = i % 2

    def slot_row(idx_ref, k, r, b):
        return pltpu.make_async_copy(ys_hbm.at[pl.ds(idx_ref[0, 0, r], 1), :],
                                     yg_s.at[b, k, pl.ds(r, 1), :], sem.at[b])

    def each_row(fn):
        def body(r, c):
            fn(r)
            return c
        lax.fori_loop(0, tm, body, 0, unroll=8)

    def start(i0, i1, b):
        each_row(lambda r: (slot_row(i0, 0, r, b).start(), slot_row(i1, 1, r, b).start()))

    @pl.when(i == 0)
    def _():
        start(d0_ref, d1_ref, 0)

    @pl.when(i + 1 < pl.num_programs(0))
    def _():
        start(d0n_ref, d1n_ref, 1 - buf)

    each_row(lambda r: (slot_row(d0_ref, 0, r, buf).wait(), slot_row(d1_ref, 1, r, buf).wait()))
    w = w_ref[...]
    mix = yg_s[buf, 0] * w[:, 0:1] + yg_s[buf, 1] * w[:, 1:2]
    o_ref[...] = _layer_norm(alpha * x_ref[...] + mix, g_ref[...], b_ref[...])


def _moe_combine_deepnorm(x, ys, dest, wgt, ln_g, ln_b, alpha):
    m, d = x.shape
    tm = _tile(m, 512)
    n = m // tm
    dest3 = dest.reshape(2 * n, 1, tm)
    smem_rows = lambda imap: pl.BlockSpec((1, 1, tm), imap, memory_space=pltpu.SMEM)
    nxt = lambda i: jnp.minimum(i + 1, n - 1)
    return pl.pallas_call(
        functools.partial(_moe_combine_kernel, alpha),
        grid=(n,),
        in_specs=[smem_rows(lambda i: (i, 0, 0)),
                  smem_rows(lambda i: (n + i, 0, 0)),
                  smem_rows(lambda i: (nxt(i), 0, 0)),
                  smem_rows(lambda i: (n + nxt(i), 0, 0)),
                  pl.BlockSpec((tm, d), lambda i: (i, 0)),
                  pl.BlockSpec((tm, LANES), lambda i: (i, 0)),
                  pl.BlockSpec((1, d), lambda i: (0, 0)),
                  pl.BlockSpec((1, d), lambda i: (0, 0)),
                  pl.BlockSpec(memory_space=pl.ANY)],
        out_specs=pl.BlockSpec((tm, d), lambda i: (i, 0)),
        out_shape=jax.ShapeDtypeStruct((m, d), F32),
        scratch_shapes=[pltpu.VMEM((2, 2, tm, d), F32), pltpu.SemaphoreType.DMA((2,))],
        compiler_params=_cparams(("arbitrary",)),
        name="moe_combine_deepnorm",
    )(dest3, dest3, dest3, dest3, x, wgt, ln_g.reshape(1, d), ln_b.reshape(1, d), ys)


def _moe_deepnorm(x, router, w_gate, w_up, w_down, ln_g, ln_b, alpha):
    m = x.shape[0]
    n_exp = router.shape[1]
    tm = _tile(m, MOE_ROWS)
    idx, wgt = _router_top2(x, router)
    dest, tile_exp, n_valid, fill_lo, fill_hi = _moe_plan(idx, m, tm, n_exp)
    xs = _moe_permute(x, dest, fill_lo, fill_hi, 2 * m + n_exp * tm)
    ys = _moe_experts(xs, tile_exp, n_valid, tm, w_gate, w_up, w_down)
    return _moe_combine_deepnorm(x, ys, dest, wgt, ln_g, ln_b, alpha)


def _proj_pairs_kernel(transposed, x_ref, w_ref, o_ref, xb_s):
    @pl.when(pl.program_id(1) == 0)
    def _():
        xb_s[...] = x_ref[...].astype(BF16)

    n_slab = o_ref.shape[0]
    if transposed:
        res = _dot_nt(w_ref[...], xb_s[...])
        for q in range(n_slab):
            o_ref[q] = res[q * LANES:(q + 1) * LANES, :].astype(o_ref.dtype)
    else:
        res = _dot(xb_s[...], w_ref[...])
        for q in range(n_slab):
            o_ref[q] = res[:, q * LANES:(q + 1) * LANES].astype(o_ref.dtype)


def _proj_pairs(x, w, transposed, out_dtype=F32):
    m, d = x.shape
    n = w.shape[1]
    tm = _tile(m, 512)
    tn = _tile(n, 1024)
    n_slab = tn // LANES
    if transposed:
        wb = w.T.astype(BF16)
        w_spec = pl.BlockSpec((tn, d), lambda i, j: (j, 0))
        o_spec = pl.BlockSpec((n_slab, LANES, tm), lambda i, j: (j, 0, i))
        o_shape = jax.ShapeDtypeStruct((n // LANES, LANES, m), out_dtype)
    else:
        wb = w.astype(BF16)
        w_spec = pl.BlockSpec((d, tn), lambda i, j: (0, j))
        o_spec = pl.BlockSpec((n_slab, tm, LANES), lambda i, j: (j, i, 0))
        o_shape = jax.ShapeDtypeStruct((n // LANES, m, LANES), out_dtype)
    return pl.pallas_call(
        functools.partial(_proj_pairs_kernel, transposed),
        grid=(m // tm, n // tn),
        in_specs=[pl.BlockSpec((tm, d), lambda i, j: (i, 0)), w_spec],
        out_specs=o_spec,
        out_shape=o_shape,
        scratch_shapes=[pltpu.VMEM((tm, d), BF16)],
        compiler_params=_cparams(("parallel", "arbitrary")),
        name="proj_pairs_t" if transposed else "proj_pairs",
    )(x, wb)


def _block_mean_kernel(k_ref, o_ref):
    k = k_ref[0]
    nb = k.shape[0] // MOBA_BLOCK
    o_ref[0] = jnp.mean(k.reshape(nb, MOBA_BLOCK, LANES), axis=1)


def _block_means(k_pairs, batch):
    n_pair, m, _ = k_pairs.shape
    seq = m // batch
    nb = seq // MOBA_BLOCK
    return pl.pallas_call(
        _block_mean_kernel,
        grid=(n_pair, batch),
        in_specs=[pl.BlockSpec((1, seq, LANES), lambda p, b: (p, b, 0))],
        out_specs=pl.BlockSpec((1, nb, LANES), lambda p, b: (p, b, 0)),
        out_shape=jax.ShapeDtypeStruct((n_pair, batch * nb, LANES), F32),
        compiler_params=_cparams(("parallel", "parallel")),
        name="moba_block_means",
    )(k_pairs)


def _moba_kernel(scale, qt_ref, k_ref, vt_ref, km_ref, o_ref, neg_s, sca_s, scb_s, m_s, acc_s):
    own = pl.program_id(2)
    blk = MOBA_BLOCK
    n_slab = qt_ref.shape[0]
    nb = km_ref.shape[1]
    row = lax.broadcasted_iota(jnp.int32, (LANES, blk), 0)
    zero_q = jnp.zeros((LANES, blk), F32)
    n_iota = lax.broadcasted_iota(jnp.int32, (nb, blk), 0).astype(F32)
    past = n_iota < own.astype(F32)
    neg_inf = jnp.float32(-jnp.inf)

    streams = [(g, h) for g in range(n_slab) for h in range(2)]
    qh = []
    for s, (g, h) in enumerate(streams):
        in_head = (row < HEAD_DIM) if h == 0 else (row >= HEAD_DIM)
        q_h = jnp.where(in_head, qt_ref[g], zero_q)
        qh.append((q_h * (scale * LOG2E)).astype(BF16))
        gate = _dot_sp(_split(km_ref[g]), _split(q_h))
        gate = jnp.where(past, gate, neg_inf)
        neg = jnp.full((nb, blk), NEG_BIG, F32)
        for _ in range(min(MOBA_TOPK, nb)):
            mx = jnp.max(gate, axis=0, keepdims=True)
            idx = jnp.min(jnp.where(gate == mx, n_iota, float(nb)), axis=0, keepdims=True)
            pick = n_iota == idx
            neg = jnp.where(jnp.logical_and(pick, past), 0.0, neg)
            gate = jnp.where(pick, neg_inf, gate)
        neg_s[s] = neg

    ones_rows = jnp.ones((DEN_ROWS, blk), BF16)

    def block_scores(n, s):
        start = pl.multiple_of(n * blk, blk)
        return _dot(k_ref[streams[s][0], pl.ds(start, blk), :], qh[s])

    def absorb(n, st, bias, s):
        g, h = streams[s]
        start = pl.multiple_of(n * blk, blk)
        mx = m_s[s]
        mblk = jnp.max(st, axis=0, keepdims=True)
        if bias is not None:
            mblk = mblk + bias
        mx_new = jnp.maximum(mx, mblk)
        alpha = jnp.exp2(mx - mx_new)
        shift = mx_new if bias is None else mx_new - bias
        p = jnp.exp2(st - shift).astype(BF16)
        vtb = vt_ref[g, h * HEAD_DIM:(h + 1) * HEAD_DIM, pl.ds(start, blk)]
        m_s[s] = mx_new
        acc_s[s] = alpha * acc_s[s] + _dot(jnp.concatenate([vtb, ones_rows], axis=0), p)

    kpos = lax.broadcasted_iota(jnp.int32, (blk, blk), 0)
    qpos = lax.broadcasted_iota(jnp.int32, (blk, blk), 1)
    causal = kpos <= qpos
    own_scores = [block_scores(own, s) for s in range(len(streams))]
    for s in range(len(streams)):
        sca_s[s] = block_scores(0, s)
        m_s[s] = jnp.full((1, blk), neg_inf, F32)
        acc_s[s] = jnp.zeros((HEAD_DIM + DEN_ROWS, blk), F32)
    for s in range(len(streams)):
        absorb(own, jnp.where(causal, own_scores[s], neg_inf), None, s)

    def body(j, carry):
        n0 = 2 * j
        n1 = n0 + 1
        n2 = jnp.minimum(n0 + 2, nb - 1)
        for s in range(len(streams)):
            scb_s[s] = block_scores(n1, s)
            absorb(n0, sca_s[s], neg_s[s, pl.ds(n0, 1), :], s)
        for s in range(len(streams)):
            sca_s[s] = block_scores(n2, s)
            absorb(n1, scb_s[s], neg_s[s, pl.ds(n1, 1), :], s)
        return carry

    lax.fori_loop(0, (own + 1) // 2, body, 0)
    for g in range(n_slab):
        a0, a1 = acc_s[2 * g], acc_s[2 * g + 1]
        o_t = jnp.concatenate([a0[:HEAD_DIM] / a0[HEAD_DIM:HEAD_DIM + 1],
                               a1[:HEAD_DIM] / a1[HEAD_DIM:HEAD_DIM + 1]], axis=0)
        o_ref[g] = o_t.T.astype(o_ref.dtype)


def _moba_attention(q_t, k, v_t, k_means, batch):
    n_pair, _, m = q_t.shape
    seq = m // batch
    nb = seq // MOBA_BLOCK
    g = _tile(n_pair, MOBA_SLABS)
    return pl.pallas_call(
        functools.partial(_moba_kernel, HEAD_DIM ** -0.5),
        grid=(n_pair // g, batch, nb),
        in_specs=[pl.BlockSpec((g, LANES, MOBA_BLOCK), lambda p, b, i: (p, 0, b * nb + i)),
                  pl.BlockSpec((g, seq, LANES), lambda p, b, i: (p, b, 0),
                               pipeline_mode=pl.Buffered(1)),
                  pl.BlockSpec((g, LANES, seq), lambda p, b, i: (p, 0, b),
                               pipeline_mode=pl.Buffered(1)),
                  pl.BlockSpec((g, nb, LANES), lambda p, b, i: (p, b, 0))],
        out_specs=pl.BlockSpec((g, MOBA_BLOCK, LANES), lambda p, b, i: (p, b * nb + i, 0)),
        out_shape=jax.ShapeDtypeStruct((n_pair, m, LANES), BF16),
        scratch_shapes=[pltpu.VMEM((2 * g, nb, MOBA_BLOCK), F32),
                        pltpu.VMEM((2 * g, MOBA_BLOCK, MOBA_BLOCK), F32),
                        pltpu.VMEM((2 * g, MOBA_BLOCK, MOBA_BLOCK), F32),
                        pltpu.VMEM((2 * g, 1, MOBA_BLOCK), F32),
                        pltpu.VMEM((2 * g, HEAD_DIM + DEN_ROWS, MOBA_BLOCK), F32)],
        compiler_params=_cparams(("parallel", "parallel", "arbitrary")),
        name="moba_attention",
    )(q_t, k, v_t, k_means)


def kernel(x, rwkv_mu, rwkv_w_rkv, rwkv_w_out, rwkv_decay_w0, rwkv_decay_w1, rwkv_decay_w2, rwkv_iclr_a0, rwkv_iclr_a1, rwkv_iclr_a2, rwkv_vres_v0, rwkv_vres_v1, rwkv_vres_v2, rwkv_gate_g1, rwkv_gate_g2, rwkv_k_k, rwkv_k_a, rwkv_r_k, rwkv_gn_g, rwkv_gn_b, moba_w_k, moba_w_v, moba_w_q, moba_w_o, ffn_w_gate, ffn_w_up, ffn_w_down, moe_router, moe_w_gate, moe_w_up, moe_w_down, ln_g, ln_b):
    batch, seq, d = x.shape
    assert d % (2 * LANES) == 0 and seq % MOBA_BLOCK == 0 and seq % WKV_CHUNK == 0
    depth = ln_g.shape[0]
    n_rwkv = rwkv_mu.shape[0]
    alpha = (2.0 * depth) ** 0.25
    h = x.reshape(batch * seq, d)
    v_first = None
    kv = None
    for layer in range(depth):
        if layer < n_rwkv:
            i = layer
            vres = None if i == 0 else (rwkv_vres_v0[i - 1], rwkv_vres_v1[i - 1], rwkv_vres_v2[i - 1])
            r, e, k, v, a, b, g = _rwkv_proj(
                h, seq, rwkv_mu[i], rwkv_w_rkv[i], rwkv_decay_w1[i], rwkv_iclr_a1[i],
                rwkv_gate_g1[i], rwkv_decay_w2[i], rwkv_iclr_a2[i], rwkv_gate_g2[i],
                rwkv_decay_w0[i], rwkv_iclr_a0[i], rwkv_k_k[i], rwkv_k_a[i],
                vres=vres, v_first=v_first)
            if i == 0:
                v_first = v
            y = _wkv_scan(r, e, k, v, a, b, batch)
            mix = _rwkv_post(y, r, k, v, g, rwkv_r_k[i], rwkv_gn_g[i], rwkv_gn_b[i])
            w_out = rwkv_w_out[i]
        else:
            jdx = layer - n_rwkv
            k_pairs, v_t, k_means = kv
            q_t = _proj_pairs(h, moba_w_q[jdx], transposed=True)
            mix = _moba_attention(q_t, k_pairs, v_t, k_means, batch)
            w_out = moba_w_o[jdx]
        h = _out_proj_deepnorm(mix, w_out, h, ln_g[layer, 0], ln_b[layer, 0], alpha)
        ex = layer // 2
        if layer % 2 == 0:
            h = _ffn_deepnorm(h, _layer_bf16(ffn_w_gate, ex), _layer_bf16(ffn_w_up, ex),
                              _layer_bf16(ffn_w_down, ex), ln_g[layer, 1], ln_b[layer, 1], alpha)
        else:
            h = _moe_deepnorm(h, moe_router[ex], _layer_bf16(moe_w_gate, ex),
                              _layer_bf16(moe_w_up, ex), _layer_bf16(moe_w_down, ex),
                              ln_g[layer, 1], ln_b[layer, 1], alpha)
        if layer == n_rwkv - 1:
            k_pairs = _proj_pairs(h, moba_w_k, transposed=False)
            v_t = _proj_pairs(h, moba_w_v, transposed=True, out_dtype=BF16)
            kv = (k_pairs.astype(BF16), v_t, _block_means(k_pairs, batch))
    return h.reshape(batch, seq, d)
```

```python
import functools
import math

import jax
import jax.numpy as jnp
from jax import lax
from jax.experimental import pallas as pl
from jax.experimental.pallas import tpu as pltpu

HEAD_DIM = 64
LANES = 128
MXU_DEPTH = 256
GN_EPS = 64e-5
LN_EPS = 1e-5
MOBA_BLOCK = 256
MOBA_TOPK = 3
MOE_ROWS = 512
FFN_TILE = 2816
CAST_BLOCK_ELEMS = 1 << 20
WKV_CHUNK = 64
WKV_SLABS = 8
NEG_BIG = -1e30
MOBA_SLABS = 8
DEN_ROWS = 16
LOG2E = 1.4426950408889634

F32 = jnp.float32
BF16 = jnp.bfloat16
VMEM_LIMIT = 56 * 1024 * 1024


def _cparams(sem):
    return pltpu.CompilerParams(dimension_semantics=sem, vmem_limit_bytes=VMEM_LIMIT)


def _dot(a, b):
    return jnp.dot(a, b, preferred_element_type=F32)


def _dot_nt(a, b):
    return lax.dot_general(a, b, (((1,), (1,)), ((), ())), preferred_element_type=F32)


def _split(x):
    hi = x.astype(BF16)
    lo = (x - hi.astype(F32)).astype(BF16)
    return hi, lo


def _dot_sp(a, b, nt=False):
    d = _dot_nt if nt else _dot
    return d(a[0], b[0]) + (d(a[0], b[1]) + d(a[1], b[0]))


def _dot_exact_rhs(a, b_exact, nt=False):
    d = _dot_nt if nt else _dot
    hi, lo = _split(a)
    return d(hi, b_exact) + d(lo, b_exact)


def _sigmoid(x):
    return 1.0 / (1.0 + jnp.exp(-x))


def _layer_norm(y, g, b):
    mu = jnp.mean(y, axis=-1, keepdims=True)
    yc = y - mu
    var = jnp.mean(yc * yc, axis=-1, keepdims=True)
    return yc * lax.rsqrt(var + LN_EPS) * g + b


def _head_blockdiag(n):
    i = jnp.arange(n) // HEAD_DIM
    return (i[:, None] == i[None, :]).astype(BF16)


def _tile(n, want):
    t = min(n, want)
    assert n % t == 0, (n, want)
    return t


def _cast_kernel(w_ref, o_ref):
    o_ref[...] = w_ref[...].astype(o_ref.dtype)


def _layer_bf16(w, layer):
    shape = w.shape[1:]
    c = shape[-1]
    rows = math.prod(shape[:-1])
    tr = rows
    for cand in range(min(rows, CAST_BLOCK_ELEMS // c) // 16, 0, -1):
        if rows % (cand * 16) == 0:
            tr = cand * 16
            break
    out = pl.pallas_call(
        _cast_kernel,
        grid=(rows // tr,),
        in_specs=[pl.BlockSpec((1, tr, c), lambda i: (layer, i, 0))],
        out_specs=pl.BlockSpec((1, tr, c), lambda i: (0, i, 0)),
        out_shape=jax.ShapeDtypeStruct((1, rows, c), BF16),
        compiler_params=_cparams(("parallel",)),
        name="weight_to_bf16",
    )(w.reshape(w.shape[0], rows, c))
    return out.reshape(shape)


def _rwkv_proj_kernel(has_vres, steps_per_seq, *refs):
    if has_vres:
        (x_ref, xp_ref, mu_ref, wrkv_ref, wd1_ref, wa1_ref, wg1_ref, wv1_ref,
         wd2_ref, wa2_ref, wg2_ref, wv2_ref, w0_ref, a0_ref, v0_ref, kk_ref, ka_ref,
         bd_ref, vf_ref,
         r_out, e_out, k_out, v_out, a_out, b_out, g_out,
         xm_s, hd_s, ha_s, hg_s, hv_s) = refs
    else:
        (x_ref, xp_ref, mu_ref, wrkv_ref, wd1_ref, wa1_ref, wg1_ref,
         wd2_ref, wa2_ref, wg2_ref, w0_ref, a0_ref, kk_ref, ka_ref,
         bd_ref,
         r_out, e_out, k_out, v_out, a_out, b_out, g_out,
         xm_s, hd_s, ha_s, hg_s) = refs
    i = pl.program_id(0)
    j = pl.program_id(1)

    @pl.when(j == 0)
    def _():
        x = x_ref[...]
        tm = x.shape[0]
        prev_row = jnp.where(i % steps_per_seq == 0, 0.0, xp_ref[7:8, :])
        rolled = pltpu.roll(x, 1, axis=0)
        row = lax.broadcasted_iota(jnp.int32, (tm, 1), 0)
        x_prev = jnp.where(row == 0, prev_row, rolled)
        xx = x_prev - x
        for c in range(3):
            xm_s[c] = (x + xx * mu_ref[c:c + 1, :]).astype(BF16)
        xw = (x + xx * mu_ref[3:4, :]).astype(BF16)
        xa = (x + xx * mu_ref[4:5, :]).astype(BF16)
        xg = (x + xx * mu_ref[5:6, :]).astype(BF16)
        hd_s[...] = jnp.tanh(_dot(xw, wd1_ref[...])).astype(BF16)
        ha_s[...] = _dot(xa, wa1_ref[...]).astype(BF16)
        hg_s[...] = _sigmoid(_dot(xg, wg1_ref[...])).astype(BF16)
        if has_vres:
            hv_s[...] = _dot(xm_s[2], wv1_ref[...]).astype(BF16)

    r = _dot(xm_s[0], wrkv_ref[0])
    k = _dot(xm_s[1], wrkv_ref[1])
    v = _dot(xm_s[2], wrkv_ref[2])
    z = w0_ref[...] + _dot(hd_s[...], wd2_ref[...])
    nz = -z
    softplus = jnp.maximum(nz, 0.0) + jnp.log(1.0 + jnp.exp(-jnp.abs(nz)))
    e = jnp.exp(-softplus - 0.5)
    a = _sigmoid(a0_ref[...] + _dot(ha_s[...], wa2_ref[...]))
    g = _dot(hg_s[...], wg2_ref[...])
    n_slab = r.shape[1] // LANES
    if has_vres:
        vf = jnp.concatenate([vf_ref[q] for q in range(n_slab)], axis=1)
        v = v + (vf - v) * _sigmoid(v0_ref[...] + _dot(hv_s[...], wv2_ref[...]))
    kk = k * kk_ref[...]
    ss = _dot_exact_rhs(kk * kk, bd_ref[...])
    kk = kk * lax.rsqrt(jnp.maximum(ss, 1e-24))
    k = k * (1.0 + (a - 1.0) * ka_ref[...])
    for q in range(n_slab):
        sl = slice(q * LANES, (q + 1) * LANES)
        r_out[q] = r[:, sl].astype(r_out.dtype)
        e_out[q] = e[:, sl]
        k_out[q] = k[:, sl].astype(k_out.dtype)
        v_out[q] = v[:, sl].astype(v_out.dtype)
        a_out[q] = (-kk)[:, sl].astype(a_out.dtype)
        b_out[q] = (kk * a)[:, sl].astype(b_out.dtype)
        g_out[q] = g[:, sl].astype(g_out.dtype)


def _rwkv_proj(x, seq_len, mu, w_rkv, wd1, wa1, wg1, wd2, wa2, wg2, w0, a0, k_k, k_a,
               vres=None, v_first=None):
    m, d = x.shape
    tm = _tile(seq_len, 512)
    tn = _tile(d, 256)
    n_slab = tn // LANES
    has_vres = vres is not None
    row = lambda a: a.reshape(1, d)
    full = lambda a: pl.BlockSpec(a.shape, lambda i, j: (0,) * a.ndim)
    colblk = lambda rows: pl.BlockSpec((rows, tn), lambda i, j: (0, j))
    bd = _head_blockdiag(tn)
    wd1, wa1, wg1 = wd1.astype(BF16), wa1.astype(BF16), wg1.astype(BF16)
    wd2, wa2, wg2 = wd2.astype(BF16), wa2.astype(BF16), wg2.astype(BF16)
    w_rkv = w_rkv.astype(BF16)
    args = [x, x, mu, w_rkv, wd1, wa1, wg1]
    specs = [pl.BlockSpec((tm, d), lambda i, j: (i, 0)),
             pl.BlockSpec((8, d), lambda i, j: (jnp.maximum(i * (tm // 8) - 1, 0), 0)),
             full(mu),
             pl.BlockSpec((3, d, tn), lambda i, j: (0, 0, j)),
             full(wd1), full(wa1), full(wg1)]
    if has_vres:
        v0, wv1, wv2 = vres
        wv1, wv2 = wv1.astype(BF16), wv2.astype(BF16)
        args += [wv1]
        specs += [full(wv1)]
    args += [wd2, wa2, wg2]
    specs += [colblk(wd2.shape[0]), colblk(wa2.shape[0]), colblk(wg2.shape[0])]
    if has_vres:
        args += [wv2]
        specs += [colblk(wv2.shape[0])]
    args += [row(w0), row(a0)]
    specs += [colblk(1), colblk(1)]
    if has_vres:
        args += [row(v0)]
        specs += [colblk(1)]
    args += [row(k_k), row(k_a), bd]
    specs += [colblk(1), colblk(1), full(bd)]
    slab_spec = pl.BlockSpec((n_slab, tm, LANES), lambda i, j: (j, i, 0))
    if has_vres:
        args += [v_first]
        specs += [slab_spec]
    slab = lambda dt: jax.ShapeDtypeStruct((d // LANES, m, LANES), dt)
    scratch = [pltpu.VMEM((3, tm, d), BF16),
               pltpu.VMEM((tm, wd1.shape[1]), BF16),
               pltpu.VMEM((tm, wa1.shape[1]), BF16),
               pltpu.VMEM((tm, wg1.shape[1]), BF16)]
    if has_vres:
        scratch.append(pltpu.VMEM((tm, wv1.shape[1]), BF16))
    return pl.pallas_call(
        functools.partial(_rwkv_proj_kernel, has_vres, seq_len // tm),
        grid=(m // tm, d // tn),
        in_specs=specs,
        out_specs=[slab_spec] * 7,
        out_shape=[slab(BF16), slab(F32)] + [slab(BF16)] * 5,
        scratch_shapes=scratch,
        compiler_params=_cparams(("parallel", "arbitrary")),
        name="rwkv_proj",
    )(*args)


def _wkv_kernel(r_ref, e_ref, k_ref, v_ref, a_ref, b_ref, y_ref, s_ref):
    c = WKV_CHUNK
    n_slab, n_batch, tb, _ = r_ref.shape
    n_chunks = tb // c

    @pl.when(pl.program_id(1) == 0)
    def _():
        s_ref[...] = jnp.zeros_like(s_ref)

    lane = lax.broadcasted_iota(jnp.int32, (c, LANES), 1)
    t_idx = lax.broadcasted_iota(jnp.int32, (c, LANES), 0)
    head0 = lane < HEAD_DIM
    s_idx = jnp.bitwise_and(lane, HEAD_DIM - 1)
    strict = s_idx < t_idx
    incl = s_idx <= t_idx
    rr = lax.broadcasted_iota(jnp.int32, (LANES, LANES), 0)
    cc = lax.broadcasted_iota(jnp.int32, (LANES, LANES), 1)
    same_head = (rr < HEAD_DIM) == (cc < HEAD_DIM)

    def bd(x):
        z = jnp.zeros_like(x)
        return jnp.concatenate([jnp.where(head0, x, z), jnp.where(head0, z, x)], axis=0)

    def cat(a, b, axis=0):
        return jnp.concatenate([a, b], axis=axis)

    def chunk(ci, carry):
        stages = [one_slab(ci, p, bb) for p in range(n_slab) for bb in range(n_batch)]
        while stages:
            stages = [g for g in stages if next(g, "done") != "done"]
        return carry

    def one_slab(ci, p, bb):
        sl = pl.ds(pl.multiple_of(ci * c, c), c)
        r = r_ref[p, bb, sl, :]
        e = e_ref[p, bb, sl, :]
        k = k_ref[p, bb, sl, :]
        v = v_ref[p, bb, sl, :]
        a = a_ref[p, bb, sl, :]
        b = b_ref[p, bb, sl, :]
        s0 = s_ref[p * n_batch + bb]
        cum = _cumsum_rows(e, t_idx)
        yield
        tot = cum[c - 1:c, :]
        ar = cat(a * jnp.exp(e - cum), r * jnp.exp(-cum)).astype(BF16)
        ec = jnp.exp(cum)
        bt = (b * ec).astype(BF16)
        kt = (k * ec).astype(BF16)
        eh = jnp.exp(cum - tot)
        bk = cat(b * eh, k * eh).astype(BF16)
        bdv = bd(v.astype(BF16))
        gram = _dot_nt(ar, cat(bd(bt), bd(kt)))
        yield
        zero = jnp.zeros((c, LANES), F32)
        l_ab = jnp.where(strict, gram[:c, :LANES], zero)
        n_ak = jnp.where(strict, gram[:c, LANES:], zero)
        m_rb = jnp.where(incl, gram[c:, :LANES], zero)
        m_rk = jnp.where(incl, gram[c:, LANES:], zero)
        xy = _dot_nt(ar, s0.astype(BF16)) + _dot(cat(n_ak, m_rk).astype(BF16), bdv)
        x = xy[:c]
        y0 = xy[c:]
        yield
        lp = l_ab.astype(BF16)
        n_steps = int(math.log2(c))
        for step in range(n_steps):
            bdx = bd(x.astype(BF16))
            if step + 1 < n_steps:
                t = _dot(lp, cat(bdx, bd(lp), axis=1))
                x = x + t[:, :LANES]
                lp = t[:, LANES:].astype(BF16)
            else:
                x = x + _dot(lp, bdx)
            yield
        y_ref[p, bb, sl, :] = y0 + _dot(m_rb.astype(BF16), bd(x.astype(BF16)))
        uv_t = cat(x, v.astype(F32)).T.astype(BF16)
        upd = _dot(uv_t, bk)
        s_ref[p * n_batch + bb] = (s0 * jnp.exp(-tot)
                                   + jnp.where(same_head, upd, jnp.zeros_like(upd)))
        yield

    lax.fori_loop(0, n_chunks, chunk, 0)


def _cumsum_rows(e, row_idx):
    cum = e
    shift = 1
    while shift < e.shape[0]:
        cum = cum + jnp.where(row_idx >= shift, pltpu.roll(cum, shift, axis=0), 0.0)
        shift *= 2
    return cum


def _wkv_scan(r, e, k, v, a, b, batch):
    n_pair, m, _ = r.shape
    seq = m // batch
    tb = _tile(seq, 256)
    n_slab = _tile(n_pair, WKV_SLABS)
    view = lambda t: t.reshape(n_pair, batch, seq, LANES)
    spec = pl.BlockSpec((n_slab, batch, tb, LANES), lambda p, t: (p, 0, t, 0))
    y = pl.pallas_call(
        _wkv_kernel,
        grid=(n_pair // n_slab, seq // tb),
        in_specs=[spec] * 6,
        out_specs=spec,
        out_shape=jax.ShapeDtypeStruct((n_pair, batch, seq, LANES), F32),
        scratch_shapes=[pltpu.VMEM((n_slab * batch, LANES, LANES), F32)],
        compiler_params=_cparams(("parallel", "arbitrary")),
        name="wkv_scan",
    )(view(r), view(e), view(k), view(v), view(a), view(b))
    return y.reshape(n_pair, m, LANES)


def _rwkv_post_kernel(y_ref, r_ref, k_ref, v_ref, g_ref, rk_ref, gg_ref, gb_ref, bd_ref, z_ref):
    y = y_ref[0]
    bd2 = bd_ref[...]
    bdm = bd2[:LANES, :LANES]
    inv_n = 1.0 / HEAD_DIM
    rk = r_ref[0].astype(F32) * k_ref[0].astype(F32) * rk_ref[0]
    sums = _dot_exact_rhs(jnp.concatenate([y, rk], axis=1), bd2)
    yc = y - sums[:, :LANES] * inv_n
    var = _dot_exact_rhs(yc * yc, bdm) * inv_n
    yn = yc * lax.rsqrt(var + GN_EPS) * gg_ref[0] + gb_ref[0]
    bonus = sums[:, LANES:] * v_ref[0].astype(F32)
    z_ref[0] = ((yn + bonus) * g_ref[0].astype(F32)).astype(z_ref.dtype)


def _rwkv_post(y, r, k, v, g, r_k, gn_g, gn_b):
    n_pair, m, _ = y.shape
    tm = _tile(m, 2048)
    spec = pl.BlockSpec((1, tm, LANES), lambda p, i: (p, i, 0))
    pspec = pl.BlockSpec((1, 1, LANES), lambda p, i: (p, 0, 0))
    bd = _head_blockdiag(2 * LANES)
    slab = lambda a: a.reshape(n_pair, 1, LANES)
    return pl.pallas_call(
        _rwkv_post_kernel,
        grid=(n_pair, m // tm),
        in_specs=[spec] * 5 + [pspec] * 3 + [pl.BlockSpec(bd.shape, lambda p, i: (0, 0))],
        out_specs=spec,
        out_shape=jax.ShapeDtypeStruct((n_pair, m, LANES), BF16),
        compiler_params=_cparams(("parallel", "parallel")),
        name="rwkv_post",
    )(y, r, k, v, g, slab(r_k), slab(gn_g), slab(gn_b), bd)


def _out_proj_kernel(alpha, z_ref, w_ref, x_ref, g_ref, b_ref, o_ref):
    per = w_ref.shape[1] // LANES
    acc = None
    for c in range(w_ref.shape[0]):
        zc = jnp.concatenate([z_ref[c * per + q] for q in range(per)], axis=1)
        part = _dot(zc, w_ref[c])
        acc = part if acc is None else acc + part
    o_ref[...] = _layer_norm(alpha * x_ref[...] + acc, g_ref[...], b_ref[...])


def _out_proj_deepnorm(z, w, x, ln_g, ln_b, alpha):
    n_pair, m, _ = z.shape
    d = w.shape[1]
    tm = _tile(m, 1024)
    kc = _tile(d, MXU_DEPTH)
    w3 = w.astype(BF16).reshape(d // kc, kc, d)
    return pl.pallas_call(
        functools.partial(_out_proj_kernel, alpha),
        grid=(m // tm,),
        in_specs=[pl.BlockSpec((n_pair, tm, LANES), lambda i: (0, i, 0)),
                  pl.BlockSpec(w3.shape, lambda i: (0, 0, 0)),
                  pl.BlockSpec((tm, d), lambda i: (i, 0)),
                  pl.BlockSpec((1, d), lambda i: (0, 0)),
                  pl.BlockSpec((1, d), lambda i: (0, 0))],
        out_specs=pl.BlockSpec((tm, d), lambda i: (i, 0)),
        out_shape=jax.ShapeDtypeStruct((m, d), F32),
        compiler_params=_cparams(("parallel",)),
        name="out_proj_deepnorm",
    )(z, w3, x, ln_g.reshape(1, d), ln_b.reshape(1, d))


def _ff_tile(ff, want):
    for cand in range(min(ff, want) // LANES, 0, -1):
        if ff % (cand * LANES) == 0:
            return cand * LANES
    return ff


def _swiglu_partial(xb, wg, wu, wd):
    h1 = _dot(xb, wg)
    h2 = _dot(xb, wu)
    act = h1 * _sigmoid(h1) * h2
    return _dot(act.astype(BF16), wd)


def _ffn_kernel(alpha, x_ref, wg_ref, wu_ref, wd_ref, g_ref, b_ref, o_ref, xb_s, acc_s):
    f = pl.program_id(1)

    @pl.when(f == 0)
    def _():
        xb_s[...] = x_ref[...].astype(BF16)
        acc_s[...] = jnp.zeros_like(acc_s)

    acc_s[...] += _swiglu_partial(xb_s[...], wg_ref[...], wu_ref[...], wd_ref[...])

    @pl.when(f == pl.num_programs(1) - 1)
    def _():
        o_ref[...] = _layer_norm(alpha * x_ref[...] + acc_s[...], g_ref[...], b_ref[...])


def _ffn_deepnorm(x, w_gate, w_up, w_down, ln_g, ln_b, alpha):
    m, d = x.shape
    ff = w_gate.shape[1]
    tm = _tile(m, 512)
    tf = _ff_tile(ff, FFN_TILE)
    wmode = dict(pipeline_mode=pl.Buffered(1)) if tf == ff else {}
    return pl.pallas_call(
        functools.partial(_ffn_kernel, alpha),
        grid=(m // tm, ff // tf),
        in_specs=[pl.BlockSpec((tm, d), lambda i, f: (i, 0)),
                  pl.BlockSpec((d, tf), lambda i, f: (0, f), **wmode),
                  pl.BlockSpec((d, tf), lambda i, f: (0, f), **wmode),
                  pl.BlockSpec((tf, d), lambda i, f: (f, 0), **wmode),
                  pl.BlockSpec((1, d), lambda i, f: (0, 0)),
                  pl.BlockSpec((1, d), lambda i, f: (0, 0))],
        out_specs=pl.BlockSpec((tm, d), lambda i, f: (i, 0)),
        out_shape=jax.ShapeDtypeStruct((m, d), F32),
        scratch_shapes=[pltpu.VMEM((tm, d), BF16), pltpu.VMEM((tm, d), F32)],
        compiler_params=_cparams(("parallel", "arbitrary")),
        name="ffn_deepnorm",
    )(x, w_gate.astype(BF16), w_up.astype(BF16), w_down.astype(BF16),
      ln_g.reshape(1, d), ln_b.reshape(1, d))


def _router_kernel(n_exp, x_ref, w_ref, idx_ref, wgt_ref):
    logits = _dot_sp(_split(x_ref[...]), _split(w_ref[...]))
    lane = lax.broadcasted_iota(jnp.int32, logits.shape, 1).astype(F32)
    neg_inf = jnp.float32(-jnp.inf)
    lg = jnp.where(lane < n_exp, logits, neg_inf)
    m1 = jnp.max(lg, axis=1, keepdims=True)
    i1 = jnp.min(jnp.where(lg == m1, lane, float(LANES)), axis=1, keepdims=True)
    lg2 = jnp.where(lane == i1, neg_inf, lg)
    m2 = jnp.max(lg2, axis=1, keepdims=True)
    i2 = jnp.min(jnp.where(lg2 == m2, lane, float(LANES)), axis=1, keepdims=True)
    e2 = jnp.exp(m2 - m1)
    den = 1.0 + e2
    idx_ref[...] = jnp.where(lane == 0.0, i1, jnp.where(lane == 1.0, i2, 0.0)).astype(jnp.int32)
    wgt_ref[...] = jnp.where(lane == 0.0, 1.0 / den, jnp.where(lane == 1.0, e2 / den, 0.0))


def _router_top2(x, router):
    m, d = x.shape
    n_exp = router.shape[1]
    tm = _tile(m, 512)
    w = jnp.pad(router, ((0, 0), (0, LANES - n_exp)))
    out_spec = pl.BlockSpec((tm, LANES), lambda i: (i, 0))
    return pl.pallas_call(
        functools.partial(_router_kernel, n_exp),
        grid=(m // tm,),
        in_specs=[pl.BlockSpec((tm, d), lambda i: (i, 0)),
                  pl.BlockSpec((d, LANES), lambda i: (0, 0))],
        out_specs=[out_spec, out_spec],
        out_shape=[jax.ShapeDtypeStruct((m, LANES), jnp.int32),
                   jax.ShapeDtypeStruct((m, LANES), F32)],
        compiler_params=_cparams(("parallel",)),
        name="moe_router",
    )(x, w)


def _moe_plan(idx, m, tm, n_exp):
    n_ent = 2 * m
    expert = jnp.concatenate([idx[:, 0], idx[:, 1]])
    onehot = (expert[:, None] == jnp.arange(n_exp, dtype=jnp.int32)[None, :]).astype(jnp.int32)
    csum = jnp.cumsum(onehot, axis=0)
    counts = csum[-1]
    rank = jnp.sum(csum * onehot, axis=1) - 1
    padded = ((counts + tm - 1) // tm) * tm
    ends = jnp.cumsum(padded)
    starts = ends - padded
    dest = jnp.sum(starts[None, :] * onehot, axis=1) + rank
    n_tiles = n_ent // tm + n_exp
    tile_start = jnp.arange(n_tiles, dtype=jnp.int32) * tm
    tile_exp = jnp.minimum(jnp.sum((tile_start[:, None] >= ends[None, :]).astype(jnp.int32), axis=1),
                           n_exp - 1)
    n_valid = jnp.clip(starts[tile_exp] + counts[tile_exp] - tile_start, 0, tm)
    fill_lo = jnp.concatenate([starts + counts, ends[-1:]])
    fill_hi = jnp.concatenate([ends, jnp.full((1,), n_tiles * tm, jnp.int32)])
    return dest, tile_exp, n_valid, fill_lo, fill_hi


def _moe_permute_kernel(lo_ref, hi_ref, dest_ref, x_ref, xs_hbm, sem):
    i = pl.program_id(0)
    te = dest_ref.shape[2]

    def entry_row(r):
        return pltpu.make_async_copy(x_ref.at[pl.ds(r, 1), :],
                                     xs_hbm.at[pl.ds(dest_ref[0, 0, r], 1), :], sem.at[0])

    def filler_row(r):
        return pltpu.make_async_copy(x_ref.at[pl.ds(0, 1), :], xs_hbm.at[pl.ds(r, 1), :], sem.at[1])

    def each(lo, hi, fn, unroll=1):
        def body(r, c):
            fn(r)
            return c
        lax.fori_loop(lo, hi, body, 0, unroll=unroll)

    each(0, te // 2, lambda q: (entry_row(2 * q).start(priority=0),
                                entry_row(2 * q + 1).start(priority=1)), unroll=4)

    @pl.when(i == 0)
    def _():
        for e in range(lo_ref.shape[0]):
            each(lo_ref[e], hi_ref[e], lambda r: filler_row(r).start())
        for e in range(lo_ref.shape[0]):
            each(lo_ref[e], hi_ref[e], lambda r: filler_row(r).wait())

    each(0, te, lambda r: entry_row(r).wait(), unroll=8)


def _moe_permute(x, dest, fill_lo, fill_hi, n_rows):
    m, d = x.shape
    te = _tile(m, 512)
    n_x = m // te
    dest3 = dest.reshape(-1, 1, te)
    grid_spec = pltpu.PrefetchScalarGridSpec(
        num_scalar_prefetch=2,
        grid=(dest3.shape[0],),
        in_specs=[pl.BlockSpec((1, 1, te), lambda i, lo, hi: (i, 0, 0), memory_space=pltpu.SMEM),
                  pl.BlockSpec((te, d), lambda i, lo, hi: (i % n_x, 0))],
        out_specs=pl.BlockSpec(memory_space=pl.ANY),
        scratch_shapes=[pltpu.SemaphoreType.DMA((2,))],
    )
    return pl.pallas_call(
        _moe_permute_kernel,
        grid_spec=grid_spec,
        out_shape=jax.ShapeDtypeStruct((n_rows, d), F32),
        compiler_params=_cparams(("arbitrary",)),
        name="moe_permute",
    )(fill_lo, fill_hi, dest3, x)


def _moe_experts_kernel(te_ref, nv_ref, x_ref, wg_ref, wu_ref, wd_ref, y_ref, xb_s, acc_s):
    t = pl.program_id(0)
    f = pl.program_id(1)

    @pl.when(f == 0)
    def _():
        xb_s[...] = x_ref[...].astype(BF16)
        acc_s[...] = jnp.zeros_like(acc_s)

    @pl.when(nv_ref[t] > 0)
    def _():
        acc_s[...] += _swiglu_partial(xb_s[...], wg_ref[0], wu_ref[0], wd_ref[0])

    @pl.when(f == pl.num_programs(1) - 1)
    def _():
        y_ref[...] = acc_s[...]


def _moe_experts(xs, tile_exp, n_valid, tm, w_gate, w_up, w_down):
    n_rows, d = xs.shape
    ff = w_gate.shape[2]
    tf = _ff_tile(ff, 1792)
    grid_spec = pltpu.PrefetchScalarGridSpec(
        num_scalar_prefetch=2,
        grid=(n_rows // tm, ff // tf),
        in_specs=[pl.BlockSpec((tm, d), lambda t, f, te, nv: (t, 0)),
                  pl.BlockSpec((1, d, tf), lambda t, f, te, nv: (te[t], 0, f)),
                  pl.BlockSpec((1, d, tf), lambda t, f, te, nv: (te[t], 0, f)),
                  pl.BlockSpec((1, tf, d), lambda t, f, te, nv: (te[t], f, 0))],
        out_specs=pl.BlockSpec((tm, d), lambda t, f, te, nv: (t, 0)),
        scratch_shapes=[pltpu.VMEM((tm, d), BF16), pltpu.VMEM((tm, d), F32)],
    )
    return pl.pallas_call(
        _moe_experts_kernel,
        grid_spec=grid_spec,
        out_shape=jax.ShapeDtypeStruct((n_rows, d), F32),
        compiler_params=_cparams(("parallel", "arbitrary")),
        name="moe_experts",
    )(tile_exp, n_valid, xs, w_gate.astype(BF16), w_up.astype(BF16), w_down.astype(BF16))


def _moe_combine_kernel(alpha, d0_ref, d1_ref, d0n_ref, d1n_ref, x_ref, w_ref, g_ref, b_ref,
                        ys_hbm, o_ref, yg_s, sem):
    i = pl.program_id(0)
    tm = x_ref.shape[0]
    buf = i % 2

    def slot_row(idx_ref, k, r, b):
        return pltpu.make_async_copy(ys_hbm.at[pl.ds(idx_ref[0, 0, r], 1), :],
                                     yg_s.at[b, k, pl.ds(r, 1), :], sem.at[b])

    def each_row(fn):
        def body(r, c):
            fn(r)
            return c
        lax.fori_loop(0, tm, body, 0, unroll=8)

    def start(i0, i1, b):
        each_row(lambda r: (slot_row(i0, 0, r, b).start(priority=0),
                            slot_row(i1, 1, r, b).start(priority=1)))

    @pl.when(i == 0)
    def _():
        start(d0_ref, d1_ref, 0)

    @pl.when(i + 1 < pl.num_programs(0))
    def _():
        start(d0n_ref, d1n_ref, 1 - buf)

    each_row(lambda r: (slot_row(d0_ref, 0, r, buf).wait(), slot_row(d1_ref, 1, r, buf).wait()))
    w = w_ref[...]
    mix = yg_s[buf, 0] * w[:, 0:1] + yg_s[buf, 1] * w[:, 1:2]
    o_ref[...] = _layer_norm(alpha * x_ref[...] + mix, g_ref[...], b_ref[...])


def _moe_combine_deepnorm(x, ys, dest, wgt, ln_g, ln_b, alpha):
    m, d = x.shape
    tm = _tile(m, 512)
    n = m // tm
    dest3 = dest.reshape(2 * n, 1, tm)
    smem_rows = lambda imap: pl.BlockSpec((1, 1, tm), imap, memory_space=pltpu.SMEM)
    nxt = lambda i: jnp.minimum(i + 1, n - 1)
    return pl.pallas_call(
        functools.partial(_moe_combine_kernel, alpha),
        grid=(n,),
        in_specs=[smem_rows(lambda i: (i, 0, 0)),
                  smem_rows(lambda i: (n + i, 0, 0)),
                  smem_rows(lambda i: (nxt(i), 0, 0)),
                  smem_rows(lambda i: (n + nxt(i), 0, 0)),
                  pl.BlockSpec((tm, d), lambda i: (i, 0)),
                  pl.BlockSpec((tm, LANES), lambda i: (i, 0)),
                  pl.BlockSpec((1, d), lambda i: (0, 0)),
                  pl.BlockSpec((1, d), lambda i: (0, 0)),
                  pl.BlockSpec(memory_space=pl.ANY)],
        out_specs=pl.BlockSpec((tm, d), lambda i: (i, 0)),
        out_shape=jax.ShapeDtypeStruct((m, d), F32),
        scratch_shapes=[pltpu.VMEM((2, 2, tm, d), F32), pltpu.SemaphoreType.DMA((2,))],
        compiler_params=_cparams(("arbitrary",)),
        name="moe_combine_deepnorm",
    )(dest3, dest3, dest3, dest3, x, wgt, ln_g.reshape(1, d), ln_b.reshape(1, d), ys)


def _moe_deepnorm(x, router, w_gate, w_up, w_down, ln_g, ln_b, alpha):
    m = x.shape[0]
    n_exp = router.shape[1]
    tm = _tile(m, MOE_ROWS)
    idx, wgt = _router_top2(x, router)
    dest, tile_exp, n_valid, fill_lo, fill_hi = _moe_plan(idx, m, tm, n_exp)
    xs = _moe_permute(x, dest, fill_lo, fill_hi, 2 * m + n_exp * tm)
    ys = _moe_experts(xs, tile_exp, n_valid, tm, w_gate, w_up, w_down)
    return _moe_combine_deepnorm(x, ys, dest, wgt, ln_g, ln_b, alpha)


def _proj_pairs_kernel(transposed, x_ref, w_ref, o_ref, xb_s):
    @pl.when(pl.program_id(1) == 0)
    def _():
        xb_s[...] = x_ref[...].astype(BF16)

    n_slab = o_ref.shape[0]
    if transposed:
        res = _dot_nt(w_ref[...], xb_s[...])
        for q in range(n_slab):
            o_ref[q] = res[q * LANES:(q + 1) * LANES, :].astype(o_ref.dtype)
    else:
        res = _dot(xb_s[...], w_ref[...])
        for q in range(n_slab):
            o_ref[q] = res[:, q * LANES:(q + 1) * LANES].astype(o_ref.dtype)


def _proj_pairs(x, w, transposed, out_dtype=F32):
    m, d = x.shape
    n = w.shape[1]
    tm = _tile(m, 512)
    tn = _tile(n, 1024)
    n_slab = tn // LANES
    if transposed:
        wb = w.T.astype(BF16)
        w_spec = pl.BlockSpec((tn, d), lambda i, j: (j, 0))
        o_spec = pl.BlockSpec((n_slab, LANES, tm), lambda i, j: (j, 0, i))
        o_shape = jax.ShapeDtypeStruct((n // LANES, LANES, m), out_dtype)
    else:
        wb = w.astype(BF16)
        w_spec = pl.BlockSpec((d, tn), lambda i, j: (0, j))
        o_spec = pl.BlockSpec((n_slab, tm, LANES), lambda i, j: (j, i, 0))
        o_shape = jax.ShapeDtypeStruct((n // LANES, m, LANES), out_dtype)
    return pl.pallas_call(
        functools.partial(_proj_pairs_kernel, transposed),
        grid=(m // tm, n // tn),
        in_specs=[pl.BlockSpec((tm, d), lambda i, j: (i, 0)), w_spec],
        out_specs=o_spec,
        out_shape=o_shape,
        scratch_shapes=[pltpu.VMEM((tm, d), BF16)],
        compiler_params=_cparams(("parallel", "arbitrary")),
        name="proj_pairs_t" if transposed else "proj_pairs",
    )(x, wb)


def _block_mean_kernel(k_ref, o_ref):
    k = k_ref[0]
    nb = k.shape[0] // MOBA_BLOCK
    o_ref[0] = jnp.mean(k.reshape(nb, MOBA_BLOCK, LANES), axis=1)


def _block_means(k_pairs, batch):
    n_pair, m, _ = k_pairs.shape
    seq = m // batch
    nb = seq // MOBA_BLOCK
    return pl.pallas_call(
        _block_mean_kernel,
        grid=(n_pair, batch),
        in_specs=[pl.BlockSpec((1, seq, LANES), lambda p, b: (p, b, 0))],
        out_specs=pl.BlockSpec((1, nb, LANES), lambda p, b: (p, b, 0)),
        out_shape=jax.ShapeDtypeStruct((n_pair, batch * nb, LANES), F32),
        compiler_params=_cparams(("parallel", "parallel")),
        name="moba_block_means",
    )(k_pairs)


def _moba_kernel(scale, qt_ref, k_ref, vt_ref, km_ref, o_ref, neg_s, sca_s, scb_s, m_s, acc_s):
    own = pl.program_id(2)
    blk = MOBA_BLOCK
    n_slab = qt_ref.shape[0]
    nb = km_ref.shape[1]
    row = lax.broadcasted_iota(jnp.int32, (LANES, blk), 0)
    zero_q = jnp.zeros((LANES, blk), F32)
    n_iota = lax.broadcasted_iota(jnp.int32, (nb, blk), 0).astype(F32)
    past = n_iota < own.astype(F32)
    neg_inf = jnp.float32(-jnp.inf)

    streams = [(g, h) for g in range(n_slab) for h in range(2)]
    qh = []
    for s, (g, h) in enumerate(streams):
        in_head = (row < HEAD_DIM) if h == 0 else (row >= HEAD_DIM)
        q_h = jnp.where(in_head, qt_ref[g], zero_q)
        qh.append((q_h * (scale * LOG2E)).astype(BF16))
        gate = _dot_sp(_split(km_ref[g]), _split(q_h))
        gate = jnp.where(past, gate, neg_inf)
        neg = jnp.full((nb, blk), NEG_BIG, F32)
        for _ in range(min(MOBA_TOPK, nb)):
            mx = jnp.max(gate, axis=0, keepdims=True)
            idx = jnp.min(jnp.where(gate == mx, n_iota, float(nb)), axis=0, keepdims=True)
            pick = n_iota == idx
            neg = jnp.where(jnp.logical_and(pick, past), 0.0, neg)
            gate = jnp.where(pick, neg_inf, gate)
        neg_s[s] = neg

    ones_rows = jnp.ones((DEN_ROWS, blk), BF16)

    def block_scores(n, s):
        start = pl.multiple_of(n * blk, blk)
        return _dot(k_ref[streams[s][0], pl.ds(start, blk), :], qh[s])

    def absorb(n, st, bias, s):
        g, h = streams[s]
        start = pl.multiple_of(n * blk, blk)
        mx = m_s[s]
        mblk = jnp.max(st, axis=0, keepdims=True)
        if bias is not None:
            mblk = mblk + bias
        mx_new = jnp.maximum(mx, mblk)
        alpha = jnp.exp2(mx - mx_new)
        shift = mx_new if bias is None else mx_new - bias
        p = jnp.exp2(st - shift).astype(BF16)
        vtb = vt_ref[g, h * HEAD_DIM:(h + 1) * HEAD_DIM, pl.ds(start, blk)]
        m_s[s] = mx_new
        acc_s[s] = alpha * acc_s[s] + _dot(jnp.concatenate([vtb, ones_rows], axis=0), p)

    kpos = lax.broadcasted_iota(jnp.int32, (blk, blk), 0)
    qpos = lax.broadcasted_iota(jnp.int32, (blk, blk), 1)
    causal = kpos <= qpos
    own_scores = [block_scores(own, s) for s in range(len(streams))]
    for s in range(len(streams)):
        sca_s[s] = block_scores(0, s)
        m_s[s] = jnp.full((1, blk), neg_inf, F32)
        acc_s[s] = jnp.zeros((HEAD_DIM + DEN_ROWS, blk), F32)
    for s in range(len(streams)):
        absorb(own, jnp.where(causal, own_scores[s], neg_inf), None, s)

    def body(j, carry):
        n0 = 2 * j
        n1 = n0 + 1
        n2 = jnp.minimum(n0 + 2, nb - 1)
        for s in range(len(streams)):
            scb_s[s] = block_scores(n1, s)
            absorb(n0, sca_s[s], neg_s[s, pl.ds(n0, 1), :], s)
        for s in range(len(streams)):
            sca_s[s] = block_scores(n2, s)
            absorb(n1, scb_s[s], neg_s[s, pl.ds(n1, 1), :], s)
        return carry

    lax.fori_loop(0, (own + 1) // 2, body, 0)
    for g in range(n_slab):
        a0, a1 = acc_s[2 * g], acc_s[2 * g + 1]
        o_t = jnp.concatenate([a0[:HEAD_DIM] / a0[HEAD_DIM:HEAD_DIM + 1],
                               a1[:HEAD_DIM] / a1[HEAD_DIM:HEAD_DIM + 1]], axis=0)
        o_ref[g] = o_t.T.astype(o_ref.dtype)


def _moba_attention(q_t, k, v_t, k_means, batch):
    n_pair, _, m = q_t.shape
    seq = m // batch
    nb = seq // MOBA_BLOCK
    g = _tile(n_pair, MOBA_SLABS)
    return pl.pallas_call(
        functools.partial(_moba_kernel, HEAD_DIM ** -0.5),
        grid=(n_pair // g, batch, nb),
        in_specs=[pl.BlockSpec((g, LANES, MOBA_BLOCK), lambda p, b, i: (p, 0, b * nb + i)),
                  pl.BlockSpec((g, seq, LANES), lambda p, b, i: (p, b, 0),
                               pipeline_mode=pl.Buffered(1)),
                  pl.BlockSpec((g, LANES, seq), lambda p, b, i: (p, 0, b),
                               pipeline_mode=pl.Buffered(1)),
                  pl.BlockSpec((g, nb, LANES), lambda p, b, i: (p, b, 0))],
        out_specs=pl.BlockSpec((g, MOBA_BLOCK, LANES), lambda p, b, i: (p, b * nb + i, 0)),
        out_shape=jax.ShapeDtypeStruct((n_pair, m, LANES), BF16),
        scratch_shapes=[pltpu.VMEM((2 * g, nb, MOBA_BLOCK), F32),
                        pltpu.VMEM((2 * g, MOBA_BLOCK, MOBA_BLOCK), F32),
                        pltpu.VMEM((2 * g, MOBA_BLOCK, MOBA_BLOCK), F32),
                        pltpu.VMEM((2 * g, 1, MOBA_BLOCK), F32),
                        pltpu.VMEM((2 * g, HEAD_DIM + DEN_ROWS, MOBA_BLOCK), F32)],
        compiler_params=_cparams(("parallel", "parallel", "arbitrary")),
        name="moba_attention",
    )(q_t, k, v_t, k_means)


def kernel(x, rwkv_mu, rwkv_w_rkv, rwkv_w_out, rwkv_decay_w0, rwkv_decay_w1, rwkv_decay_w2, rwkv_iclr_a0, rwkv_iclr_a1, rwkv_iclr_a2, rwkv_vres_v0, rwkv_vres_v1, rwkv_vres_v2, rwkv_gate_g1, rwkv_gate_g2, rwkv_k_k, rwkv_k_a, rwkv_r_k, rwkv_gn_g, rwkv_gn_b, moba_w_k, moba_w_v, moba_w_q, moba_w_o, ffn_w_gate, ffn_w_up, ffn_w_down, moe_router, moe_w_gate, moe_w_up, moe_w_down, ln_g, ln_b):
    batch, seq, d = x.shape
    assert d % (2 * LANES) == 0 and seq % MOBA_BLOCK == 0 and seq % WKV_CHUNK == 0
    depth = ln_g.shape[0]
    n_rwkv = rwkv_mu.shape[0]
    alpha = (2.0 * depth) ** 0.25
    h = x.reshape(batch * seq, d)
    v_first = None
    kv = None
    for layer in range(depth):
        if layer < n_rwkv:
            i = layer
            vres = None if i == 0 else (rwkv_vres_v0[i - 1], rwkv_vres_v1[i - 1], rwkv_vres_v2[i - 1])
            r, e, k, v, a, b, g = _rwkv_proj(
                h, seq, rwkv_mu[i], rwkv_w_rkv[i], rwkv_decay_w1[i], rwkv_iclr_a1[i],
                rwkv_gate_g1[i], rwkv_decay_w2[i], rwkv_iclr_a2[i], rwkv_gate_g2[i],
                rwkv_decay_w0[i], rwkv_iclr_a0[i], rwkv_k_k[i], rwkv_k_a[i],
                vres=vres, v_first=v_first)
            if i == 0:
                v_first = v
            y = _wkv_scan(r, e, k, v, a, b, batch)
            mix = _rwkv_post(y, r, k, v, g, rwkv_r_k[i], rwkv_gn_g[i], rwkv_gn_b[i])
            w_out = rwkv_w_out[i]
        else:
            jdx = layer - n_rwkv
            k_pairs, v_t, k_means = kv
            q_t = _proj_pairs(h, moba_w_q[jdx], transposed=True)
            mix = _moba_attention(q_t, k_pairs, v_t, k_means, batch)
            w_out = moba_w_o[jdx]
        h = _out_proj_deepnorm(mix, w_out, h, ln_g[layer, 0], ln_b[layer, 0], alpha)
        ex = layer // 2
        if layer % 2 == 0:
            h = _ffn_deepnorm(h, _layer_bf16(ffn_w_gate, ex), _layer_bf16(ffn_w_up, ex),
                              _layer_bf16(ffn_w_down, ex), ln_g[layer, 1], ln_b[layer, 1], alpha)
        else:
            h = _moe_deepnorm(h, moe_router[ex], _layer_bf16(moe_w_gate, ex),
                              _layer_bf16(moe_w_up, ex), _layer_bf16(moe_w_down, ex),
                              ln_g[layer, 1], ln_b[layer, 1], alpha)
        if layer == n_rwkv - 1:
            k_pairs = _proj_pairs(h, moba_w_k, transposed=False)
            v_t = _proj_pairs(h, moba_w_v, transposed=True, out_dtype=BF16)
            kv = (k_pairs.astype(BF16), v_t, _block_means(k_pairs, batch))
    return h.reshape(batch, seq, d)
```

```python
import functools
import math

import jax
import jax.numpy as jnp
from jax import lax
from jax.experimental import pallas as pl
from jax.experimental.pallas import tpu as pltpu

HEAD_DIM = 64
LANES = 128
MXU_DEPTH = 256
GN_EPS = 64e-5
LN_EPS = 1e-5
MOBA_BLOCK = 256
MOBA_TOPK = 3
MOE_ROWS = 512
FFN_TILE = 2816
CAST_BLOCK_ELEMS = 1 << 20
WKV_CHUNK = 64
WKV_SLABS = 8
NEG_BIG = -1e30
MOBA_SLABS = 8
DEN_ROWS = 16
LOG2E = 1.4426950408889634

F32 = jnp.float32
BF16 = jnp.bfloat16
VMEM_LIMIT = 56 * 1024 * 1024


def _cparams(sem):
    return pltpu.CompilerParams(dimension_semantics=sem, vmem_limit_bytes=VMEM_LIMIT)


def _dot(a, b):
    return jnp.dot(a, b, preferred_element_type=F32)


def _dot_nt(a, b):
    return lax.dot_general(a, b, (((1,), (1,)), ((), ())), preferred_element_type=F32)


def _split(x):
    hi = x.astype(BF16)
    lo = (x - hi.astype(F32)).astype(BF16)
    return hi, lo


def _dot_sp(a, b, nt=False):
    d = _dot_nt if nt else _dot
    return d(a[0], b[0]) + (d(a[0], b[1]) + d(a[1], b[0]))


def _dot_exact_rhs(a, b_exact, nt=False):
    d = _dot_nt if nt else _dot
    hi, lo = _split(a)
    return d(hi, b_exact) + d(lo, b_exact)


def _sigmoid(x):
    return 1.0 / (1.0 + jnp.exp(-x))


def _layer_norm(y, g, b):
    mu = jnp.mean(y, axis=-1, keepdims=True)
    yc = y - mu
    var = jnp.mean(yc * yc, axis=-1, keepdims=True)
    return yc * lax.rsqrt(var + LN_EPS) * g + b


def _head_blockdiag(n):
    i = jnp.arange(n) // HEAD_DIM
    return (i[:, None] == i[None, :]).astype(BF16)


def _tile(n, want):
    t = min(n, want)
    assert n % t == 0, (n, want)
    return t


def _cast_kernel(w_ref, o_ref):
    o_ref[...] = w_ref[...].astype(o_ref.dtype)


def _layer_bf16(w, layer):
    shape = w.shape[1:]
    c = shape[-1]
    rows = math.prod(shape[:-1])
    tr = rows
    for cand in range(min(rows, CAST_BLOCK_ELEMS // c) // 16, 0, -1):
        if rows % (cand * 16) == 0:
            tr = cand * 16
            break
    out = pl.pallas_call(
        _cast_kernel,
        grid=(rows // tr,),
        in_specs=[pl.BlockSpec((1, tr, c), lambda i: (layer, i, 0))],
        out_specs=pl.BlockSpec((1, tr, c), lambda i: (0, i, 0)),
        out_shape=jax.ShapeDtypeStruct((1, rows, c), BF16),
        compiler_params=_cparams(("parallel",)),
        name="weight_to_bf16",
    )(w.reshape(w.shape[0], rows, c))
    return out.reshape(shape)


def _rwkv_proj_kernel(has_vres, steps_per_seq, *refs):
    if has_vres:
        (x_ref, xp_ref, mu_ref, wrkv_ref, wd1_ref, wa1_ref, wg1_ref, wv1_ref,
         wd2_ref, wa2_ref, wg2_ref, wv2_ref, w0_ref, a0_ref, v0_ref, kk_ref, ka_ref,
         bd_ref, vf_ref,
         r_out, e_out, k_out, v_out, a_out, b_out, g_out,
         xm_s, hd_s, ha_s, hg_s, hv_s) = refs
    else:
        (x_ref, xp_ref, mu_ref, wrkv_ref, wd1_ref, wa1_ref, wg1_ref,
         wd2_ref, wa2_ref, wg2_ref, w0_ref, a0_ref, kk_ref, ka_ref,
         bd_ref,
         r_out, e_out, k_out, v_out, a_out, b_out, g_out,
         xm_s, hd_s, ha_s, hg_s) = refs
    i = pl.program_id(0)
    j = pl.program_id(1)

    @pl.when(j == 0)
    def _():
        x = x_ref[...]
        tm = x.shape[0]
        prev_row = jnp.where(i % steps_per_seq == 0, 0.0, xp_ref[7:8, :])
        rolled = pltpu.roll(x, 1, axis=0)
        row = lax.broadcasted_iota(jnp.int32, (tm, 1), 0)
        x_prev = jnp.where(row == 0, prev_row, rolled)
        xx = x_prev - x
        for c in range(3):
            xm_s[c] = (x + xx * mu_ref[c:c + 1, :]).astype(BF16)
        xw = (x + xx * mu_ref[3:4, :]).astype(BF16)
        xa = (x + xx * mu_ref[4:5, :]).astype(BF16)
        xg = (x + xx * mu_ref[5:6, :]).astype(BF16)
        hd_s[...] = jnp.tanh(_dot(xw, wd1_ref[...])).astype(BF16)
        ha_s[...] = _dot(xa, wa1_ref[...]).astype(BF16)
        hg_s[...] = _sigmoid(_dot(xg, wg1_ref[...])).astype(BF16)
        if has_vres:
            hv_s[...] = _dot(xm_s[2], wv1_ref[...]).astype(BF16)

    r = _dot(xm_s[0], wrkv_ref[0])
    k = _dot(xm_s[1], wrkv_ref[1])
    v = _dot(xm_s[2], wrkv_ref[2])
    z = w0_ref[...] + _dot(hd_s[...], wd2_ref[...])
    nz = -z
    softplus = jnp.maximum(nz, 0.0) + jnp.log(1.0 + jnp.exp(-jnp.abs(nz)))
    e = jnp.exp(-softplus - 0.5)
    a = _sigmoid(a0_ref[...] + _dot(ha_s[...], wa2_ref[...]))
    g = _dot(hg_s[...], wg2_ref[...])
    n_slab = r.shape[1] // LANES
    if has_vres:
        vf = jnp.concatenate([vf_ref[q] for q in range(n_slab)], axis=1)
        v = v + (vf - v) * _sigmoid(v0_ref[...] + _dot(hv_s[...], wv2_ref[...]))
    kk = k * kk_ref[...]
    ss = _dot_exact_rhs(kk * kk, bd_ref[...])
    kk = kk * lax.rsqrt(jnp.maximum(ss, 1e-24))
    k = k * (1.0 + (a - 1.0) * ka_ref[...])
    for q in range(n_slab):
        sl = slice(q * LANES, (q + 1) * LANES)
        r_out[q] = r[:, sl].astype(r_out.dtype)
        e_out[q] = e[:, sl]
        k_out[q] = k[:, sl].astype(k_out.dtype)
        v_out[q] = v[:, sl].astype(v_out.dtype)
        a_out[q] = (-kk)[:, sl].astype(a_out.dtype)
        b_out[q] = (kk * a)[:, sl].astype(b_out.dtype)
        g_out[q] = g[:, sl].astype(g_out.dtype)


def _rwkv_proj(x, seq_len, mu, w_rkv, wd1, wa1, wg1, wd2, wa2, wg2, w0, a0, k_k, k_a,
               vres=None, v_first=None):
    m, d = x.shape
    tm = _tile(seq_len, 512)
    tn = _tile(d, 256)
    n_slab = tn // LANES
    has_vres = vres is not None
    row = lambda a: a.reshape(1, d)
    full = lambda a: pl.BlockSpec(a.shape, lambda i, j: (0,) * a.ndim)
    colblk = lambda rows: pl.BlockSpec((rows, tn), lambda i, j: (0, j))
    bd = _head_blockdiag(tn)
    wd1, wa1, wg1 = wd1.astype(BF16), wa1.astype(BF16), wg1.astype(BF16)
    wd2, wa2, wg2 = wd2.astype(BF16), wa2.astype(BF16), wg2.astype(BF16)
    w_rkv = w_rkv.astype(BF16)
    args = [x, x, mu, w_rkv, wd1, wa1, wg1]
    specs = [pl.BlockSpec((tm, d), lambda i, j: (i, 0)),
             pl.BlockSpec((8, d), lambda i, j: (jnp.maximum(i * (tm // 8) - 1, 0), 0)),
             full(mu),
             pl.BlockSpec((3, d, tn), lambda i, j: (0, 0, j)),
             full(wd1), full(wa1), full(wg1)]
    if has_vres:
        v0, wv1, wv2 = vres
        wv1, wv2 = wv1.astype(BF16), wv2.astype(BF16)
        args += [wv1]
        specs += [full(wv1)]
    args += [wd2, wa2, wg2]
    specs += [colblk(wd2.shape[0]), colblk(wa2.shape[0]), colblk(wg2.shape[0])]
    if has_vres:
        args += [wv2]
        specs += [colblk(wv2.shape[0])]
    args += [row(w0), row(a0)]
    specs += [colblk(1), colblk(1)]
    if has_vres:
        args += [row(v0)]
        specs += [colblk(1)]
    args += [row(k_k), row(k_a), bd]
    specs += [colblk(1), colblk(1), full(bd)]
    slab_spec = pl.BlockSpec((n_slab, tm, LANES), lambda i, j: (j, i, 0))
    if has_vres:
        args += [v_first]
        specs += [slab_spec]
    slab = lambda dt: jax.ShapeDtypeStruct((d // LANES, m, LANES), dt)
    scratch = [pltpu.VMEM((3, tm, d), BF16),
               pltpu.VMEM((tm, wd1.shape[1]), BF16),
               pltpu.VMEM((tm, wa1.shape[1]), BF16),
               pltpu.VMEM((tm, wg1.shape[1]), BF16)]
    if has_vres:
        scratch.append(pltpu.VMEM((tm, wv1.shape[1]), BF16))
    return pl.pallas_call(
        functools.partial(_rwkv_proj_kernel, has_vres, seq_len // tm),
        grid=(m // tm, d // tn),
        in_specs=specs,
        out_specs=[slab_spec] * 7,
        out_shape=[slab(BF16), slab(F32)] + [slab(BF16)] * 5,
        scratch_shapes=scratch,
        compiler_params=_cparams(("parallel", "arbitrary")),
        name="rwkv_proj",
    )(*args)


def _wkv_kernel(r_ref, e_ref, k_ref, v_ref, a_ref, b_ref, y_ref, s_ref):
    c = WKV_CHUNK
    n_slab, n_batch, tb, _ = r_ref.shape
    n_chunks = tb // c

    @pl.when(pl.program_id(1) == 0)
    def _():
        s_ref[...] = jnp.zeros_like(s_ref)

    lane = lax.broadcasted_iota(jnp.int32, (c, LANES), 1)
    t_idx = lax.broadcasted_iota(jnp.int32, (c, LANES), 0)
    head0 = lane < HEAD_DIM
    s_idx = jnp.bitwise_and(lane, HEAD_DIM - 1)
    strict = s_idx < t_idx
    incl = s_idx <= t_idx
    rr = lax.broadcasted_iota(jnp.int32, (LANES, LANES), 0)
    cc = lax.broadcasted_iota(jnp.int32, (LANES, LANES), 1)
    same_head = (rr < HEAD_DIM) == (cc < HEAD_DIM)

    def bd(x):
        z = jnp.zeros_like(x)
        return jnp.concatenate([jnp.where(head0, x, z), jnp.where(head0, z, x)], axis=0)

    def cat(a, b, axis=0):
        return jnp.concatenate([a, b], axis=axis)

    def chunk(ci, carry):
        stages = [one_slab(ci, p, bb) for p in range(n_slab) for bb in range(n_batch)]
        while stages:
            stages = [g for g in stages if next(g, "done") != "done"]
        return carry

    def one_slab(ci, p, bb):
        sl = pl.ds(pl.multiple_of(ci * c, c), c)
        r = r_ref[p, bb, sl, :]
        e = e_ref[p, bb, sl, :]
        k = k_ref[p, bb, sl, :]
        v = v_ref[p, bb, sl, :]
        a = a_ref[p, bb, sl, :]
        b = b_ref[p, bb, sl, :]
        s0 = s_ref[p * n_batch + bb]
        cum = _cumsum_rows(e, t_idx)
        yield
        tot = cum[c - 1:c, :]
        ar = cat(a * jnp.exp(e - cum), r * jnp.exp(-cum)).astype(BF16)
        ec = jnp.exp(cum)
        bt = (b * ec).astype(BF16)
        kt = (k * ec).astype(BF16)
        eh = jnp.exp(cum - tot)
        bk = cat(b * eh, k * eh).astype(BF16)
        bdv = bd(v.astype(BF16))
        gram = _dot_nt(ar, cat(bd(bt), bd(kt)))
        yield
        zero = jnp.zeros((c, LANES), F32)
        l_ab = jnp.where(strict, gram[:c, :LANES], zero)
        n_ak = jnp.where(strict, gram[:c, LANES:], zero)
        m_rb = jnp.where(incl, gram[c:, :LANES], zero)
        m_rk = jnp.where(incl, gram[c:, LANES:], zero)
        xy = _dot_nt(ar, s0.astype(BF16)) + _dot(cat(n_ak, m_rk).astype(BF16), bdv)
        x = xy[:c]
        y0 = xy[c:]
        yield
        lp = l_ab.astype(BF16)
        n_steps = int(math.log2(c))
        for step in range(n_steps):
            bdx = bd(x.astype(BF16))
            if step + 1 < n_steps:
                t = _dot(lp, cat(bdx, bd(lp), axis=1))
                x = x + t[:, :LANES]
                lp = t[:, LANES:].astype(BF16)
            else:
                x = x + _dot(lp, bdx)
            yield
        y_ref[p, bb, sl, :] = y0 + _dot(m_rb.astype(BF16), bd(x.astype(BF16)))
        uv_t = cat(x, v.astype(F32)).T.astype(BF16)
        upd = _dot(uv_t, bk)
        s_ref[p * n_batch + bb] = (s0 * jnp.exp(-tot)
                                   + jnp.where(same_head, upd, jnp.zeros_like(upd)))
        yield

    lax.fori_loop(0, n_chunks, chunk, 0)


def _cumsum_rows(e, row_idx):
    cum = e
    shift = 1
    while shift < e.shape[0]:
        cum = cum + jnp.where(row_idx >= shift, pltpu.roll(cum, shift, axis=0), 0.0)
        shift *= 2
    return cum


def _wkv_scan(r, e, k, v, a, b, batch):
    n_pair, m, _ = r.shape
    seq = m // batch
    tb = _tile(seq, 256)
    n_slab = _tile(n_pair, WKV_SLABS)
    view = lambda t: t.reshape(n_pair, batch, seq, LANES)
    spec = pl.BlockSpec((n_slab, batch, tb, LANES), lambda p, t: (p, 0, t, 0))
    y = pl.pallas_call(
        _wkv_kernel,
        grid=(n_pair // n_slab, seq // tb),
        in_specs=[spec] * 6,
        out_specs=spec,
        out_shape=jax.ShapeDtypeStruct((n_pair, batch, seq, LANES), F32),
        scratch_shapes=[pltpu.VMEM((n_slab * batch, LANES, LANES), F32)],
        compiler_params=_cparams(("parallel", "arbitrary")),
        name="wkv_scan",
    )(view(r), view(e), view(k), view(v), view(a), view(b))
    return y.reshape(n_pair, m, LANES)


def _rwkv_post_kernel(y_ref, r_ref, k_ref, v_ref, g_ref, rk_ref, gg_ref, gb_ref, bd_ref, z_ref):
    y = y_ref[0]
    bd2 = bd_ref[...]
    bdm = bd2[:LANES, :LANES]
    inv_n = 1.0 / HEAD_DIM
    rk = r_ref[0].astype(F32) * k_ref[0].astype(F32) * rk_ref[0]
    sums = _dot_exact_rhs(jnp.concatenate([y, rk], axis=1), bd2)
    yc = y - sums[:, :LANES] * inv_n
    var = _dot_exact_rhs(yc * yc, bdm) * inv_n
    yn = yc * lax.rsqrt(var + GN_EPS) * gg_ref[0] + gb_ref[0]
    bonus = sums[:, LANES:] * v_ref[0].astype(F32)
    z_ref[0] = ((yn + bonus) * g_ref[0].astype(F32)).astype(z_ref.dtype)


def _rwkv_post(y, r, k, v, g, r_k, gn_g, gn_b):
    n_pair, m, _ = y.shape
    tm = _tile(m, 2048)
    spec = pl.BlockSpec((1, tm, LANES), lambda p, i: (p, i, 0))
    pspec = pl.BlockSpec((1, 1, LANES), lambda p, i: (p, 0, 0))
    bd = _head_blockdiag(2 * LANES)
    slab = lambda a: a.reshape(n_pair, 1, LANES)
    return pl.pallas_call(
        _rwkv_post_kernel,
        grid=(n_pair, m // tm),
        in_specs=[spec] * 5 + [pspec] * 3 + [pl.BlockSpec(bd.shape, lambda p, i: (0, 0))],
        out_specs=spec,
        out_shape=jax.ShapeDtypeStruct((n_pair, m, LANES), BF16),
        compiler_params=_cparams(("parallel", "parallel")),
        name="rwkv_post",
    )(y, r, k, v, g, slab(r_k), slab(gn_g), slab(gn_b), bd)


def _out_proj_kernel(alpha, z_ref, w_ref, x_ref, g_ref, b_ref, o_ref):
    per = w_ref.shape[1] // LANES
    acc = None
    for c in range(w_ref.shape[0]):
        zc = jnp.concatenate([z_ref[c * per + q] for q in range(per)], axis=1)
        part = _dot(zc, w_ref[c])
        acc = part if acc is None else acc + part
    o_ref[...] = _layer_norm(alpha * x_ref[...] + acc, g_ref[...], b_ref[...])


def _out_proj_deepnorm(z, w, x, ln_g, ln_b, alpha):
    n_pair, m, _ = z.shape
    d = w.shape[1]
    tm = _tile(m, 1024)
    kc = _tile(d, MXU_DEPTH)
    w3 = w.astype(BF16).reshape(d // kc, kc, d)
    return pl.pallas_call(
        functools.partial(_out_proj_kernel, alpha),
        grid=(m // tm,),
        in_specs=[pl.BlockSpec((n_pair, tm, LANES), lambda i: (0, i, 0)),
                  pl.BlockSpec(w3.shape, lambda i: (0, 0, 0)),
                  pl.BlockSpec((tm, d), lambda i: (i, 0)),
                  pl.BlockSpec((1, d), lambda i: (0, 0)),
                  pl.BlockSpec((1, d), lambda i: (0, 0))],
        out_specs=pl.BlockSpec((tm, d), lambda i: (i, 0)),
        out_shape=jax.ShapeDtypeStruct((m, d), F32),
        compiler_params=_cparams(("parallel",)),
        name="out_proj_deepnorm",
    )(z, w3, x, ln_g.reshape(1, d), ln_b.reshape(1, d))


def _ff_tile(ff, want):
    for cand in range(min(ff, want) // LANES, 0, -1):
        if ff % (cand * LANES) == 0:
            return cand * LANES
    return ff


def _swiglu_partial(xb, wg, wu, wd):
    h1 = _dot(xb, wg)
    h2 = _dot(xb, wu)
    act = h1 * _sigmoid(h1) * h2
    return _dot(act.astype(BF16), wd)


def _ffn_kernel(alpha, x_ref, wg_ref, wu_ref, wd_ref, g_ref, b_ref, o_ref, xb_s, acc_s):
    f = pl.program_id(1)

    @pl.when(f == 0)
    def _():
        xb_s[...] = x_ref[...].astype(BF16)
        acc_s[...] = jnp.zeros_like(acc_s)

    acc_s[...] += _swiglu_partial(xb_s[...], wg_ref[...], wu_ref[...], wd_ref[...])

    @pl.when(f == pl.num_programs(1) - 1)
    def _():
        o_ref[...] = _layer_norm(alpha * x_ref[...] + acc_s[...], g_ref[...], b_ref[...])


def _ffn_deepnorm(x, w_gate, w_up, w_down, ln_g, ln_b, alpha):
    m, d = x.shape
    ff = w_gate.shape[1]
    tm = _tile(m, 512)
    tf = _ff_tile(ff, FFN_TILE)
    wmode = dict(pipeline_mode=pl.Buffered(1)) if tf == ff else {}
    return pl.pallas_call(
        functools.partial(_ffn_kernel, alpha),
        grid=(m // tm, ff // tf),
        in_specs=[pl.BlockSpec((tm, d), lambda i, f: (i, 0)),
                  pl.BlockSpec((d, tf), lambda i, f: (0, f), **wmode),
                  pl.BlockSpec((d, tf), lambda i, f: (0, f), **wmode),
                  pl.BlockSpec((tf, d), lambda i, f: (f, 0), **wmode),
                  pl.BlockSpec((1, d), lambda i, f: (0, 0)),
                  pl.BlockSpec((1, d), lambda i, f: (0, 0))],
        out_specs=pl.BlockSpec((tm, d), lambda i, f: (i, 0)),
        out_shape=jax.ShapeDtypeStruct((m, d), F32),
        scratch_shapes=[pltpu.VMEM((tm, d), BF16), pltpu.VMEM((tm, d), F32)],
        compiler_params=_cparams(("parallel", "arbitrary")),
        name="ffn_deepnorm",
    )(x, w_gate.astype(BF16), w_up.astype(BF16), w_down.astype(BF16),
      ln_g.reshape(1, d), ln_b.reshape(1, d))


def _router_kernel(n_exp, x_ref, w_ref, idx_ref, wgt_ref):
    logits = _dot_sp(_split(x_ref[...]), _split(w_ref[...]))
    lane = lax.broadcasted_iota(jnp.int32, logits.shape, 1).astype(F32)
    neg_inf = jnp.float32(-jnp.inf)
    lg = jnp.where(lane < n_exp, logits, neg_inf)
    m1 = jnp.max(lg, axis=1, keepdims=True)
    i1 = jnp.min(jnp.where(lg == m1, lane, float(LANES)), axis=1, keepdims=True)
    lg2 = jnp.where(lane == i1, neg_inf, lg)
    m2 = jnp.max(lg2, axis=1, keepdims=True)
    i2 = jnp.min(jnp.where(lg2 == m2, lane, float(LANES)), axis=1, keepdims=True)
    e2 = jnp.exp(m2 - m1)
    den = 1.0 + e2
    idx_ref[...] = jnp.where(lane == 0.0, i1, jnp.where(lane == 1.0, i2, 0.0)).astype(jnp.int32)
    wgt_ref[...] = jnp.where(lane == 0.0, 1.0 / den, jnp.where(lane == 1.0, e2 / den, 0.0))


def _router_top2(x, router):
    m, d = x.shape
    n_exp = router.shape[1]
    tm = _tile(m, 512)
    w = jnp.pad(router, ((0, 0), (0, LANES - n_exp)))
    out_spec = pl.BlockSpec((tm, LANES), lambda i: (i, 0))
    return pl.pallas_call(
        functools.partial(_router_kernel, n_exp),
        grid=(m // tm,),
        in_specs=[pl.BlockSpec((tm, d), lambda i: (i, 0)),
                  pl.BlockSpec((d, LANES), lambda i: (0, 0))],
        out_specs=[out_spec, out_spec],
        out_shape=[jax.ShapeDtypeStruct((m, LANES), jnp.int32),
                   jax.ShapeDtypeStruct((m, LANES), F32)],
        compiler_params=_cparams(("parallel",)),
        name="moe_router",
    )(x, w)


def _moe_plan(idx, m, tm, n_exp):
    n_ent = 2 * m
    expert = jnp.concatenate([idx[:, 0], idx[:, 1]])
    onehot = (expert[:, None] == jnp.arange(n_exp, dtype=jnp.int32)[None, :]).astype(jnp.int32)
    csum = jnp.cumsum(onehot, axis=0)
    counts = csum[-1]
    rank = jnp.sum(csum * onehot, axis=1) - 1
    padded = ((counts + tm - 1) // tm) * tm
    ends = jnp.cumsum(padded)
    starts = ends - padded
    dest = jnp.sum(starts[None, :] * onehot, axis=1) + rank
    n_tiles = n_ent // tm + n_exp
    tile_start = jnp.arange(n_tiles, dtype=jnp.int32) * tm
    tile_exp = jnp.minimum(jnp.sum((tile_start[:, None] >= ends[None, :]).astype(jnp.int32), axis=1),
                           n_exp - 1)
    n_valid = jnp.clip(starts[tile_exp] + counts[tile_exp] - tile_start, 0, tm)
    fill_lo = jnp.concatenate([starts + counts, ends[-1:]])
    fill_hi = jnp.concatenate([ends, jnp.full((1,), n_tiles * tm, jnp.int32)])
    return dest, tile_exp, n_valid, fill_lo, fill_hi


def _moe_permute_kernel(lo_ref, hi_ref, dest_ref, x_ref, xs_hbm, sem):
    i = pl.program_id(0)
    te = dest_ref.shape[2]

    def entry_row(r):
        return pltpu.make_async_copy(x_ref.at[pl.ds(r, 1), :],
                                     xs_hbm.at[pl.ds(dest_ref[0, 0, r], 1), :], sem.at[0])

    def filler_row(r):
        return pltpu.make_async_copy(x_ref.at[pl.ds(0, 1), :], xs_hbm.at[pl.ds(r, 1), :], sem.at[1])

    def each(lo, hi, fn, unroll=1):
        def body(r, c):
            fn(r)
            return c
        lax.fori_loop(lo, hi, body, 0, unroll=unroll)

    each(0, te, lambda r: entry_row(r).start(), unroll=8)

    @pl.when(i == 0)
    def _():
        for e in range(lo_ref.shape[0]):
            each(lo_ref[e], hi_ref[e], lambda r: filler_row(r).start())
        for e in range(lo_ref.shape[0]):
            each(lo_ref[e], hi_ref[e], lambda r: filler_row(r).wait())

    each(0, te, lambda r: entry_row(r).wait(), unroll=8)


def _moe_permute(x, dest, fill_lo, fill_hi, n_rows):
    m, d = x.shape
    te = _tile(m, 512)
    n_x = m // te
    dest3 = dest.reshape(-1, 1, te)
    grid_spec = pltpu.PrefetchScalarGridSpec(
        num_scalar_prefetch=2,
        grid=(dest3.shape[0],),
        in_specs=[pl.BlockSpec((1, 1, te), lambda i, lo, hi: (i, 0, 0), memory_space=pltpu.SMEM),
                  pl.BlockSpec((te, d), lambda i, lo, hi: (i % n_x, 0))],
        out_specs=pl.BlockSpec(memory_space=pl.ANY),
        scratch_shapes=[pltpu.SemaphoreType.DMA((2,))],
    )
    return pl.pallas_call(
        _moe_permute_kernel,
        grid_spec=grid_spec,
        out_shape=jax.ShapeDtypeStruct((n_rows, d), F32),
        compiler_params=_cparams(("arbitrary",)),
        name="moe_permute",
    )(fill_lo, fill_hi, dest3, x)


def _moe_experts_kernel(te_ref, nv_ref, x_ref, wg_ref, wu_ref, wd_ref, y_ref, xb_s, acc_s):
    t = pl.program_id(0)
    f = pl.program_id(1)

    @pl.when(f == 0)
    def _():
        xb_s[...] = x_ref[...].astype(BF16)
        acc_s[...] = jnp.zeros_like(acc_s)

    @pl.when(nv_ref[t] > 0)
    def _():
        acc_s[...] += _swiglu_partial(xb_s[...], wg_ref[0], wu_ref[0], wd_ref[0])

    @pl.when(f == pl.num_programs(1) - 1)
    def _():
        y_ref[...] = acc_s[...]


def _moe_experts(xs, tile_exp, n_valid, tm, w_gate, w_up, w_down):
    n_rows, d = xs.shape
    ff = w_gate.shape[2]
    tf = _ff_tile(ff, 1792)
    grid_spec = pltpu.PrefetchScalarGridSpec(
        num_scalar_prefetch=2,
        grid=(n_rows // tm, ff // tf),
        in_specs=[pl.BlockSpec((tm, d), lambda t, f, te, nv: (t, 0)),
                  pl.BlockSpec((1, d, tf), lambda t, f, te, nv: (te[t], 0, f)),
                  pl.BlockSpec((1, d, tf), lambda t, f, te, nv: (te[t], 0, f)),
                  pl.BlockSpec((1, tf, d), lambda t, f, te, nv: (te[t], f, 0))],
        out_specs=pl.BlockSpec((tm, d), lambda t, f, te, nv: (t, 0)),
        scratch_shapes=[pltpu.VMEM((tm, d), BF16), pltpu.VMEM((tm, d), F32)],
    )
    return pl.pallas_call(
        _moe_experts_kernel,
        grid_spec=grid_spec,
        out_shape=jax.ShapeDtypeStruct((n_rows, d), F32),
        compiler_params=_cparams(("parallel", "arbitrary")),
        name="moe_experts",
    )(tile_exp, n_valid, xs, w_gate.astype(BF16), w_up.astype(BF16), w_down.astype(BF16))


def _moe_combine_kernel(alpha, d0_ref, d1_ref, d0n_ref, d1n_ref, x_ref, w_ref, g_ref, b_ref,
                        ys_hbm, o_ref, yg_s, sem):
    i = pl.program_id(0)
    tm = x_ref.shape[0]
    buf = i % 2

    def slot_row(idx_ref, k, r, b):
        return pltpu.make_async_copy(ys_hbm.at[pl.ds(idx_ref[0, 0, r], 1), :],
                                     yg_s.at[b, k, pl.ds(r, 1), :], sem.at[b])

    def each_row(fn):
        def body(r, c):
            fn(r)
            return c
        lax.fori_loop(0, tm, body, 0, unroll=8)

    def start(i0, i1, b):
        each_row(lambda r: (slot_row(i0, 0, r, b).start(), slot_row(i1, 1, r, b).start()))

    @pl.when(i == 0)
    def _():
        start(d0_ref, d1_ref, 0)

    @pl.when(i + 1 < pl.num_programs(0))
    def _():
        start(d0n_ref, d1n_ref, 1 - buf)

    each_row(lambda r: (slot_row(d0_ref, 0, r, buf).wait(), slot_row(d1_ref, 1, r, buf).wait()))
    w = w_ref[...]
    mix = yg_s[buf, 0] * w[:, 0:1] + yg_s[buf, 1] * w[:, 1:2]
    o_ref[...] = _layer_norm(alpha * x_ref[...] + mix, g_ref[...], b_ref[...])


def _moe_combine_deepnorm(x, ys, dest, wgt, ln_g, ln_b, alpha):
    m, d = x.shape
    tm = _tile(m, 512)
    n = m // tm
    dest3 = dest.reshape(2 * n, 1, tm)
    smem_rows = lambda imap: pl.BlockSpec((1, 1, tm), imap, memory_space=pltpu.SMEM)
    nxt = lambda i: jnp.minimum(i + 1, n - 1)
    return pl.pallas_call(
        functools.partial(_moe_combine_kernel, alpha),
        grid=(n,),
        in_specs=[smem_rows(lambda i: (i, 0, 0)),
                  smem_rows(lambda i: (n + i, 0, 0)),
                  smem_rows(lambda i: (nxt(i), 0, 0)),
                  smem_rows(lambda i: (n + nxt(i), 0, 0)),
                  pl.BlockSpec((tm, d), lambda i: (i, 0)),
                  pl.BlockSpec((tm, LANES), lambda i: (i, 0)),
                  pl.BlockSpec((1, d), lambda i: (0, 0)),
                  pl.BlockSpec((1, d), lambda i: (0, 0)),
                  pl.BlockSpec(memory_space=pl.ANY)],
        out_specs=pl.BlockSpec((tm, d), lambda i: (i, 0)),
        out_shape=jax.ShapeDtypeStruct((m, d), F32),
        scratch_shapes=[pltpu.VMEM((2, 2, tm, d), F32), pltpu.SemaphoreType.DMA((2,))],
        compiler_params=_cparams(("arbitrary",)),
        name="moe_combine_deepnorm",
    )(dest3, dest3, dest3, dest3, x, wgt, ln_g.reshape(1, d), ln_b.reshape(1, d), ys)


def _moe_deepnorm(x, router, w_gate, w_up, w_down, ln_g, ln_b, alpha):
    m = x.shape[0]
    n_exp = router.shape[1]
    tm = _tile(m, MOE_ROWS)
    idx, wgt = _router_top2(x, router)
    dest, tile_exp, n_valid, fill_lo, fill_hi = _moe_plan(idx, m, tm, n_exp)
    xs = _moe_permute(x, dest, fill_lo, fill_hi, 2 * m + n_exp * tm)
    ys = _moe_experts(xs, tile_exp, n_valid, tm, w_gate, w_up, w_down)
    return _moe_combine_deepnorm(x, ys, dest, wgt, ln_g, ln_b, alpha)


def _proj_pairs_kernel(transposed, x_ref, w_ref, o_ref, xb_s):
    @pl.when(pl.program_id(1) == 0)
    def _():
        xb_s[...] = x_ref[...].astype(BF16)

    n_slab = o_ref.shape[0]
    if transposed:
        res = _dot_nt(w_ref[...], xb_s[...])
        for q in range(n_slab):
            o_ref[q] = res[q * LANES:(q + 1) * LANES, :].astype(o_ref.dtype)
    else:
        res = _dot(xb_s[...], w_ref[...])
        for q in range(n_slab):
            o_ref[q] = res[:, q * LANES:(q + 1) * LANES].astype(o_ref.dtype)


def _proj_pairs(x, w, transposed, out_dtype=F32):
    m, d = x.shape
    n = w.shape[1]
    tm = _tile(m, 512)
    tn = _tile(n, 1024)
    n_slab = tn // LANES
    if transposed:
        wb = w.T.astype(BF16)
        w_spec = pl.BlockSpec((tn, d), lambda i, j: (j, 0))
        o_spec = pl.BlockSpec((n_slab, LANES, tm), lambda i, j: (j, 0, i))
        o_shape = jax.ShapeDtypeStruct((n // LANES, LANES, m), out_dtype)
    else:
        wb = w.astype(BF16)
        w_spec = pl.BlockSpec((d, tn), lambda i, j: (0, j))
        o_spec = pl.BlockSpec((n_slab, tm, LANES), lambda i, j: (j, i, 0))
        o_shape = jax.ShapeDtypeStruct((n // LANES, m, LANES), out_dtype)
    return pl.pallas_call(
        functools.partial(_proj_pairs_kernel, transposed),
        grid=(m // tm, n // tn),
        in_specs=[pl.BlockSpec((tm, d), lambda i, j: (i, 0)), w_spec],
        out_specs=o_spec,
        out_shape=o_shape,
        scratch_shapes=[pltpu.VMEM((tm, d), BF16)],
        compiler_params=_cparams(("parallel", "arbitrary")),
        name="proj_pairs_t" if transposed else "proj_pairs",
    )(x, wb)


def _block_mean_kernel(k_ref, o_ref):
    k = k_ref[0]
    nb = k.shape[0] // MOBA_BLOCK
    o_ref[0] = jnp.mean(k.reshape(nb, MOBA_BLOCK, LANES), axis=1)


def _block_means(k_pairs, batch):
    n_pair, m, _ = k_pairs.shape
    seq = m // batch
    nb = seq // MOBA_BLOCK
    return pl.pallas_call(
        _block_mean_kernel,
        grid=(n_pair, batch),
        in_specs=[pl.BlockSpec((1, seq, LANES), lambda p, b: (p, b, 0))],
        out_specs=pl.BlockSpec((1, nb, LANES), lambda p, b: (p, b, 0)),
        out_shape=jax.ShapeDtypeStruct((n_pair, batch * nb, LANES), F32),
        compiler_params=_cparams(("parallel", "parallel")),
        name="moba_block_means",
    )(k_pairs)


def _moba_kernel(scale, qt_ref, k_ref, vt_ref, km_ref, o_ref, neg_s, sca_s, scb_s, m_s, acc_s):
    own = pl.program_id(2)
    blk = MOBA_BLOCK
    n_slab = qt_ref.shape[0]
    nb = km_ref.shape[1]
    row = lax.broadcasted_iota(jnp.int32, (LANES, blk), 0)
    zero_q = jnp.zeros((LANES, blk), F32)
    n_iota = lax.broadcasted_iota(jnp.int32, (nb, blk), 0).astype(F32)
    past = n_iota < own.astype(F32)
    neg_inf = jnp.float32(-jnp.inf)

    streams = [(g, h) for g in range(n_slab) for h in range(2)]
    qh = []
    for s, (g, h) in enumerate(streams):
        in_head = (row < HEAD_DIM) if h == 0 else (row >= HEAD_DIM)
        q_h = jnp.where(in_head, qt_ref[g], zero_q)
        qh.append((q_h * (scale * LOG2E)).astype(BF16))
        gate = _dot_sp(_split(km_ref[g]), _split(q_h))
        gate = jnp.where(past, gate, neg_inf)
        neg = jnp.full((nb, blk), NEG_BIG, F32)
        for _ in range(min(MOBA_TOPK, nb)):
            mx = jnp.max(gate, axis=0, keepdims=True)
            idx = jnp.min(jnp.where(gate == mx, n_iota, float(nb)), axis=0, keepdims=True)
            pick = n_iota == idx
            neg = jnp.where(jnp.logical_and(pick, past), 0.0, neg)
            gate = jnp.where(pick, neg_inf, gate)
        neg_s[s] = neg

    ones_rows = jnp.ones((DEN_ROWS, blk), BF16)

    def block_scores(n, s):
        start = pl.multiple_of(n * blk, blk)
        return _dot(k_ref[streams[s][0], pl.ds(start, blk), :], qh[s])

    def absorb(n, st, bias, s):
        g, h = streams[s]
        start = pl.multiple_of(n * blk, blk)
        mx = m_s[s]
        mblk = jnp.max(st, axis=0, keepdims=True)
        if bias is not None:
            mblk = mblk + bias
        mx_new = jnp.maximum(mx, mblk)
        alpha = jnp.exp2(mx - mx_new)
        shift = mx_new if bias is None else mx_new - bias
        p = jnp.exp2(st - shift).astype(BF16)
        vtb = vt_ref[g, h * HEAD_DIM:(h + 1) * HEAD_DIM, pl.ds(start, blk)]
        m_s[s] = mx_new
        acc_s[s] = alpha * acc_s[s] + _dot(jnp.concatenate([vtb, ones_rows], axis=0), p)

    kpos = lax.broadcasted_iota(jnp.int32, (blk, blk), 0)
    qpos = lax.broadcasted_iota(jnp.int32, (blk, blk), 1)
    causal = kpos <= qpos
    own_scores = [block_scores(own, s) for s in range(len(streams))]
    for s in range(len(streams)):
        sca_s[s] = block_scores(0, s)
        m_s[s] = jnp.full((1, blk), neg_inf, F32)
        acc_s[s] = jnp.zeros((HEAD_DIM + DEN_ROWS, blk), F32)
    for s in range(len(streams)):
        absorb(own, jnp.where(causal, own_scores[s], neg_inf), None, s)

    def body(j, carry):
        n0 = 2 * j
        n1 = n0 + 1
        n2 = jnp.minimum(n0 + 2, nb - 1)
        for s in range(len(streams)):
            scb_s[s] = block_scores(n1, s)
            absorb(n0, sca_s[s], neg_s[s, pl.ds(n0, 1), :], s)
        for s in range(len(streams)):
            sca_s[s] = block_scores(n2, s)
            absorb(n1, scb_s[s], neg_s[s, pl.ds(n1, 1), :], s)
        return carry

    lax.fori_loop(0, own // 2, body, 0)

    @pl.when(own % 2 == 1)
    def _():
        for s in range(len(streams)):
            absorb(own - 1, sca_s[s], neg_s[s, pl.ds(own - 1, 1), :], s)

    for g in range(n_slab):
        a0, a1 = acc_s[2 * g], acc_s[2 * g + 1]
        o_t = jnp.concatenate([a0[:HEAD_DIM] / a0[HEAD_DIM:HEAD_DIM + 1],
                               a1[:HEAD_DIM] / a1[HEAD_DIM:HEAD_DIM + 1]], axis=0)
        o_ref[g] = o_t.T.astype(o_ref.dtype)


def _moba_attention(q_t, k, v_t, k_means, batch):
    n_pair, _, m = q_t.shape
    seq = m // batch
    nb = seq // MOBA_BLOCK
    g = _tile(n_pair, MOBA_SLABS)
    return pl.pallas_call(
        functools.partial(_moba_kernel, HEAD_DIM ** -0.5),
        grid=(n_pair // g, batch, nb),
        in_specs=[pl.BlockSpec((g, LANES, MOBA_BLOCK), lambda p, b, i: (p, 0, b * nb + i)),
                  pl.BlockSpec((g, seq, LANES), lambda p, b, i: (p, b, 0),
                               pipeline_mode=pl.Buffered(1)),
                  pl.BlockSpec((g, LANES, seq), lambda p, b, i: (p, 0, b),
                               pipeline_mode=pl.Buffered(1)),
                  pl.BlockSpec((g, nb, LANES), lambda p, b, i: (p, b, 0))],
        out_specs=pl.BlockSpec((g, MOBA_BLOCK, LANES), lambda p, b, i: (p, b * nb + i, 0)),
        out_shape=jax.ShapeDtypeStruct((n_pair, m, LANES), BF16),
        scratch_shapes=[pltpu.VMEM((2 * g, nb, MOBA_BLOCK), F32),
                        pltpu.VMEM((2 * g, MOBA_BLOCK, MOBA_BLOCK), F32),
                        pltpu.VMEM((2 * g, MOBA_BLOCK, MOBA_BLOCK), F32),
                        pltpu.VMEM((2 * g, 1, MOBA_BLOCK), F32),
                        pltpu.VMEM((2 * g, HEAD_DIM + DEN_ROWS, MOBA_BLOCK), F32)],
        compiler_params=_cparams(("parallel", "parallel", "arbitrary")),
        name="moba_attention",
    )(q_t, k, v_t, k_means)


def kernel(x, rwkv_mu, rwkv_w_rkv, rwkv_w_out, rwkv_decay_w0, rwkv_decay_w1, rwkv_decay_w2, rwkv_iclr_a0, rwkv_iclr_a1, rwkv_iclr_a2, rwkv_vres_v0, rwkv_vres_v1, rwkv_vres_v2, rwkv_gate_g1, rwkv_gate_g2, rwkv_k_k, rwkv_k_a, rwkv_r_k, rwkv_gn_g, rwkv_gn_b, moba_w_k, moba_w_v, moba_w_q, moba_w_o, ffn_w_gate, ffn_w_up, ffn_w_down, moe_router, moe_w_gate, moe_w_up, moe_w_down, ln_g, ln_b):
    batch, seq, d = x.shape
    assert d % (2 * LANES) == 0 and seq % MOBA_BLOCK == 0 and seq % WKV_CHUNK == 0
    depth = ln_g.shape[0]
    n_rwkv = rwkv_mu.shape[0]
    alpha = (2.0 * depth) ** 0.25
    h = x.reshape(batch * seq, d)
    v_first = None
    kv = None
    for layer in range(depth):
        if layer < n_rwkv:
            i = layer
            vres = None if i == 0 else (rwkv_vres_v0[i - 1], rwkv_vres_v1[i - 1], rwkv_vres_v2[i - 1])
            r, e, k, v, a, b, g = _rwkv_proj(
                h, seq, rwkv_mu[i], rwkv_w_rkv[i], rwkv_decay_w1[i], rwkv_iclr_a1[i],
                rwkv_gate_g1[i], rwkv_decay_w2[i], rwkv_iclr_a2[i], rwkv_gate_g2[i],
                rwkv_decay_w0[i], rwkv_iclr_a0[i], rwkv_k_k[i], rwkv_k_a[i],
                vres=vres, v_first=v_first)
            if i == 0:
                v_first = v
            y = _wkv_scan(r, e, k, v, a, b, batch)
            mix = _rwkv_post(y, r, k, v, g, rwkv_r_k[i], rwkv_gn_g[i], rwkv_gn_b[i])
            w_out = rwkv_w_out[i]
        else:
            jdx = layer - n_rwkv
            k_pairs, v_t, k_means = kv
            q_t = _proj_pairs(h, moba_w_q[jdx], transposed=True)
            mix = _moba_attention(q_t, k_pairs, v_t, k_means, batch)
            w_out = moba_w_o[jdx]
        h = _out_proj_deepnorm(mix, w_out, h, ln_g[layer, 0], ln_b[layer, 0], alpha)
        ex = layer // 2
        if layer % 2 == 0:
            h = _ffn_deepnorm(h, _layer_bf16(ffn_w_gate, ex), _layer_bf16(ffn_w_up, ex),
                              _layer_bf16(ffn_w_down, ex), ln_g[layer, 1], ln_b[layer, 1], alpha)
        else:
            h = _moe_deepnorm(h, moe_router[ex], _layer_bf16(moe_w_gate, ex),
                              _layer_bf16(moe_w_up, ex), _layer_bf16(moe_w_down, ex),
                              ln_g[layer, 1], ln_b[layer, 1], alpha)
        if layer == n_rwkv - 1:
            k_pairs = _proj_pairs(h, moba_w_k, transposed=False)
            v_t = _proj_pairs(h, moba_w_v, transposed=True, out_dtype=BF16)
            kv = (k_pairs.astype(BF16), v_t, _block_means(k_pairs, batch))
    return h.reshape(batch, seq, d)
```

```python
import functools
import math

import jax
import jax.numpy as jnp
from jax import lax
from jax.experimental import pallas as pl
from jax.experimental.pallas import tpu as pltpu

HEAD_DIM = 64
LANES = 128
MXU_DEPTH = 256
GN_EPS = 64e-5
LN_EPS = 1e-5
MOBA_BLOCK = 256
MOBA_TOPK = 3
MOE_ROWS = 512
FFN_TILE = 2816
CAST_BLOCK_ELEMS = 1 << 20
WKV_CHUNK = 64
WKV_SLABS = 8
NEG_BIG = -1e30
MOBA_SLABS = 8
DEN_ROWS = 16
LOG2E = 1.4426950408889634

F32 = jnp.float32
BF16 = jnp.bfloat16
VMEM_LIMIT = 56 * 1024 * 1024


def _cparams(sem):
    return pltpu.CompilerParams(dimension_semantics=sem, vmem_limit_bytes=VMEM_LIMIT)


def _dot(a, b):
    return jnp.dot(a, b, preferred_element_type=F32)


def _dot_nt(a, b):
    return lax.dot_general(a, b, (((1,), (1,)), ((), ())), preferred_element_type=F32)


def _split(x):
    hi = x.astype(BF16)
    lo = (x - hi.astype(F32)).astype(BF16)
    return hi, lo


def _dot_sp(a, b, nt=False):
    d = _dot_nt if nt else _dot
    return d(a[0], b[0]) + (d(a[0], b[1]) + d(a[1], b[0]))


def _dot_exact_rhs(a, b_exact, nt=False):
    d = _dot_nt if nt else _dot
    hi, lo = _split(a)
    return d(hi, b_exact) + d(lo, b_exact)


def _sigmoid(x):
    return 1.0 / (1.0 + jnp.exp(-x))


def _layer_norm(y, g, b):
    mu = jnp.mean(y, axis=-1, keepdims=True)
    yc = y - mu
    var = jnp.mean(yc * yc, axis=-1, keepdims=True)
    return yc * lax.rsqrt(var + LN_EPS) * g + b


def _head_blockdiag(n):
    i = jnp.arange(n) // HEAD_DIM
    return (i[:, None] == i[None, :]).astype(BF16)


def _tile(n, want):
    t = min(n, want)
    assert n % t == 0, (n, want)
    return t


def _cast_kernel(w_ref, o_ref):
    o_ref[...] = w_ref[...].astype(o_ref.dtype)


def _layer_bf16(w, layer):
    shape = w.shape[1:]
    c = shape[-1]
    rows = math.prod(shape[:-1])
    tr = rows
    for cand in range(min(rows, CAST_BLOCK_ELEMS // c) // 16, 0, -1):
        if rows % (cand * 16) == 0:
            tr = cand * 16
            break
    out = pl.pallas_call(
        _cast_kernel,
        grid=(rows // tr,),
        in_specs=[pl.BlockSpec((1, tr, c), lambda i: (layer, i, 0))],
        out_specs=pl.BlockSpec((1, tr, c), lambda i: (0, i, 0)),
        out_shape=jax.ShapeDtypeStruct((1, rows, c), BF16),
        compiler_params=_cparams(("parallel",)),
        name="weight_to_bf16",
    )(w.reshape(w.shape[0], rows, c))
    return out.reshape(shape)


def _rwkv_proj_kernel(has_vres, steps_per_seq, *refs):
    if has_vres:
        (x_ref, xp_ref, mu_ref, wrkv_ref, wd1_ref, wa1_ref, wg1_ref, wv1_ref,
         wd2_ref, wa2_ref, wg2_ref, wv2_ref, w0_ref, a0_ref, v0_ref, kk_ref, ka_ref,
         bd_ref, vf_ref,
         r_out, e_out, k_out, v_out, a_out, b_out, g_out,
         xm_s, hd_s, ha_s, hg_s, hv_s) = refs
    else:
        (x_ref, xp_ref, mu_ref, wrkv_ref, wd1_ref, wa1_ref, wg1_ref,
         wd2_ref, wa2_ref, wg2_ref, w0_ref, a0_ref, kk_ref, ka_ref,
         bd_ref,
         r_out, e_out, k_out, v_out, a_out, b_out, g_out,
         xm_s, hd_s, ha_s, hg_s) = refs
    i = pl.program_id(0)
    j = pl.program_id(1)

    @pl.when(j == 0)
    def _():
        x = x_ref[...]
        tm = x.shape[0]
        prev_row = jnp.where(i % steps_per_seq == 0, 0.0, xp_ref[7:8, :])
        rolled = pltpu.roll(x, 1, axis=0)
        row = lax.broadcasted_iota(jnp.int32, (tm, 1), 0)
        x_prev = jnp.where(row == 0, prev_row, rolled)
        xx = x_prev - x
        for c in range(3):
            xm_s[c] = (x + xx * mu_ref[c:c + 1, :]).astype(BF16)
        xw = (x + xx * mu_ref[3:4, :]).astype(BF16)
        xa = (x + xx * mu_ref[4:5, :]).astype(BF16)
        xg = (x + xx * mu_ref[5:6, :]).astype(BF16)
        hd_s[...] = jnp.tanh(_dot(xw, wd1_ref[...])).astype(BF16)
        ha_s[...] = _dot(xa, wa1_ref[...]).astype(BF16)
        hg_s[...] = _sigmoid(_dot(xg, wg1_ref[...])).astype(BF16)
        if has_vres:
            hv_s[...] = _dot(xm_s[2], wv1_ref[...]).astype(BF16)

    r = _dot(xm_s[0], wrkv_ref[0])
    k = _dot(xm_s[1], wrkv_ref[1])
    v = _dot(xm_s[2], wrkv_ref[2])
    z = w0_ref[...] + _dot(hd_s[...], wd2_ref[...])
    nz = -z
    softplus = jnp.maximum(nz, 0.0) + jnp.log(1.0 + jnp.exp(-jnp.abs(nz)))
    e = jnp.exp(-softplus - 0.5)
    a = _sigmoid(a0_ref[...] + _dot(ha_s[...], wa2_ref[...]))
    g = _dot(hg_s[...], wg2_ref[...])
    n_slab = r.shape[1] // LANES
    if has_vres:
        vf = jnp.concatenate([vf_ref[q] for q in range(n_slab)], axis=1)
        v = v + (vf - v) * _sigmoid(v0_ref[...] + _dot(hv_s[...], wv2_ref[...]))
    kk = k * kk_ref[...]
    ss = _dot_exact_rhs(kk * kk, bd_ref[...])
    kk = kk * lax.rsqrt(jnp.maximum(ss, 1e-24))
    k = k * (1.0 + (a - 1.0) * ka_ref[...])
    for q in range(n_slab):
        sl = slice(q * LANES, (q + 1) * LANES)
        r_out[q] = r[:, sl].astype(r_out.dtype)
        e_out[q] = e[:, sl]
        k_out[q] = k[:, sl].astype(k_out.dtype)
        v_out[q] = v[:, sl].astype(v_out.dtype)
        a_out[q] = (-kk)[:, sl].astype(a_out.dtype)
        b_out[q] = (kk * a)[:, sl].astype(b_out.dtype)
        g_out[q] = g[:, sl].astype(g_out.dtype)


def _rwkv_proj(x, seq_len, mu, w_rkv, wd1, wa1, wg1, wd2, wa2, wg2, w0, a0, k_k, k_a,
               vres=None, v_first=None):
    m, d = x.shape
    tm = _tile(seq_len, 512)
    tn = _tile(d, 256)
    n_slab = tn // LANES
    has_vres = vres is not None
    row = lambda a: a.reshape(1, d)
    full = lambda a: pl.BlockSpec(a.shape, lambda i, j: (0,) * a.ndim)
    colblk = lambda rows: pl.BlockSpec((rows, tn), lambda i, j: (0, j))
    bd = _head_blockdiag(tn)
    wd1, wa1, wg1 = wd1.astype(BF16), wa1.astype(BF16), wg1.astype(BF16)
    wd2, wa2, wg2 = wd2.astype(BF16), wa2.astype(BF16), wg2.astype(BF16)
    w_rkv = w_rkv.astype(BF16)
    args = [x, x, mu, w_rkv, wd1, wa1, wg1]
    specs = [pl.BlockSpec((tm, d), lambda i, j: (i, 0)),
             pl.BlockSpec((8, d), lambda i, j: (jnp.maximum(i * (tm // 8) - 1, 0), 0)),
             full(mu),
             pl.BlockSpec((3, d, tn), lambda i, j: (0, 0, j)),
             full(wd1), full(wa1), full(wg1)]
    if has_vres:
        v0, wv1, wv2 = vres
        wv1, wv2 = wv1.astype(BF16), wv2.astype(BF16)
        args += [wv1]
        specs += [full(wv1)]
    args += [wd2, wa2, wg2]
    specs += [colblk(wd2.shape[0]), colblk(wa2.shape[0]), colblk(wg2.shape[0])]
    if has_vres:
        args += [wv2]
        specs += [colblk(wv2.shape[0])]
    args += [row(w0), row(a0)]
    specs += [colblk(1), colblk(1)]
    if has_vres:
        args += [row(v0)]
        specs += [colblk(1)]
    args += [row(k_k), row(k_a), bd]
    specs += [colblk(1), colblk(1), full(bd)]
    slab_spec = pl.BlockSpec((n_slab, tm, LANES), lambda i, j: (j, i, 0))
    if has_vres:
        args += [v_first]
        specs += [slab_spec]
    slab = lambda dt: jax.ShapeDtypeStruct((d // LANES, m, LANES), dt)
    scratch = [pltpu.VMEM((3, tm, d), BF16),
               pltpu.VMEM((tm, wd1.shape[1]), BF16),
               pltpu.VMEM((tm, wa1.shape[1]), BF16),
               pltpu.VMEM((tm, wg1.shape[1]), BF16)]
    if has_vres:
        scratch.append(pltpu.VMEM((tm, wv1.shape[1]), BF16))
    return pl.pallas_call(
        functools.partial(_rwkv_proj_kernel, has_vres, seq_len // tm),
        grid=(m // tm, d // tn),
        in_specs=specs,
        out_specs=[slab_spec] * 7,
        out_shape=[slab(BF16), slab(F32)] + [slab(BF16)] * 5,
        scratch_shapes=scratch,
        compiler_params=_cparams(("parallel", "arbitrary")),
        name="rwkv_proj",
    )(*args)


def _wkv_kernel(r_ref, e_ref, k_ref, v_ref, a_ref, b_ref, y_ref, s_ref):
    c = WKV_CHUNK
    n_slab, n_batch, tb, _ = r_ref.shape
    n_chunks = tb // c

    @pl.when(pl.program_id(1) == 0)
    def _():
        s_ref[...] = jnp.zeros_like(s_ref)

    lane = lax.broadcasted_iota(jnp.int32, (c, LANES), 1)
    t_idx = lax.broadcasted_iota(jnp.int32, (c, LANES), 0)
    head0 = lane < HEAD_DIM
    s_idx = jnp.bitwise_and(lane, HEAD_DIM - 1)
    strict = s_idx < t_idx
    incl = s_idx <= t_idx
    rr = lax.broadcasted_iota(jnp.int32, (LANES, LANES), 0)
    cc = lax.broadcasted_iota(jnp.int32, (LANES, LANES), 1)
    same_head = (rr < HEAD_DIM) == (cc < HEAD_DIM)

    def bd(x):
        z = jnp.zeros_like(x)
        return jnp.concatenate([jnp.where(head0, x, z), jnp.where(head0, z, x)], axis=0)

    def cat(a, b, axis=0):
        return jnp.concatenate([a, b], axis=axis)

    def chunk(ci, carry):
        stages = [one_slab(ci, p, bb) for p in range(n_slab) for bb in range(n_batch)]
        while stages:
            stages = [g for g in stages if next(g, "done") != "done"]
        return carry

    def one_slab(ci, p, bb):
        sl = pl.ds(pl.multiple_of(ci * c, c), c)
        r = r_ref[p, bb, sl, :]
        e = e_ref[p, bb, sl, :]
        k = k_ref[p, bb, sl, :]
        v = v_ref[p, bb, sl, :]
        a = a_ref[p, bb, sl, :]
        b = b_ref[p, bb, sl, :]
        s0 = s_ref[p * n_batch + bb]
        cum = _cumsum_rows(e, t_idx)
        yield
        tot = cum[c - 1:c, :]
        ar = cat(a * jnp.exp(e - cum), r * jnp.exp(-cum)).astype(BF16)
        ec = jnp.exp(cum)
        bt = (b * ec).astype(BF16)
        kt = (k * ec).astype(BF16)
        eh = jnp.exp(cum - tot)
        bk = cat(b * eh, k * eh).astype(BF16)
        bdv = bd(v.astype(BF16))
        gram = _dot_nt(ar, cat(bd(bt), bd(kt)))
        yield
        zero = jnp.zeros((c, LANES), F32)
        l_ab = jnp.where(strict, gram[:c, :LANES], zero)
        n_ak = jnp.where(strict, gram[:c, LANES:], zero)
        m_rb = jnp.where(incl, gram[c:, :LANES], zero)
        m_rk = jnp.where(incl, gram[c:, LANES:], zero)
        xy = _dot_nt(ar, s0.astype(BF16)) + _dot(cat(n_ak, m_rk).astype(BF16), bdv)
        x = xy[:c]
        y0 = xy[c:]
        yield
        lp = l_ab.astype(BF16)
        n_steps = int(math.log2(c))
        for step in range(n_steps):
            bdx = bd(x.astype(BF16))
            if step + 1 < n_steps:
                t = _dot(lp, cat(bdx, bd(lp), axis=1))
                x = x + t[:, :LANES]
                lp = t[:, LANES:].astype(BF16)
            else:
                x = x + _dot(lp, bdx)
            yield
        y_ref[p, bb, sl, :] = y0 + _dot(m_rb.astype(BF16), bd(x.astype(BF16)))
        uv_t = cat(x, v.astype(F32)).T.astype(BF16)
        upd = _dot(uv_t, bk)
        s_ref[p * n_batch + bb] = (s0 * jnp.exp(-tot)
                                   + jnp.where(same_head, upd, jnp.zeros_like(upd)))
        yield

    lax.fori_loop(0, n_chunks, chunk, 0)


def _cumsum_rows(e, row_idx):
    cum = e
    shift = 1
    while shift < e.shape[0]:
        cum = cum + jnp.where(row_idx >= shift, pltpu.roll(cum, shift, axis=0), 0.0)
        shift *= 2
    return cum


def _wkv_scan(r, e, k, v, a, b, batch):
    n_pair, m, _ = r.shape
    seq = m // batch
    tb = _tile(seq, 256)
    n_slab = _tile(n_pair, WKV_SLABS)
    view = lambda t: t.reshape(n_pair, batch, seq, LANES)
    spec = pl.BlockSpec((n_slab, batch, tb, LANES), lambda p, t: (p, 0, t, 0))
    y = pl.pallas_call(
        _wkv_kernel,
        grid=(n_pair // n_slab, seq // tb),
        in_specs=[spec] * 6,
        out_specs=spec,
        out_shape=jax.ShapeDtypeStruct((n_pair, batch, seq, LANES), F32),
        scratch_shapes=[pltpu.VMEM((n_slab * batch, LANES, LANES), F32)],
        compiler_params=_cparams(("parallel", "arbitrary")),
        name="wkv_scan",
    )(view(r), view(e), view(k), view(v), view(a), view(b))
    return y.reshape(n_pair, m, LANES)


def _rwkv_post_kernel(y_ref, r_ref, k_ref, v_ref, g_ref, rk_ref, gg_ref, gb_ref, bd_ref, z_ref):
    y = y_ref[0]
    bd2 = bd_ref[...]
    bdm = bd2[:LANES, :LANES]
    inv_n = 1.0 / HEAD_DIM
    rk = r_ref[0].astype(F32) * k_ref[0].astype(F32) * rk_ref[0]
    sums = _dot_exact_rhs(jnp.concatenate([y, rk], axis=1), bd2)
    yc = y - sums[:, :LANES] * inv_n
    var = _dot_exact_rhs(yc * yc, bdm) * inv_n
    yn = yc * lax.rsqrt(var + GN_EPS) * gg_ref[0] + gb_ref[0]
    bonus = sums[:, LANES:] * v_ref[0].astype(F32)
    z_ref[0] = ((yn + bonus) * g_ref[0].astype(F32)).astype(z_ref.dtype)


def _rwkv_post(y, r, k, v, g, r_k, gn_g, gn_b):
    n_pair, m, _ = y.shape
    tm = _tile(m, 2048)
    spec = pl.BlockSpec((1, tm, LANES), lambda p, i: (p, i, 0))
    pspec = pl.BlockSpec((1, 1, LANES), lambda p, i: (p, 0, 0))
    bd = _head_blockdiag(2 * LANES)
    slab = lambda a: a.reshape(n_pair, 1, LANES)
    return pl.pallas_call(
        _rwkv_post_kernel,
        grid=(n_pair, m // tm),
        in_specs=[spec] * 5 + [pspec] * 3 + [pl.BlockSpec(bd.shape, lambda p, i: (0, 0))],
        out_specs=spec,
        out_shape=jax.ShapeDtypeStruct((n_pair, m, LANES), BF16),
        compiler_params=_cparams(("parallel", "parallel")),
        name="rwkv_post",
    )(y, r, k, v, g, slab(r_k), slab(gn_g), slab(gn_b), bd)


def _out_proj_kernel(alpha, z_ref, w_ref, x_ref, g_ref, b_ref, o_ref):
    per = w_ref.shape[1] // LANES
    acc = None
    for c in range(w_ref.shape[0]):
        zc = jnp.concatenate([z_ref[c * per + q] for q in range(per)], axis=1)
        part = _dot(zc, w_ref[c])
        acc = part if acc is None else acc + part
    o_ref[...] = _layer_norm(alpha * x_ref[...] + acc, g_ref[...], b_ref[...])


def _out_proj_deepnorm(z, w, x, ln_g, ln_b, alpha):
    n_pair, m, _ = z.shape
    d = w.shape[1]
    tm = _tile(m, 1024)
    kc = _tile(d, MXU_DEPTH)
    w3 = w.astype(BF16).reshape(d // kc, kc, d)
    return pl.pallas_call(
        functools.partial(_out_proj_kernel, alpha),
        grid=(m // tm,),
        in_specs=[pl.BlockSpec((n_pair, tm, LANES), lambda i: (0, i, 0)),
                  pl.BlockSpec(w3.shape, lambda i: (0, 0, 0)),
                  pl.BlockSpec((tm, d), lambda i: (i, 0)),
                  pl.BlockSpec((1, d), lambda i: (0, 0)),
                  pl.BlockSpec((1, d), lambda i: (0, 0))],
        out_specs=pl.BlockSpec((tm, d), lambda i: (i, 0)),
        out_shape=jax.ShapeDtypeStruct((m, d), F32),
        compiler_params=_cparams(("parallel",)),
        name="out_proj_deepnorm",
    )(z, w3, x, ln_g.reshape(1, d), ln_b.reshape(1, d))


def _ff_tile(ff, want):
    for cand in range(min(ff, want) // LANES, 0, -1):
        if ff % (cand * LANES) == 0:
            return cand * LANES
    return ff


def _swiglu_partial(xb, wg, wu, wd):
    h1 = _dot(xb, wg)
    h2 = _dot(xb, wu)
    act = h1 * _sigmoid(h1) * h2
    return _dot(act.astype(BF16), wd)


def _ffn_kernel(alpha, x_ref, wg_ref, wu_ref, wd_ref, g_ref, b_ref, o_ref, xb_s, acc_s):
    f = pl.program_id(1)

    @pl.when(f == 0)
    def _():
        xb_s[...] = x_ref[...].astype(BF16)
        acc_s[...] = jnp.zeros_like(acc_s)

    acc_s[...] += _swiglu_partial(xb_s[...], wg_ref[...], wu_ref[...], wd_ref[...])

    @pl.when(f == pl.num_programs(1) - 1)
    def _():
        o_ref[...] = _layer_norm(alpha * x_ref[...] + acc_s[...], g_ref[...], b_ref[...])


def _ffn_deepnorm(x, w_gate, w_up, w_down, ln_g, ln_b, alpha):
    m, d = x.shape
    ff = w_gate.shape[1]
    tm = _tile(m, 512)
    tf = _ff_tile(ff, FFN_TILE)
    wmode = dict(pipeline_mode=pl.Buffered(1)) if tf == ff else {}
    return pl.pallas_call(
        functools.partial(_ffn_kernel, alpha),
        grid=(m // tm, ff // tf),
        in_specs=[pl.BlockSpec((tm, d), lambda i, f: (i, 0)),
                  pl.BlockSpec((d, tf), lambda i, f: (0, f), **wmode),
                  pl.BlockSpec((d, tf), lambda i, f: (0, f), **wmode),
                  pl.BlockSpec((tf, d), lambda i, f: (f, 0), **wmode),
                  pl.BlockSpec((1, d), lambda i, f: (0, 0)),
                  pl.BlockSpec((1, d), lambda i, f: (0, 0))],
        out_specs=pl.BlockSpec((tm, d), lambda i, f: (i, 0)),
        out_shape=jax.ShapeDtypeStruct((m, d), F32),
        scratch_shapes=[pltpu.VMEM((tm, d), BF16), pltpu.VMEM((tm, d), F32)],
        compiler_params=_cparams(("parallel", "arbitrary")),
        name="ffn_deepnorm",
    )(x, w_gate.astype(BF16), w_up.astype(BF16), w_down.astype(BF16),
      ln_g.reshape(1, d), ln_b.reshape(1, d))


def _router_kernel(n_exp, x_ref, w_ref, idx_ref, wgt_ref):
    logits = _dot_sp(_split(x_ref[...]), _split(w_ref[...]))
    lane = lax.broadcasted_iota(jnp.int32, logits.shape, 1).astype(F32)
    neg_inf = jnp.float32(-jnp.inf)
    lg = jnp.where(lane < n_exp, logits, neg_inf)
    m1 = jnp.max(lg, axis=1, keepdims=True)
    i1 = jnp.min(jnp.where(lg == m1, lane, float(LANES)), axis=1, keepdims=True)
    lg2 = jnp.where(lane == i1, neg_inf, lg)
    m2 = jnp.max(lg2, axis=1, keepdims=True)
    i2 = jnp.min(jnp.where(lg2 == m2, lane, float(LANES)), axis=1, keepdims=True)
    e2 = jnp.exp(m2 - m1)
    den = 1.0 + e2
    idx_ref[...] = jnp.where(lane == 0.0, i1, jnp.where(lane == 1.0, i2, 0.0)).astype(jnp.int32)
    wgt_ref[...] = jnp.where(lane == 0.0, 1.0 / den, jnp.where(lane == 1.0, e2 / den, 0.0))


def _router_top2(x, router):
    m, d = x.shape
    n_exp = router.shape[1]
    tm = _tile(m, 512)
    w = jnp.pad(router, ((0, 0), (0, LANES - n_exp)))
    out_spec = pl.BlockSpec((tm, LANES), lambda i: (i, 0))
    return pl.pallas_call(
        functools.partial(_router_kernel, n_exp),
        grid=(m // tm,),
        in_specs=[pl.BlockSpec((tm, d), lambda i: (i, 0)),
                  pl.BlockSpec((d, LANES), lambda i: (0, 0))],
        out_specs=[out_spec, out_spec],
        out_shape=[jax.ShapeDtypeStruct((m, LANES), jnp.int32),
                   jax.ShapeDtypeStruct((m, LANES), F32)],
        compiler_params=_cparams(("parallel",)),
        name="moe_router",
    )(x, w)


def _moe_plan(idx, m, tm, n_exp):
    n_ent = 2 * m
    expert = jnp.concatenate([idx[:, 0], idx[:, 1]])
    onehot = (expert[:, None] == jnp.arange(n_exp, dtype=jnp.int32)[None, :]).astype(jnp.int32)
    csum = jnp.cumsum(onehot, axis=0)
    counts = csum[-1]
    rank = jnp.sum(csum * onehot, axis=1) - 1
    padded = ((counts + tm - 1) // tm) * tm
    ends = jnp.cumsum(padded)
    starts = ends - padded
    dest = jnp.sum(starts[None, :] * onehot, axis=1) + rank
    n_tiles = n_ent // tm + n_exp
    tile_start = jnp.arange(n_tiles, dtype=jnp.int32) * tm
    tile_exp = jnp.minimum(jnp.sum((tile_start[:, None] >= ends[None, :]).astype(jnp.int32), axis=1),
                           n_exp - 1)
    n_valid = jnp.clip(starts[tile_exp] + counts[tile_exp] - tile_start, 0, tm)
    fill_lo = jnp.concatenate([starts + counts, ends[-1:]])
    fill_hi = jnp.concatenate([ends, jnp.full((1,), n_tiles * tm, jnp.int32)])
    return dest, tile_exp, n_valid, fill_lo, fill_hi


def _moe_permute_kernel(lo_ref, hi_ref, dest_ref, x_ref, xs_hbm, sem):
    i = pl.program_id(0)
    te = dest_ref.shape[2]

    def entry_row(r):
        return pltpu.make_async_copy(x_ref.at[pl.ds(r, 1), :],
                                     xs_hbm.at[pl.ds(dest_ref[0, 0, r], 1), :], sem.at[0])

    def filler_row(r):
        return pltpu.make_async_copy(x_ref.at[pl.ds(0, 1), :], xs_hbm.at[pl.ds(r, 1), :], sem.at[1])

    def each(lo, hi, fn, unroll=1):
        def body(r, c):
            fn(r)
            return c
        lax.fori_loop(lo, hi, body, 0, unroll=unroll)

    each(0, te, lambda r: entry_row(r).start(), unroll=8)

    @pl.when(i == 0)
    def _():
        for e in range(lo_ref.shape[0]):
            each(lo_ref[e], hi_ref[e], lambda r: filler_row(r).start())
        for e in range(lo_ref.shape[0]):
            each(lo_ref[e], hi_ref[e], lambda r: filler_row(r).wait())

    each(0, te, lambda r: entry_row(r).wait(), unroll=8)


def _moe_permute(x, dest, fill_lo, fill_hi, n_rows):
    m, d = x.shape
    te = _tile(m, 512)
    n_x = m // te
    dest3 = dest.reshape(-1, 1, te)
    grid_spec = pltpu.PrefetchScalarGridSpec(
        num_scalar_prefetch=2,
        grid=(dest3.shape[0],),
        in_specs=[pl.BlockSpec((1, 1, te), lambda i, lo, hi: (i, 0, 0), memory_space=pltpu.SMEM),
                  pl.BlockSpec((te, d), lambda i, lo, hi: (i % n_x, 0))],
        out_specs=pl.BlockSpec(memory_space=pl.ANY),
        scratch_shapes=[pltpu.SemaphoreType.DMA((2,))],
    )
    return pl.pallas_call(
        _moe_permute_kernel,
        grid_spec=grid_spec,
        out_shape=jax.ShapeDtypeStruct((n_rows, d), F32),
        compiler_params=_cparams(("arbitrary",)),
        name="moe_permute",
    )(fill_lo, fill_hi, dest3, x)


def _moe_experts_kernel(te_ref, nv_ref, x_ref, wg_ref, wu_ref, wd_ref, y_ref, xb_s, acc_s):
    t = pl.program_id(0)
    f = pl.program_id(1)

    @pl.when(f == 0)
    def _():
        xb_s[...] = x_ref[...].astype(BF16)
        acc_s[...] = jnp.zeros_like(acc_s)

    @pl.when(nv_ref[t] > 0)
    def _():
        acc_s[...] += _swiglu_partial(xb_s[...], wg_ref[0], wu_ref[0], wd_ref[0])

    @pl.when(f == pl.num_programs(1) - 1)
    def _():
        y_ref[...] = acc_s[...]


def _moe_experts(xs, tile_exp, n_valid, tm, w_gate, w_up, w_down):
    n_rows, d = xs.shape
    ff = w_gate.shape[2]
    tf = _ff_tile(ff, 1792)
    grid_spec = pltpu.PrefetchScalarGridSpec(
        num_scalar_prefetch=2,
        grid=(n_rows // tm, ff // tf),
        in_specs=[pl.BlockSpec((tm, d), lambda t, f, te, nv: (t, 0)),
                  pl.BlockSpec((1, d, tf), lambda t, f, te, nv: (te[t], 0, f)),
                  pl.BlockSpec((1, d, tf), lambda t, f, te, nv: (te[t], 0, f)),
                  pl.BlockSpec((1, tf, d), lambda t, f, te, nv: (te[t], f, 0))],
        out_specs=pl.BlockSpec((tm, d), lambda t, f, te, nv: (t, 0)),
        scratch_shapes=[pltpu.VMEM((tm, d), BF16), pltpu.VMEM((tm, d), F32)],
    )
    return pl.pallas_call(
        _moe_experts_kernel,
        grid_spec=grid_spec,
        out_shape=jax.ShapeDtypeStruct((n_rows, d), F32),
        compiler_params=_cparams(("parallel", "arbitrary")),
        name="moe_experts",
    )(tile_exp, n_valid, xs, w_gate.astype(BF16), w_up.astype(BF16), w_down.astype(BF16))


def _moe_combine_kernel(alpha, d0_ref, d1_ref, d0n_ref, d1n_ref, x_ref, w_ref, g_ref, b_ref,
                        ys_hbm, o_ref, yg_s, sem):
    i = pl.program_id(0)
    tm = x_ref.shape[0]
    buf = i % 2

    def slot_row(idx_ref, k, r, b):
        return pltpu.make_async_copy(ys_hbm.at[pl.ds(idx_ref[0, 0, r], 1), :],
                                     yg_s.at[b, k, pl.ds(r, 1), :], sem.at[b])

    def each_row(fn):
        def body(r, c):
            fn(r)
            return c
        lax.fori_loop(0, tm, body, 0, unroll=8)

    def start(i0, i1, b):
        each_row(lambda r: (slot_row(i0, 0, r, b).start(), slot_row(i1, 1, r, b).start()))

    @pl.when(i == 0)
    def _():
        start(d0_ref, d1_ref, 0)

    @pl.when(i + 1 < pl.num_programs(0))
    def _():
        start(d0n_ref, d1n_ref, 1 - buf)

    each_row(lambda r: (slot_row(d0_ref, 0, r, buf).wait(), slot_row(d1_ref, 1, r, buf).wait()))
    w = w_ref[...]
    mix = yg_s[buf, 0] * w[:, 0:1] + yg_s[buf, 1] * w[:, 1:2]
    o_ref[...] = _layer_norm(alpha * x_ref[...] + mix, g_ref[...], b_ref[...])


def _moe_combine_deepnorm(x, ys, dest, wgt, ln_g, ln_b, alpha):
    m, d = x.shape
    tm = _tile(m, 512)
    n = m // tm
    dest3 = dest.reshape(2 * n, 1, tm)
    smem_rows = lambda imap: pl.BlockSpec((1, 1, tm), imap, memory_space=pltpu.SMEM)
    nxt = lambda i: jnp.minimum(i + 1, n - 1)
    return pl.pallas_call(
        functools.partial(_moe_combine_kernel, alpha),
        grid=(n,),
        in_specs=[smem_rows(lambda i: (i, 0, 0)),
                  smem_rows(lambda i: (n + i, 0, 0)),
                  smem_rows(lambda i: (nxt(i), 0, 0)),
                  smem_rows(lambda i: (n + nxt(i), 0, 0)),
                  pl.BlockSpec((tm, d), lambda i: (i, 0)),
                  pl.BlockSpec((tm, LANES), lambda i: (i, 0)),
                  pl.BlockSpec((1, d), lambda i: (0, 0)),
                  pl.BlockSpec((1, d), lambda i: (0, 0)),
                  pl.BlockSpec(memory_space=pl.ANY)],
        out_specs=pl.BlockSpec((tm, d), lambda i: (i, 0)),
        out_shape=jax.ShapeDtypeStruct((m, d), F32),
        scratch_shapes=[pltpu.VMEM((2, 2, tm, d), F32), pltpu.SemaphoreType.DMA((2,))],
        compiler_params=_cparams(("arbitrary",)),
        name="moe_combine_deepnorm",
    )(dest3, dest3, dest3, dest3, x, wgt, ln_g.reshape(1, d), ln_b.reshape(1, d), ys)


def _moe_deepnorm(x, router, w_gate, w_up, w_down, ln_g, ln_b, alpha):
    m = x.shape[0]
    n_exp = router.shape[1]
    tm = _tile(m, MOE_ROWS)
    idx, wgt = _router_top2(x, router)
    dest, tile_exp, n_valid, fill_lo, fill_hi = _moe_plan(idx, m, tm, n_exp)
    xs = _moe_permute(x, dest, fill_lo, fill_hi, 2 * m + n_exp * tm)
    ys = _moe_experts(xs, tile_exp, n_valid, tm, w_gate, w_up, w_down)
    return _moe_combine_deepnorm(x, ys, dest, wgt, ln_g, ln_b, alpha)


def _proj_pairs_kernel(transposed, x_ref, w_ref, o_ref, xb_s):
    @pl.when(pl.program_id(1) == 0)
    def _():
        xb_s[...] = x_ref[...].astype(BF16)

    n_slab = o_ref.shape[0]
    if transposed:
        res = _dot_nt(w_ref[...], xb_s[...])
        for q in range(n_slab):
            o_ref[q] = res[q * LANES:(q + 1) * LANES, :].astype(o_ref.dtype)
    else:
        res = _dot(xb_s[...], w_ref[...])
        for q in range(n_slab):
            o_ref[q] = res[:, q * LANES:(q + 1) * LANES].astype(o_ref.dtype)


def _proj_pairs(x, w, transposed, out_dtype=F32):
    m, d = x.shape
    n = w.shape[1]
    tm = _tile(m, 512)
    tn = _tile(n, 1024)
    n_slab = tn // LANES
    if transposed:
        wb = w.T.astype(BF16)
        w_spec = pl.BlockSpec((tn, d), lambda i, j: (j, 0))
        o_spec = pl.BlockSpec((n_slab, LANES, tm), lambda i, j: (j, 0, i))
        o_shape = jax.ShapeDtypeStruct((n // LANES, LANES, m), out_dtype)
    else:
        wb = w.astype(BF16)
        w_spec = pl.BlockSpec((d, tn), lambda i, j: (0, j))
        o_spec = pl.BlockSpec((n_slab, tm, LANES), lambda i, j: (j, i, 0))
        o_shape = jax.ShapeDtypeStruct((n // LANES, m, LANES), out_dtype)
    return pl.pallas_call(
        functools.partial(_proj_pairs_kernel, transposed),
        grid=(m // tm, n // tn),
        in_specs=[pl.BlockSpec((tm, d), lambda i, j: (i, 0)), w_spec],
        out_specs=o_spec,
        out_shape=o_shape,
        scratch_shapes=[pltpu.VMEM((tm, d), BF16)],
        compiler_params=_cparams(("parallel", "arbitrary")),
        name="proj_pairs_t" if transposed else "proj_pairs",
    )(x, wb)


def _block_mean_kernel(k_ref, o_ref):
    k = k_ref[0]
    nb = k.shape[0] // MOBA_BLOCK
    o_ref[0] = jnp.mean(k.reshape(nb, MOBA_BLOCK, LANES), axis=1)


def _block_means(k_pairs, batch):
    n_pair, m, _ = k_pairs.shape
    seq = m // batch
    nb = seq // MOBA_BLOCK
    return pl.pallas_call(
        _block_mean_kernel,
        grid=(n_pair, batch),
        in_specs=[pl.BlockSpec((1, seq, LANES), lambda p, b: (p, b, 0))],
        out_specs=pl.BlockSpec((1, nb, LANES), lambda p, b: (p, b, 0)),
        out_shape=jax.ShapeDtypeStruct((n_pair, batch * nb, LANES), F32),
        compiler_params=_cparams(("parallel", "parallel")),
        name="moba_block_means",
    )(k_pairs)


def _moba_kernel(scale, qt_ref, k_ref, vt_ref, km_ref, o_ref, neg_s, sca_s, scb_s, m_s, acc_s):
    own = pl.program_id(2)
    blk = MOBA_BLOCK
    n_slab = qt_ref.shape[0]
    nb = km_ref.shape[1]
    row = lax.broadcasted_iota(jnp.int32, (LANES, blk), 0)
    zero_q = jnp.zeros((LANES, blk), F32)
    n_iota = lax.broadcasted_iota(jnp.int32, (nb, blk), 0).astype(F32)
    past = n_iota < own.astype(F32)
    neg_inf = jnp.float32(-jnp.inf)

    streams = [(g, h) for g in range(n_slab) for h in range(2)]
    qh = []
    for s, (g, h) in enumerate(streams):
        in_head = (row < HEAD_DIM) if h == 0 else (row >= HEAD_DIM)
        q_h = jnp.where(in_head, qt_ref[g], zero_q)
        qh.append((q_h * (scale * LOG2E)).astype(BF16))
        gate = _dot_sp(_split(km_ref[g]), _split(q_h))
        gate = jnp.where(past, gate, neg_inf)
        neg = jnp.full((nb, blk), NEG_BIG, F32)
        for _ in range(min(MOBA_TOPK, nb)):
            mx = jnp.max(gate, axis=0, keepdims=True)
            idx = jnp.min(jnp.where(gate == mx, n_iota, float(nb)), axis=0, keepdims=True)
            pick = n_iota == idx
            neg = jnp.where(jnp.logical_and(pick, past), 0.0, neg)
            gate = jnp.where(pick, neg_inf, gate)
        neg_s[s] = neg

    ones_rows = jnp.ones((DEN_ROWS, blk), BF16)

    def block_scores(n, s):
        start = pl.multiple_of(n * blk, blk)
        return _dot(k_ref[streams[s][0], pl.ds(start, blk), :], qh[s])

    def absorb(n, st, bias, s):
        g, h = streams[s]
        start = pl.multiple_of(n * blk, blk)
        mx = m_s[s]
        mblk = jnp.max(st, axis=0, keepdims=True)
        if bias is not None:
            mblk = mblk + bias
        mx_new = jnp.maximum(mx, mblk)
        alpha = jnp.exp2(mx - mx_new)
        shift = mx_new if bias is None else mx_new - bias
        p = jnp.exp2(st - shift).astype(BF16)
        vtb = vt_ref[g, h * HEAD_DIM:(h + 1) * HEAD_DIM, pl.ds(start, blk)]
        m_s[s] = mx_new
        acc_s[s] = alpha * acc_s[s] + _dot(jnp.concatenate([vtb, ones_rows], axis=0), p)

    kpos = lax.broadcasted_iota(jnp.int32, (blk, blk), 0)
    qpos = lax.broadcasted_iota(jnp.int32, (blk, blk), 1)
    causal = kpos <= qpos
    for s in range(len(streams)):
        scb_s[s] = block_scores(own, s)
        m_s[s] = jnp.full((1, blk), neg_inf, F32)
        acc_s[s] = jnp.zeros((HEAD_DIM + DEN_ROWS, blk), F32)
    for s in range(len(streams)):
        sca_s[s] = block_scores(0, s)
        absorb(own, jnp.where(causal, scb_s[s], neg_inf), None, s)

    def body(j, carry):
        n0 = 2 * j
        n1 = n0 + 1
        n2 = jnp.minimum(n0 + 2, nb - 1)
        for s in range(len(streams)):
            scb_s[s] = block_scores(n1, s)
            absorb(n0, sca_s[s], neg_s[s, pl.ds(n0, 1), :], s)
        for s in range(len(streams)):
            sca_s[s] = block_scores(n2, s)
            absorb(n1, scb_s[s], neg_s[s, pl.ds(n1, 1), :], s)
        return carry

    lax.fori_loop(0, own // 2, body, 0)

    @pl.when(own % 2 == 1)
    def _():
        for s in range(len(streams)):
            absorb(own - 1, sca_s[s], neg_s[s, pl.ds(own - 1, 1), :], s)

    for g in range(n_slab):
        a0, a1 = acc_s[2 * g], acc_s[2 * g + 1]
        o_t = jnp.concatenate([a0[:HEAD_DIM] / a0[HEAD_DIM:HEAD_DIM + 1],
                               a1[:HEAD_DIM] / a1[HEAD_DIM:HEAD_DIM + 1]], axis=0)
        o_ref[g] = o_t.T.astype(o_ref.dtype)


def _moba_attention(q_t, k, v_t, k_means, batch):
    n_pair, _, m = q_t.shape
    seq = m // batch
    nb = seq // MOBA_BLOCK
    g = _tile(n_pair, MOBA_SLABS)
    return pl.pallas_call(
        functools.partial(_moba_kernel, HEAD_DIM ** -0.5),
        grid=(n_pair // g, batch, nb),
        in_specs=[pl.BlockSpec((g, LANES, MOBA_BLOCK), lambda p, b, i: (p, 0, b * nb + i)),
                  pl.BlockSpec((g, seq, LANES), lambda p, b, i: (p, b, 0),
                               pipeline_mode=pl.Buffered(1)),
                  pl.BlockSpec((g, LANES, seq), lambda p, b, i: (p, 0, b),
                               pipeline_mode=pl.Buffered(1)),
                  pl.BlockSpec((g, nb, LANES), lambda p, b, i: (p, b, 0))],
        out_specs=pl.BlockSpec((g, MOBA_BLOCK, LANES), lambda p, b, i: (p, b * nb + i, 0)),
        out_shape=jax.ShapeDtypeStruct((n_pair, m, LANES), BF16),
        scratch_shapes=[pltpu.VMEM((2 * g, nb, MOBA_BLOCK), F32),
                        pltpu.VMEM((2 * g, MOBA_BLOCK, MOBA_BLOCK), F32),
                        pltpu.VMEM((2 * g, MOBA_BLOCK, MOBA_BLOCK), F32),
                        pltpu.VMEM((2 * g, 1, MOBA_BLOCK), F32),
                        pltpu.VMEM((2 * g, HEAD_DIM + DEN_ROWS, MOBA_BLOCK), F32)],
        compiler_params=_cparams(("parallel", "parallel", "arbitrary")),
        name="moba_attention",
    )(q_t, k, v_t, k_means)


def kernel(x, rwkv_mu, rwkv_w_rkv, rwkv_w_out, rwkv_decay_w0, rwkv_decay_w1, rwkv_decay_w2, rwkv_iclr_a0, rwkv_iclr_a1, rwkv_iclr_a2, rwkv_vres_v0, rwkv_vres_v1, rwkv_vres_v2, rwkv_gate_g1, rwkv_gate_g2, rwkv_k_k, rwkv_k_a, rwkv_r_k, rwkv_gn_g, rwkv_gn_b, moba_w_k, moba_w_v, moba_w_q, moba_w_o, ffn_w_gate, ffn_w_up, ffn_w_down, moe_router, moe_w_gate, moe_w_up, moe_w_down, ln_g, ln_b):
    batch, seq, d = x.shape
    assert d % (2 * LANES) == 0 and seq % MOBA_BLOCK == 0 and seq % WKV_CHUNK == 0
    depth = ln_g.shape[0]
    n_rwkv = rwkv_mu.shape[0]
    alpha = (2.0 * depth) ** 0.25
    h = x.reshape(batch * seq, d)
    v_first = None
    kv = None
    for layer in range(depth):
        if layer < n_rwkv:
            i = layer
            vres = None if i == 0 else (rwkv_vres_v0[i - 1], rwkv_vres_v1[i - 1], rwkv_vres_v2[i - 1])
            r, e, k, v, a, b, g = _rwkv_proj(
                h, seq, rwkv_mu[i], rwkv_w_rkv[i], rwkv_decay_w1[i], rwkv_iclr_a1[i],
                rwkv_gate_g1[i], rwkv_decay_w2[i], rwkv_iclr_a2[i], rwkv_gate_g2[i],
                rwkv_decay_w0[i], rwkv_iclr_a0[i], rwkv_k_k[i], rwkv_k_a[i],
                vres=vres, v_first=v_first)
            if i == 0:
                v_first = v
            y = _wkv_scan(r, e, k, v, a, b, batch)
            mix = _rwkv_post(y, r, k, v, g, rwkv_r_k[i], rwkv_gn_g[i], rwkv_gn_b[i])
            w_out = rwkv_w_out[i]
        else:
            jdx = layer - n_rwkv
            k_pairs, v_t, k_means = kv
            q_t = _proj_pairs(h, moba_w_q[jdx], transposed=True)
            mix = _moba_attention(q_t, k_pairs, v_t, k_means, batch)
            w_out = moba_w_o[jdx]
        h = _out_proj_deepnorm(mix, w_out, h, ln_g[layer, 0], ln_b[layer, 0], alpha)
        ex = layer // 2
        if layer % 2 == 0:
            h = _ffn_deepnorm(h, _layer_bf16(ffn_w_gate, ex), _layer_bf16(ffn_w_up, ex),
                              _layer_bf16(ffn_w_down, ex), ln_g[layer, 1], ln_b[layer, 1], alpha)
        else:
            h = _moe_deepnorm(h, moe_router[ex], _layer_bf16(moe_w_gate, ex),
                              _layer_bf16(moe_w_up, ex), _layer_bf16(moe_w_down, ex),
                              ln_g[layer, 1], ln_b[layer, 1], alpha)
        if layer == n_rwkv - 1:
            k_pairs = _proj_pairs(h, moba_w_k, transposed=False)
            v_t = _proj_pairs(h, moba_w_v, transposed=True, out_dtype=BF16)
            kv = (k_pairs.astype(BF16), v_t, _block_means(k_pairs, batch))
    return h.reshape(batch, seq, d)
```

```python
import functools
import math

import jax
import jax.numpy as jnp
from jax import lax
from jax.experimental import pallas as pl
from jax.experimental.pallas import tpu as pltpu

HEAD_DIM = 64
LANES = 128
MXU_DEPTH = 256
GN_EPS = 64e-5
LN_EPS = 1e-5
MOBA_BLOCK = 256
MOBA_TOPK = 3
MOE_ROWS = 512
FFN_TILE = 2816
CAST_BLOCK_ELEMS = 1 << 20
WKV_CHUNK = 64
WKV_SLABS = 8
NEG_BIG = -1e30
MOBA_SLABS = 8
DEN_ROWS = 16
LOG2E = 1.4426950408889634

F32 = jnp.float32
BF16 = jnp.bfloat16
VMEM_LIMIT = 56 * 1024 * 1024


def _cparams(sem):
    return pltpu.CompilerParams(dimension_semantics=sem, vmem_limit_bytes=VMEM_LIMIT)


def _dot(a, b):
    return jnp.dot(a, b, preferred_element_type=F32)


def _dot_nt(a, b):
    return lax.dot_general(a, b, (((1,), (1,)), ((), ())), preferred_element_type=F32)


def _split(x):
    hi = x.astype(BF16)
    lo = (x - hi.astype(F32)).astype(BF16)
    return hi, lo


def _dot_sp(a, b, nt=False):
    d = _dot_nt if nt else _dot
    return d(a[0], b[0]) + (d(a[0], b[1]) + d(a[1], b[0]))


def _dot_exact_rhs(a, b_exact, nt=False):
    d = _dot_nt if nt else _dot
    hi, lo = _split(a)
    return d(hi, b_exact) + d(lo, b_exact)


def _sigmoid(x):
    return 1.0 / (1.0 + jnp.exp(-x))


def _layer_norm(y, g, b):
    mu = jnp.mean(y, axis=-1, keepdims=True)
    yc = y - mu
    var = jnp.mean(yc * yc, axis=-1, keepdims=True)
    return yc * lax.rsqrt(var + LN_EPS) * g + b


def _head_blockdiag(n):
    i = jnp.arange(n) // HEAD_DIM
    return (i[:, None] == i[None, :]).astype(BF16)


def _tile(n, want):
    t = min(n, want)
    assert n % t == 0, (n, want)
    return t


def _cast_kernel(w_ref, o_ref):
    o_ref[...] = w_ref[...].astype(o_ref.dtype)


def _layer_bf16(w, layer):
    shape = w.shape[1:]
    c = shape[-1]
    rows = math.prod(shape[:-1])
    tr = rows
    for cand in range(min(rows, CAST_BLOCK_ELEMS // c) // 16, 0, -1):
        if rows % (cand * 16) == 0:
            tr = cand * 16
            break
    out = pl.pallas_call(
        _cast_kernel,
        grid=(rows // tr,),
        in_specs=[pl.BlockSpec((1, tr, c), lambda i: (layer, i, 0))],
        out_specs=pl.BlockSpec((1, tr, c), lambda i: (0, i, 0)),
        out_shape=jax.ShapeDtypeStruct((1, rows, c), BF16),
        compiler_params=_cparams(("parallel",)),
        name="weight_to_bf16",
    )(w.reshape(w.shape[0], rows, c))
    return out.reshape(shape)


def _rwkv_proj_kernel(has_vres, steps_per_seq, *refs):
    if has_vres:
        (x_ref, xp_ref, mu_ref, wrkv_ref, wd1_ref, wa1_ref, wg1_ref, wv1_ref,
         wd2_ref, wa2_ref, wg2_ref, wv2_ref, w0_ref, a0_ref, v0_ref, kk_ref, ka_ref,
         bd_ref, vf_ref,
         r_out, e_out, k_out, v_out, a_out, b_out, g_out,
         xm_s, hd_s, ha_s, hg_s, hv_s) = refs
    else:
        (x_ref, xp_ref, mu_ref, wrkv_ref, wd1_ref, wa1_ref, wg1_ref,
         wd2_ref, wa2_ref, wg2_ref, w0_ref, a0_ref, kk_ref, ka_ref,
         bd_ref,
         r_out, e_out, k_out, v_out, a_out, b_out, g_out,
         xm_s, hd_s, ha_s, hg_s) = refs
    i = pl.program_id(0)
    j = pl.program_id(1)

    @pl.when(j == 0)
    def _():
        x = x_ref[...]
        tm = x.shape[0]
        prev_row = jnp.where(i % steps_per_seq == 0, 0.0, xp_ref[7:8, :])
        rolled = pltpu.roll(x, 1, axis=0)
        row = lax.broadcasted_iota(jnp.int32, (tm, 1), 0)
        x_prev = jnp.where(row == 0, prev_row, rolled)
        xx = x_prev - x
        for c in range(3):
            xm_s[c] = (x + xx * mu_ref[c:c + 1, :]).astype(BF16)
        xw = (x + xx * mu_ref[3:4, :]).astype(BF16)
        xa = (x + xx * mu_ref[4:5, :]).astype(BF16)
        xg = (x + xx * mu_ref[5:6, :]).astype(BF16)
        hd_s[...] = jnp.tanh(_dot(xw, wd1_ref[...])).astype(BF16)
        ha_s[...] = _dot(xa, wa1_ref[...]).astype(BF16)
        hg_s[...] = _sigmoid(_dot(xg, wg1_ref[...])).astype(BF16)
        if has_vres:
            hv_s[...] = _dot(xm_s[2], wv1_ref[...]).astype(BF16)

    r = _dot(xm_s[0], wrkv_ref[0])
    k = _dot(xm_s[1], wrkv_ref[1])
    v = _dot(xm_s[2], wrkv_ref[2])
    z = w0_ref[...] + _dot(hd_s[...], wd2_ref[...])
    nz = -z
    softplus = jnp.maximum(nz, 0.0) + jnp.log(1.0 + jnp.exp(-jnp.abs(nz)))
    e = jnp.exp(-softplus - 0.5)
    a = _sigmoid(a0_ref[...] + _dot(ha_s[...], wa2_ref[...]))
    g = _dot(hg_s[...], wg2_ref[...])
    n_slab = r.shape[1] // LANES
    if has_vres:
        vf = jnp.concatenate([vf_ref[q] for q in range(n_slab)], axis=1)
        v = v + (vf - v) * _sigmoid(v0_ref[...] + _dot(hv_s[...], wv2_ref[...]))
    kk = k * kk_ref[...]
    ss = _dot_exact_rhs(kk * kk, bd_ref[...])
    kk = kk * lax.rsqrt(jnp.maximum(ss, 1e-24))
    k = k * (1.0 + (a - 1.0) * ka_ref[...])
    for q in range(n_slab):
        sl = slice(q * LANES, (q + 1) * LANES)
        r_out[q] = r[:, sl].astype(r_out.dtype)
        e_out[q] = e[:, sl]
        k_out[q] = k[:, sl].astype(k_out.dtype)
        v_out[q] = v[:, sl].astype(v_out.dtype)
        a_out[q] = (-kk)[:, sl].astype(a_out.dtype)
        b_out[q] = (kk * a)[:, sl].astype(b_out.dtype)
        g_out[q] = g[:, sl].astype(g_out.dtype)


def _rwkv_proj(x, seq_len, mu, w_rkv, wd1, wa1, wg1, wd2, wa2, wg2, w0, a0, k_k, k_a,
               vres=None, v_first=None):
    m, d = x.shape
    tm = _tile(seq_len, 512)
    tn = _tile(d, 256)
    n_slab = tn // LANES
    has_vres = vres is not None
    row = lambda a: a.reshape(1, d)
    full = lambda a: pl.BlockSpec(a.shape, lambda i, j: (0,) * a.ndim)
    colblk = lambda rows: pl.BlockSpec((rows, tn), lambda i, j: (0, j))
    bd = _head_blockdiag(tn)
    wd1, wa1, wg1 = wd1.astype(BF16), wa1.astype(BF16), wg1.astype(BF16)
    wd2, wa2, wg2 = wd2.astype(BF16), wa2.astype(BF16), wg2.astype(BF16)
    w_rkv = w_rkv.astype(BF16)
    args = [x, x, mu, w_rkv, wd1, wa1, wg1]
    specs = [pl.BlockSpec((tm, d), lambda i, j: (i, 0)),
             pl.BlockSpec((8, d), lambda i, j: (jnp.maximum(i * (tm // 8) - 1, 0), 0)),
             full(mu),
             pl.BlockSpec((3, d, tn), lambda i, j: (0, 0, j)),
             full(wd1), full(wa1), full(wg1)]
    if has_vres:
        v0, wv1, wv2 = vres
        wv1, wv2 = wv1.astype(BF16), wv2.astype(BF16)
        args += [wv1]
        specs += [full(wv1)]
    args += [wd2, wa2, wg2]
    specs += [colblk(wd2.shape[0]), colblk(wa2.shape[0]), colblk(wg2.shape[0])]
    if has_vres:
        args += [wv2]
        specs += [colblk(wv2.shape[0])]
    args += [row(w0), row(a0)]
    specs += [colblk(1), colblk(1)]
    if has_vres:
        args += [row(v0)]
        specs += [colblk(1)]
    args += [row(k_k), row(k_a), bd]
    specs += [colblk(1), colblk(1), full(bd)]
    slab_spec = pl.BlockSpec((n_slab, tm, LANES), lambda i, j: (j, i, 0))
    if has_vres:
        args += [v_first]
        specs += [slab_spec]
    slab = lambda dt: jax.ShapeDtypeStruct((d // LANES, m, LANES), dt)
    scratch = [pltpu.VMEM((3, tm, d), BF16),
               pltpu.VMEM((tm, wd1.shape[1]), BF16),
               pltpu.VMEM((tm, wa1.shape[1]), BF16),
               pltpu.VMEM((tm, wg1.shape[1]), BF16)]
    if has_vres:
        scratch.append(pltpu.VMEM((tm, wv1.shape[1]), BF16))
    return pl.pallas_call(
        functools.partial(_rwkv_proj_kernel, has_vres, seq_len // tm),
        grid=(m // tm, d // tn),
        in_specs=specs,
        out_specs=[slab_spec] * 7,
        out_shape=[slab(BF16), slab(F32)] + [slab(BF16)] * 5,
        scratch_shapes=scratch,
        compiler_params=_cparams(("parallel", "arbitrary")),
        name="rwkv_proj",
    )(*args)


def _wkv_kernel(r_ref, e_ref, k_ref, v_ref, a_ref, b_ref, y_ref, s_ref):
    c = WKV_CHUNK
    n_slab, n_batch, tb, _ = r_ref.shape
    n_chunks = tb // c

    @pl.when(pl.program_id(1) == 0)
    def _():
        s_ref[...] = jnp.zeros_like(s_ref)

    lane = lax.broadcasted_iota(jnp.int32, (c, LANES), 1)
    t_idx = lax.broadcasted_iota(jnp.int32, (c, LANES), 0)
    head0 = lane < HEAD_DIM
    s_idx = jnp.bitwise_and(lane, HEAD_DIM - 1)
    strict = s_idx < t_idx
    incl = s_idx <= t_idx
    rr = lax.broadcasted_iota(jnp.int32, (LANES, LANES), 0)
    cc = lax.broadcasted_iota(jnp.int32, (LANES, LANES), 1)
    same_head = (rr < HEAD_DIM) == (cc < HEAD_DIM)

    def bd(x):
        z = jnp.zeros_like(x)
        return jnp.concatenate([jnp.where(head0, x, z), jnp.where(head0, z, x)], axis=0)

    def cat(a, b, axis=0):
        return jnp.concatenate([a, b], axis=axis)

    def chunk(ci, carry):
        stages = [one_slab(ci, p, bb) for p in range(n_slab) for bb in range(n_batch)]
        while stages:
            stages = [g for g in stages if next(g, "done") != "done"]
        return carry

    def one_slab(ci, p, bb):
        sl = pl.ds(pl.multiple_of(ci * c, c), c)
        r = r_ref[p, bb, sl, :]
        e = e_ref[p, bb, sl, :]
        k = k_ref[p, bb, sl, :]
        v = v_ref[p, bb, sl, :]
        a = a_ref[p, bb, sl, :]
        b = b_ref[p, bb, sl, :]
        s0 = s_ref[p * n_batch + bb]
        cum = _cumsum_rows(e, t_idx)
        yield
        tot = cum[c - 1:c, :]
        ar = cat(a * jnp.exp(e - cum), r * jnp.exp(-cum)).astype(BF16)
        ec = jnp.exp(cum)
        bt = (b * ec).astype(BF16)
        kt = (k * ec).astype(BF16)
        eh = jnp.exp(cum - tot)
        bk = cat(b * eh, k * eh).astype(BF16)
        bdv = bd(v.astype(BF16))
        gram = _dot_nt(ar, cat(bd(bt), bd(kt)))
        yield
        zero = jnp.zeros((c, LANES), F32)
        l_ab = jnp.where(strict, gram[:c, :LANES], zero)
        n_ak = jnp.where(strict, gram[:c, LANES:], zero)
        m_rb = jnp.where(incl, gram[c:, :LANES], zero)
        m_rk = jnp.where(incl, gram[c:, LANES:], zero)
        xy = _dot_nt(ar, s0.astype(BF16)) + _dot(cat(n_ak, m_rk).astype(BF16), bdv)
        x = xy[:c]
        y0 = xy[c:]
        yield
        lp = l_ab.astype(BF16)
        n_steps = int(math.log2(c))
        for step in range(n_steps):
            bdx = bd(x.astype(BF16))
            if step + 1 < n_steps:
                t = _dot(lp, cat(bdx, bd(lp), axis=1))
                x = x + t[:, :LANES]
                lp = t[:, LANES:].astype(BF16)
            else:
                x = x + _dot(lp, bdx)
            yield
        y_ref[p, bb, sl, :] = y0 + _dot(m_rb.astype(BF16), bd(x.astype(BF16)))
        uv_t = cat(x, v.astype(F32)).T.astype(BF16)
        upd = _dot(uv_t, bk)
        s_ref[p * n_batch + bb] = (s0 * jnp.exp(-tot)
                                   + jnp.where(same_head, upd, jnp.zeros_like(upd)))
        yield

    lax.fori_loop(0, n_chunks, chunk, 0)


def _cumsum_rows(e, row_idx):
    cum = e
    shift = 1
    while shift < e.shape[0]:
        cum = cum + jnp.where(row_idx >= shift, pltpu.roll(cum, shift, axis=0), 0.0)
        shift *= 2
    return cum


def _wkv_scan(r, e, k, v, a, b, batch):
    n_pair, m, _ = r.shape
    seq = m // batch
    tb = _tile(seq, 256)
    n_slab = _tile(n_pair, WKV_SLABS)
    view = lambda t: t.reshape(n_pair, batch, seq, LANES)
    spec = pl.BlockSpec((n_slab, batch, tb, LANES), lambda p, t: (p, 0, t, 0))
    y = pl.pallas_call(
        _wkv_kernel,
        grid=(n_pair // n_slab, seq // tb),
        in_specs=[spec] * 6,
        out_specs=spec,
        out_shape=jax.ShapeDtypeStruct((n_pair, batch, seq, LANES), F32),
        scratch_shapes=[pltpu.VMEM((n_slab * batch, LANES, LANES), F32)],
        compiler_params=_cparams(("parallel", "arbitrary")),
        name="wkv_scan",
    )(view(r), view(e), view(k), view(v), view(a), view(b))
    return y.reshape(n_pair, m, LANES)


def _rwkv_post_kernel(y_ref, r_ref, k_ref, v_ref, g_ref, rk_ref, gg_ref, gb_ref, bd_ref, z_ref):
    y = y_ref[0]
    bd2 = bd_ref[...]
    bdm = bd2[:LANES, :LANES]
    inv_n = 1.0 / HEAD_DIM
    rk = r_ref[0].astype(F32) * k_ref[0].astype(F32) * rk_ref[0]
    sums = _dot_exact_rhs(jnp.concatenate([y, rk], axis=1), bd2)
    yc = y - sums[:, :LANES] * inv_n
    var = _dot_exact_rhs(yc * yc, bdm) * inv_n
    yn = yc * lax.rsqrt(var + GN_EPS) * gg_ref[0] + gb_ref[0]
    bonus = sums[:, LANES:] * v_ref[0].astype(F32)
    z_ref[0] = ((yn + bonus) * g_ref[0].astype(F32)).astype(z_ref.dtype)


def _rwkv_post(y, r, k, v, g, r_k, gn_g, gn_b):
    n_pair, m, _ = y.shape
    tm = _tile(m, 2048)
    spec = pl.BlockSpec((1, tm, LANES), lambda p, i: (p, i, 0))
    pspec = pl.BlockSpec((1, 1, LANES), lambda p, i: (p, 0, 0))
    bd = _head_blockdiag(2 * LANES)
    slab = lambda a: a.reshape(n_pair, 1, LANES)
    return pl.pallas_call(
        _rwkv_post_kernel,
        grid=(n_pair, m // tm),
        in_specs=[spec] * 5 + [pspec] * 3 + [pl.BlockSpec(bd.shape, lambda p, i: (0, 0))],
        out_specs=spec,
        out_shape=jax.ShapeDtypeStruct((n_pair, m, LANES), BF16),
        compiler_params=_cparams(("parallel", "parallel")),
        name="rwkv_post",
    )(y, r, k, v, g, slab(r_k), slab(gn_g), slab(gn_b), bd)


def _out_proj_kernel(alpha, z_ref, w_ref, x_ref, g_ref, b_ref, o_ref):
    per = w_ref.shape[1] // LANES
    acc = None
    for c in range(w_ref.shape[0]):
        zc = jnp.concatenate([z_ref[c * per + q] for q in range(per)], axis=1)
        part = _dot(zc, w_ref[c])
        acc = part if acc is None else acc + part
    o_ref[...] = _layer_norm(alpha * x_ref[...] + acc, g_ref[...], b_ref[...])


def _out_proj_deepnorm(z, w, x, ln_g, ln_b, alpha):
    n_pair, m, _ = z.shape
    d = w.shape[1]
    tm = _tile(m, 1024)
    kc = _tile(d, MXU_DEPTH)
    w3 = w.astype(BF16).reshape(d // kc, kc, d)
    return pl.pallas_call(
        functools.partial(_out_proj_kernel, alpha),
        grid=(m // tm,),
        in_specs=[pl.BlockSpec((n_pair, tm, LANES), lambda i: (0, i, 0)),
                  pl.BlockSpec(w3.shape, lambda i: (0, 0, 0)),
                  pl.BlockSpec((tm, d), lambda i: (i, 0)),
                  pl.BlockSpec((1, d), lambda i: (0, 0)),
                  pl.BlockSpec((1, d), lambda i: (0, 0))],
        out_specs=pl.BlockSpec((tm, d), lambda i: (i, 0)),
        out_shape=jax.ShapeDtypeStruct((m, d), F32),
        compiler_params=_cparams(("parallel",)),
        name="out_proj_deepnorm",
    )(z, w3, x, ln_g.reshape(1, d), ln_b.reshape(1, d))


def _ff_tile(ff, want):
    for cand in range(min(ff, want) // LANES, 0, -1):
        if ff % (cand * LANES) == 0:
            return cand * LANES
    return ff


def _swiglu_partial(xb, wg, wu, wd):
    h1 = _dot(xb, wg)
    h2 = _dot(xb, wu)
    act = h1 * _sigmoid(h1) * h2
    return _dot(act.astype(BF16), wd)


def _ffn_kernel(alpha, x_ref, wg_ref, wu_ref, wd_ref, g_ref, b_ref, o_ref, xb_s, acc_s):
    f = pl.program_id(1)

    @pl.when(f == 0)
    def _():
        xb_s[...] = x_ref[...].astype(BF16)
        acc_s[...] = jnp.zeros_like(acc_s)

    acc_s[...] += _swiglu_partial(xb_s[...], wg_ref[...], wu_ref[...], wd_ref[...])

    @pl.when(f == pl.num_programs(1) - 1)
    def _():
        o_ref[...] = _layer_norm(alpha * x_ref[...] + acc_s[...], g_ref[...], b_ref[...])


def _ffn_deepnorm(x, w_gate, w_up, w_down, ln_g, ln_b, alpha):
    m, d = x.shape
    ff = w_gate.shape[1]
    tm = _tile(m, 512)
    tf = _ff_tile(ff, FFN_TILE)
    wmode = dict(pipeline_mode=pl.Buffered(1)) if tf == ff else {}
    return pl.pallas_call(
        functools.partial(_ffn_kernel, alpha),
        grid=(m // tm, ff // tf),
        in_specs=[pl.BlockSpec((tm, d), lambda i, f: (i, 0)),
                  pl.BlockSpec((d, tf), lambda i, f: (0, f), **wmode),
                  pl.BlockSpec((d, tf), lambda i, f: (0, f), **wmode),
                  pl.BlockSpec((tf, d), lambda i, f: (f, 0), **wmode),
                  pl.BlockSpec((1, d), lambda i, f: (0, 0)),
                  pl.BlockSpec((1, d), lambda i, f: (0, 0))],
        out_specs=pl.BlockSpec((tm, d), lambda i, f: (i, 0)),
        out_shape=jax.ShapeDtypeStruct((m, d), F32),
        scratch_shapes=[pltpu.VMEM((tm, d), BF16), pltpu.VMEM((tm, d), F32)],
        compiler_params=_cparams(("parallel", "arbitrary")),
        name="ffn_deepnorm",
    )(x, w_gate.astype(BF16), w_up.astype(BF16), w_down.astype(BF16),
      ln_g.reshape(1, d), ln_b.reshape(1, d))


def _router_kernel(n_exp, x_ref, w_ref, idx_ref, wgt_ref):
    logits = _dot_sp(_split(x_ref[...]), _split(w_ref[...]))
    lane = lax.broadcasted_iota(jnp.int32, logits.shape, 1).astype(F32)
    neg_inf = jnp.float32(-jnp.inf)
    lg = jnp.where(lane < n_exp, logits, neg_inf)
    m1 = jnp.max(lg, axis=1, keepdims=True)
    i1 = jnp.min(jnp.where(lg == m1, lane, float(LANES)), axis=1, keepdims=True)
    lg2 = jnp.where(lane == i1, neg_inf, lg)
    m2 = jnp.max(lg2, axis=1, keepdims=True)
    i2 = jnp.min(jnp.where(lg2 == m2, lane, float(LANES)), axis=1, keepdims=True)
    e2 = jnp.exp(m2 - m1)
    den = 1.0 + e2
    idx_ref[...] = jnp.where(lane == 0.0, i1, jnp.where(lane == 1.0, i2, 0.0)).astype(jnp.int32)
    wgt_ref[...] = jnp.where(lane == 0.0, 1.0 / den, jnp.where(lane == 1.0, e2 / den, 0.0))


def _router_top2(x, router):
    m, d = x.shape
    n_exp = router.shape[1]
    tm = _tile(m, 512)
    w = jnp.pad(router, ((0, 0), (0, LANES - n_exp)))
    out_spec = pl.BlockSpec((tm, LANES), lambda i: (i, 0))
    return pl.pallas_call(
        functools.partial(_router_kernel, n_exp),
        grid=(m // tm,),
        in_specs=[pl.BlockSpec((tm, d), lambda i: (i, 0)),
                  pl.BlockSpec((d, LANES), lambda i: (0, 0))],
        out_specs=[out_spec, out_spec],
        out_shape=[jax.ShapeDtypeStruct((m, LANES), jnp.int32),
                   jax.ShapeDtypeStruct((m, LANES), F32)],
        compiler_params=_cparams(("parallel",)),
        name="moe_router",
    )(x, w)


def _moe_plan(idx, m, tm, n_exp):
    n_ent = 2 * m
    expert = jnp.concatenate([idx[:, 0], idx[:, 1]])
    onehot = (expert[:, None] == jnp.arange(n_exp, dtype=jnp.int32)[None, :]).astype(jnp.int32)
    csum = jnp.cumsum(onehot, axis=0)
    counts = csum[-1]
    rank = jnp.sum(csum * onehot, axis=1) - 1
    padded = ((counts + tm - 1) // tm) * tm
    ends = jnp.cumsum(padded)
    starts = ends - padded
    dest = jnp.sum(starts[None, :] * onehot, axis=1) + rank
    n_tiles = n_ent // tm + n_exp
    tile_start = jnp.arange(n_tiles, dtype=jnp.int32) * tm
    tile_exp = jnp.minimum(jnp.sum((tile_start[:, None] >= ends[None, :]).astype(jnp.int32), axis=1),
                           n_exp - 1)
    n_valid = jnp.clip(starts[tile_exp] + counts[tile_exp] - tile_start, 0, tm)
    fill_lo = jnp.concatenate([starts + counts, ends[-1:]])
    fill_hi = jnp.concatenate([ends, jnp.full((1,), n_tiles * tm, jnp.int32)])
    return dest, tile_exp, n_valid, fill_lo, fill_hi


def _moe_permute_kernel(lo_ref, hi_ref, dest_ref, x_ref, xs_hbm, sem):
    i = pl.program_id(0)
    te = dest_ref.shape[2]

    def entry_row(r):
        return pltpu.make_async_copy(x_ref.at[pl.ds(r, 1), :],
                                     xs_hbm.at[pl.ds(dest_ref[0, 0, r], 1), :], sem.at[0])

    def filler_row(r):
        return pltpu.make_async_copy(x_ref.at[pl.ds(0, 1), :], xs_hbm.at[pl.ds(r, 1), :], sem.at[1])

    def each(lo, hi, fn, unroll=1):
        def body(r, c):
            fn(r)
            return c
        lax.fori_loop(lo, hi, body, 0, unroll=unroll)

    each(0, te, lambda r: entry_row(r).start(), unroll=8)

    @pl.when(i == 0)
    def _():
        for e in range(lo_ref.shape[0]):
            each(lo_ref[e], hi_ref[e], lambda r: filler_row(r).start())
        for e in range(lo_ref.shape[0]):
            each(lo_ref[e], hi_ref[e], lambda r: filler_row(r).wait())

    each(0, te, lambda r: entry_row(r).wait(), unroll=8)


def _moe_permute(x, dest, fill_lo, fill_hi, n_rows):
    m, d = x.shape
    te = _tile(m, 512)
    n_x = m // te
    dest3 = dest.reshape(-1, 1, te)
    grid_spec = pltpu.PrefetchScalarGridSpec(
        num_scalar_prefetch=2,
        grid=(dest3.shape[0],),
        in_specs=[pl.BlockSpec((1, 1, te), lambda i, lo, hi: (i, 0, 0), memory_space=pltpu.SMEM),
                  pl.BlockSpec((te, d), lambda i, lo, hi: (i % n_x, 0))],
        out_specs=pl.BlockSpec(memory_space=pl.ANY),
        scratch_shapes=[pltpu.SemaphoreType.DMA((2,))],
    )
    return pl.pallas_call(
        _moe_permute_kernel,
        grid_spec=grid_spec,
        out_shape=jax.ShapeDtypeStruct((n_rows, d), F32),
        compiler_params=_cparams(("arbitrary",)),
        name="moe_permute",
    )(fill_lo, fill_hi, dest3, x)


def _moe_experts_kernel(te_ref, nv_ref, x_ref, wg_ref, wu_ref, wd_ref, y_ref, xb_s, acc_s):
    t = pl.program_id(0)
    f = pl.program_id(1)

    @pl.when(f == 0)
    def _():
        xb_s[...] = x_ref[...].astype(BF16)
        acc_s[...] = jnp.zeros_like(acc_s)

    @pl.when(nv_ref[t] > 0)
    def _():
        acc_s[...] += _swiglu_partial(xb_s[...], wg_ref[0], wu_ref[0], wd_ref[0])

    @pl.when(f == pl.num_programs(1) - 1)
    def _():
        y_ref[...] = acc_s[...]


def _moe_experts(xs, tile_exp, n_valid, tm, w_gate, w_up, w_down):
    n_rows, d = xs.shape
    ff = w_gate.shape[2]
    tf = _ff_tile(ff, 1792)
    grid_spec = pltpu.PrefetchScalarGridSpec(
        num_scalar_prefetch=2,
        grid=(n_rows // tm, ff // tf),
        in_specs=[pl.BlockSpec((tm, d), lambda t, f, te, nv: (t, 0)),
                  pl.BlockSpec((1, d, tf), lambda t, f, te, nv: (te[t], 0, f)),
                  pl.BlockSpec((1, d, tf), lambda t, f, te, nv: (te[t], 0, f)),
                  pl.BlockSpec((1, tf, d), lambda t, f, te, nv: (te[t], f, 0))],
        out_specs=pl.BlockSpec((tm, d), lambda t, f, te, nv: (t, 0)),
        scratch_shapes=[pltpu.VMEM((tm, d), BF16), pltpu.VMEM((tm, d), F32)],
    )
    return pl.pallas_call(
        _moe_experts_kernel,
        grid_spec=grid_spec,
        out_shape=jax.ShapeDtypeStruct((n_rows, d), F32),
        compiler_params=_cparams(("parallel", "arbitrary")),
        name="moe_experts",
    )(tile_exp, n_valid, xs, w_gate.astype(BF16), w_up.astype(BF16), w_down.astype(BF16))


def _moe_combine_kernel(alpha, d0_ref, d1_ref, d0n_ref, d1n_ref, x_ref, w_ref, g_ref, b_ref,
                        ys_hbm, o_ref, yg_s, sem):
    i = pl.program_id(0)
    tm = x_ref.shape[0]
    buf = i % 2

    def slot_row(idx_ref, k, r, b):
        return pltpu.make_async_copy(ys_hbm.at[pl.ds(idx_ref[0, 0, r], 1), :],
                                     yg_s.at[b, k, pl.ds(r, 1), :], sem.at[b])

    def each_row(fn):
        def body(r, c):
            fn(r)
            return c
        lax.fori_loop(0, tm, body, 0, unroll=8)

    def start(i0, i1, b):
        each_row(lambda r: (slot_row(i0, 0, r, b).start(), slot_row(i1, 1, r, b).start()))

    @pl.when(i == 0)
    def _():
        start(d0_ref, d1_ref, 0)

    @pl.when(i + 1 < pl.num_programs(0))
    def _():
        start(d0n_ref, d1n_ref, 1 - buf)

    each_row(lambda r: (slot_row(d0_ref, 0, r, buf).wait(), slot_row(d1_ref, 1, r, buf).wait()))
    w = w_ref[...]
    mix = yg_s[buf, 0] * w[:, 0:1] + yg_s[buf, 1] * w[:, 1:2]
    o_ref[...] = _layer_norm(alpha * x_ref[...] + mix, g_ref[...], b_ref[...])


def _moe_combine_deepnorm(x, ys, dest, wgt, ln_g, ln_b, alpha):
    m, d = x.shape
    tm = _tile(m, 512)
    n = m // tm
    dest3 = dest.reshape(2 * n, 1, tm)
    smem_rows = lambda imap: pl.BlockSpec((1, 1, tm), imap, memory_space=pltpu.SMEM)
    nxt = lambda i: jnp.minimum(i + 1, n - 1)
    return pl.pallas_call(
        functools.partial(_moe_combine_kernel, alpha),
        grid=(n,),
        in_specs=[smem_rows(lambda i: (i, 0, 0)),
                  smem_rows(lambda i: (n + i, 0, 0)),
                  smem_rows(lambda i: (nxt(i), 0, 0)),
                  smem_rows(lambda i: (n + nxt(i), 0, 0)),
                  pl.BlockSpec((tm, d), lambda i: (i, 0)),
                  pl.BlockSpec((tm, LANES), lambda i: (i, 0)),
                  pl.BlockSpec((1, d), lambda i: (0, 0)),
                  pl.BlockSpec((1, d), lambda i: (0, 0)),
                  pl.BlockSpec(memory_space=pl.ANY)],
        out_specs=pl.BlockSpec((tm, d), lambda i: (i, 0)),
        out_shape=jax.ShapeDtypeStruct((m, d), F32),
        scratch_shapes=[pltpu.VMEM((2, 2, tm, d), F32), pltpu.SemaphoreType.DMA((2,))],
        compiler_params=_cparams(("arbitrary",)),
        name="moe_combine_deepnorm",
    )(dest3, dest3, dest3, dest3, x, wgt, ln_g.reshape(1, d), ln_b.reshape(1, d), ys)


def _moe_deepnorm(x, router, w_gate, w_up, w_down, ln_g, ln_b, alpha):
    m = x.shape[0]
    n_exp = router.shape[1]
    tm = _tile(m, MOE_ROWS)
    idx, wgt = _router_top2(x, router)
    dest, tile_exp, n_valid, fill_lo, fill_hi = _moe_plan(idx, m, tm, n_exp)
    xs = _moe_permute(x, dest, fill_lo, fill_hi, 2 * m + n_exp * tm)
    ys = _moe_experts(xs, tile_exp, n_valid, tm, w_gate, w_up, w_down)
    return _moe_combine_deepnorm(x, ys, dest, wgt, ln_g, ln_b, alpha)


def _proj_pairs_kernel(transposed, x_ref, w_ref, o_ref, xb_s):
    @pl.when(pl.program_id(1) == 0)
    def _():
        xb_s[...] = x_ref[...].astype(BF16)

    n_slab = o_ref.shape[0]
    if transposed:
        res = _dot_nt(w_ref[...], xb_s[...])
        for q in range(n_slab):
            o_ref[q] = res[q * LANES:(q + 1) * LANES, :].astype(o_ref.dtype)
    else:
        res = _dot(xb_s[...], w_ref[...])
        for q in range(n_slab):
            o_ref[q] = res[:, q * LANES:(q + 1) * LANES].astype(o_ref.dtype)


def _proj_pairs(x, w, transposed, out_dtype=F32):
    m, d = x.shape
    n = w.shape[1]
    tm = _tile(m, 512)
    tn = _tile(n, 1024)
    n_slab = tn // LANES
    if transposed:
        wb = w.T.astype(BF16)
        w_spec = pl.BlockSpec((tn, d), lambda i, j: (j, 0))
        o_spec = pl.BlockSpec((n_slab, LANES, tm), lambda i, j: (j, 0, i))
        o_shape = jax.ShapeDtypeStruct((n // LANES, LANES, m), out_dtype)
    else:
        wb = w.astype(BF16)
        w_spec = pl.BlockSpec((d, tn), lambda i, j: (0, j))
        o_spec = pl.BlockSpec((n_slab, tm, LANES), lambda i, j: (j, i, 0))
        o_shape = jax.ShapeDtypeStruct((n // LANES, m, LANES), out_dtype)
    return pl.pallas_call(
        functools.partial(_proj_pairs_kernel, transposed),
        grid=(m // tm, n // tn),
        in_specs=[pl.BlockSpec((tm, d), lambda i, j: (i, 0)), w_spec],
        out_specs=o_spec,
        out_shape=o_shape,
        scratch_shapes=[pltpu.VMEM((tm, d), BF16)],
        compiler_params=_cparams(("parallel", "arbitrary")),
        name="proj_pairs_t" if transposed else "proj_pairs",
    )(x, wb)


def _block_mean_kernel(k_ref, o_ref):
    k = k_ref[0]
    nb = k.shape[0] // MOBA_BLOCK
    o_ref[0] = jnp.mean(k.reshape(nb, MOBA_BLOCK, LANES), axis=1)


def _block_means(k_pairs, batch):
    n_pair, m, _ = k_pairs.shape
    seq = m // batch
    nb = seq // MOBA_BLOCK
    return pl.pallas_call(
        _block_mean_kernel,
        grid=(n_pair, batch),
        in_specs=[pl.BlockSpec((1, seq, LANES), lambda p, b: (p, b, 0))],
        out_specs=pl.BlockSpec((1, nb, LANES), lambda p, b: (p, b, 0)),
        out_shape=jax.ShapeDtypeStruct((n_pair, batch * nb, LANES), F32),
        compiler_params=_cparams(("parallel", "parallel")),
        name="moba_block_means",
    )(k_pairs)


def _moba_kernel(scale, qt_ref, k_ref, vt_ref, km_ref, o_ref, neg_s, sca_s, scb_s, m_s, acc_s,
                 qh_s):
    own = pl.program_id(2)
    blk = MOBA_BLOCK
    n_slab = qt_ref.shape[0]
    nb = km_ref.shape[1]
    row = lax.broadcasted_iota(jnp.int32, (LANES, blk), 0)
    zero_q = jnp.zeros((LANES, blk), F32)
    n_iota = lax.broadcasted_iota(jnp.int32, (nb, blk), 0).astype(F32)
    past = n_iota < own.astype(F32)
    neg_inf = jnp.float32(-jnp.inf)

    streams = [(g, h) for g in range(n_slab) for h in range(2)]
    for s, (g, h) in enumerate(streams):
        in_head = (row < HEAD_DIM) if h == 0 else (row >= HEAD_DIM)
        q_h = jnp.where(in_head, qt_ref[g], zero_q)
        qh_s[s] = (q_h * (scale * LOG2E)).astype(BF16)
        gate = _dot_sp(_split(km_ref[g]), _split(q_h))
        gate = jnp.where(past, gate, neg_inf)
        neg = jnp.full((nb, blk), NEG_BIG, F32)
        for _ in range(min(MOBA_TOPK, nb)):
            mx = jnp.max(gate, axis=0, keepdims=True)
            idx = jnp.min(jnp.where(gate == mx, n_iota, float(nb)), axis=0, keepdims=True)
            pick = n_iota == idx
            neg = jnp.where(jnp.logical_and(pick, past), 0.0, neg)
            gate = jnp.where(pick, neg_inf, gate)
        neg_s[s] = neg

    ones_rows = jnp.ones((DEN_ROWS, blk), BF16)

    def block_scores(n, s):
        start = pl.multiple_of(n * blk, blk)
        return _dot(k_ref[streams[s][0], pl.ds(start, blk), :], qh_s[s])

    def absorb(n, st, bias, s):
        g, h = streams[s]
        start = pl.multiple_of(n * blk, blk)
        mx = m_s[s]
        mblk = jnp.max(st, axis=0, keepdims=True)
        if bias is not None:
            mblk = mblk + bias
        mx_new = jnp.maximum(mx, mblk)
        alpha = jnp.exp2(mx - mx_new)
        shift = mx_new if bias is None else mx_new - bias
        p = jnp.exp2(st - shift).astype(BF16)
        vtb = vt_ref[g, h * HEAD_DIM:(h + 1) * HEAD_DIM, pl.ds(start, blk)]
        m_s[s] = mx_new
        acc_s[s] = alpha * acc_s[s] + _dot(jnp.concatenate([vtb, ones_rows], axis=0), p)

    kpos = lax.broadcasted_iota(jnp.int32, (blk, blk), 0)
    qpos = lax.broadcasted_iota(jnp.int32, (blk, blk), 1)
    causal = kpos <= qpos
    for s in range(len(streams)):
        scb_s[s] = block_scores(own, s)
        m_s[s] = jnp.full((1, blk), neg_inf, F32)
        acc_s[s] = jnp.zeros((HEAD_DIM + DEN_ROWS, blk), F32)
    for s in range(len(streams)):
        sca_s[s] = block_scores(0, s)
        absorb(own, jnp.where(causal, scb_s[s], neg_inf), None, s)

    def body(j, carry):
        n0 = 2 * j
        n1 = n0 + 1
        n2 = jnp.minimum(n0 + 2, nb - 1)
        for s in range(len(streams)):
            scb_s[s] = block_scores(n1, s)
            absorb(n0, sca_s[s], neg_s[s, pl.ds(n0, 1), :], s)
        for s in range(len(streams)):
            sca_s[s] = block_scores(n2, s)
            absorb(n1, scb_s[s], neg_s[s, pl.ds(n1, 1), :], s)
        return carry

    lax.fori_loop(0, own // 2, body, 0)

    @pl.when(own % 2 == 1)
    def _():
        for s in range(len(streams)):
            absorb(own - 1, sca_s[s], neg_s[s, pl.ds(own - 1, 1), :], s)

    for g in range(n_slab):
        a0, a1 = acc_s[2 * g], acc_s[2 * g + 1]
        o_t = jnp.concatenate([a0[:HEAD_DIM] / a0[HEAD_DIM:HEAD_DIM + 1],
                               a1[:HEAD_DIM] / a1[HEAD_DIM:HEAD_DIM + 1]], axis=0)
        o_ref[g] = o_t.T.astype(o_ref.dtype)


def _moba_attention(q_t, k, v_t, k_means, batch):
    n_pair, _, m = q_t.shape
    seq = m // batch
    nb = seq // MOBA_BLOCK
    g = _tile(n_pair, MOBA_SLABS)
    return pl.pallas_call(
        functools.partial(_moba_kernel, HEAD_DIM ** -0.5),
        grid=(n_pair // g, batch, nb),
        in_specs=[pl.BlockSpec((g, LANES, MOBA_BLOCK), lambda p, b, i: (p, 0, b * nb + i)),
                  pl.BlockSpec((g, seq, LANES), lambda p, b, i: (p, b, 0),
                               pipeline_mode=pl.Buffered(1)),
                  pl.BlockSpec((g, LANES, seq), lambda p, b, i: (p, 0, b),
                               pipeline_mode=pl.Buffered(1)),
                  pl.BlockSpec((g, nb, LANES), lambda p, b, i: (p, b, 0))],
        out_specs=pl.BlockSpec((g, MOBA_BLOCK, LANES), lambda p, b, i: (p, b * nb + i, 0)),
        out_shape=jax.ShapeDtypeStruct((n_pair, m, LANES), BF16),
        scratch_shapes=[pltpu.VMEM((2 * g, nb, MOBA_BLOCK), F32),
                        pltpu.VMEM((2 * g, MOBA_BLOCK, MOBA_BLOCK), F32),
                        pltpu.VMEM((2 * g, MOBA_BLOCK, MOBA_BLOCK), F32),
                        pltpu.VMEM((2 * g, 1, MOBA_BLOCK), F32),
                        pltpu.VMEM((2 * g, HEAD_DIM + DEN_ROWS, MOBA_BLOCK), F32),
                        pltpu.VMEM((2 * g, LANES, MOBA_BLOCK), BF16)],
        compiler_params=_cparams(("parallel", "parallel", "arbitrary")),
        name="moba_attention",
    )(q_t, k, v_t, k_means)


def kernel(x, rwkv_mu, rwkv_w_rkv, rwkv_w_out, rwkv_decay_w0, rwkv_decay_w1, rwkv_decay_w2, rwkv_iclr_a0, rwkv_iclr_a1, rwkv_iclr_a2, rwkv_vres_v0, rwkv_vres_v1, rwkv_vres_v2, rwkv_gate_g1, rwkv_gate_g2, rwkv_k_k, rwkv_k_a, rwkv_r_k, rwkv_gn_g, rwkv_gn_b, moba_w_k, moba_w_v, moba_w_q, moba_w_o, ffn_w_gate, ffn_w_up, ffn_w_down, moe_router, moe_w_gate, moe_w_up, moe_w_down, ln_g, ln_b):
    batch, seq, d = x.shape
    assert d % (2 * LANES) == 0 and seq % MOBA_BLOCK == 0 and seq % WKV_CHUNK == 0
    depth = ln_g.shape[0]
    n_rwkv = rwkv_mu.shape[0]
    alpha = (2.0 * depth) ** 0.25
    h = x.reshape(batch * seq, d)
    v_first = None
    kv = None
    for layer in range(depth):
        if layer < n_rwkv:
            i = layer
            vres = None if i == 0 else (rwkv_vres_v0[i - 1], rwkv_vres_v1[i - 1], rwkv_vres_v2[i - 1])
            r, e, k, v, a, b, g = _rwkv_proj(
                h, seq, rwkv_mu[i], rwkv_w_rkv[i], rwkv_decay_w1[i], rwkv_iclr_a1[i],
                rwkv_gate_g1[i], rwkv_decay_w2[i], rwkv_iclr_a2[i], rwkv_gate_g2[i],
                rwkv_decay_w0[i], rwkv_iclr_a0[i], rwkv_k_k[i], rwkv_k_a[i],
                vres=vres, v_first=v_first)
            if i == 0:
                v_first = v
            y = _wkv_scan(r, e, k, v, a, b, batch)
            mix = _rwkv_post(y, r, k, v, g, rwkv_r_k[i], rwkv_gn_g[i], rwkv_gn_b[i])
            w_out = rwkv_w_out[i]
        else:
            jdx = layer - n_rwkv
            k_pairs, v_t, k_means = kv
            q_t = _proj_pairs(h, moba_w_q[jdx], transposed=True)
            mix = _moba_attention(q_t, k_pairs, v_t, k_means, batch)
            w_out = moba_w_o[jdx]
        h = _out_proj_deepnorm(mix, w_out, h, ln_g[layer, 0], ln_b[layer, 0], alpha)
        ex = layer // 2
        if layer % 2 == 0:
            h = _ffn_deepnorm(h, _layer_bf16(ffn_w_gate, ex), _layer_bf16(ffn_w_up, ex),
                              _layer_bf16(ffn_w_down, ex), ln_g[layer, 1], ln_b[layer, 1], alpha)
        else:
            h = _moe_deepnorm(h, moe_router[ex], _layer_bf16(moe_w_gate, ex),
                              _layer_bf16(moe_w_up, ex), _layer_bf16(moe_w_down, ex),
                              ln_g[layer, 1], ln_b[layer, 1], alpha)
        if layer == n_rwkv - 1:
            k_pairs = _proj_pairs(h, moba_w_k, transposed=False)
            v_t = _proj_pairs(h, moba_w_v, transposed=True, out_dtype=BF16)
            kv = (k_pairs.astype(BF16), v_t, _block_means(k_pairs, batch))
    return h.reshape(batch, seq, d)
```
